```python
import jax, jax.numpy as jnp
from jax import lax
import numpy as np

D_MODEL = 1024
BATCH = 8
SEQ = 4096
DEPTH = 4

D_MIX = D_MODEL
ATT_HEAD_DIM = 64
ATT_WIDTH = D_MIX // 4
ATT_HEADS = ATT_WIDTH // ATT_HEAD_DIM
DILATED_BRANCHES = ((128, 1), (512, 4), (2048, 16))
ATT_BLOCK = 128
POOL_WINDOWS = (2, 4, 8, 16)
POOL_WIDTH = D_MIX // 4
POOL_GROUP_DIM = POOL_WIDTH // len(POOL_WINDOWS)
MLSTM_WIDTH = D_MIX - ATT_WIDTH - POOL_WIDTH
MLSTM_HEADS = 4
MLSTM_HEAD_DIM = MLSTM_WIDTH // MLSTM_HEADS
MLSTM_CONV = 4
MLSTM_CHUNK = 64
IN_WIDTH = 3 * ATT_WIDTH + POOL_WIDTH + 2 * MLSTM_WIDTH + 2 * MLSTM_HEADS
D_FF = ((8 * D_MODEL // 3 + 255) // 256) * 256
MACARON_WEIGHT = 0.5
EPS = 1e-6
NEG = -1e30

kernel_name = "hybrid_dilated_pool_mlstm_macaron"


def rmsnorm(x, w):
    xf = x.astype(jnp.float32)
    y = xf * lax.rsqrt(jnp.mean(xf * xf, axis=-1, keepdims=True) + EPS)
    return (y * w.astype(jnp.float32)).astype(x.dtype)


def swiglu(u, w_up, w_down):
    g, v = jnp.split(u @ w_up, 2, axis=-1)
    return (jax.nn.silu(g) * v) @ w_down


def dilated_branch(q, k, v, window, dilation):
    B, H, S, Dh = q.shape
    Q = ATT_BLOCK
    span = window // dilation
    L = S // dilation
    nb = -(-L // Q)
    Lp = nb * Q

    def strided(t):
        t = t.reshape(B, H, L, dilation, Dh).transpose(0, 1, 3, 2, 4)
        return jnp.pad(t, ((0, 0), (0, 0), (0, 0), (0, Lp - L), (0, 0)))

    def key_blocks(t):
        t = jnp.pad(t, ((0, 0), (0, 0), (0, 0), (Q, 0), (0, 0))).reshape(B, H, dilation, nb + 1, Q, Dh)
        return jnp.concatenate([t[:, :, :, :-1], t[:, :, :, 1:]], axis=4)

    qb = strided(q).reshape(B, H, dilation, nb, Q, Dh)
    kb = key_blocks(strided(k))
    vb = key_blocks(strided(v))
    qi = jnp.arange(nb)[:, None, None] * Q + jnp.arange(Q)[None, :, None]
    ki = jnp.arange(nb)[:, None, None] * Q - Q + jnp.arange(2 * Q)[None, None, :]
    dist = qi - ki
    mask = (dist >= 0) & (dist <= span) & (ki >= 0)
    s = jnp.einsum('bhrnqd,bhrnkd->bhrnqk', qb, kb) * (Dh ** -0.5)
    s = jnp.where(mask, s, NEG)
    m = jnp.max(s, axis=-1)
    p = jnp.exp(s - m[..., None])
    l = jnp.sum(p, axis=-1)
    o = jnp.einsum('bhrnqk,bhrnkd->bhrnqd', p, vb) / l[..., None]

    def unstrided(t):
        tail = t.shape[5:]
        t = t.reshape((B, H, dilation, Lp) + tail)[:, :, :, :L]
        t = jnp.moveaxis(t, 2, 3)
        return t.reshape((B, H, S) + tail)

    return unstrided(o), unstrided(m), unstrided(l)


def dilated_attention(q, k, v):
    q, k, v = (t.astype(jnp.float32) for t in (q, k, v))
    outs = [dilated_branch(q, k, v, w, d) for (w, d) in DILATED_BRANCHES]
    o_all = jnp.stack([o for o, _, _ in outs])
    m_all = jnp.stack([m for _, m, _ in outs])
    l_all = jnp.stack([l for _, _, l in outs])
    wts = l_all * jnp.exp(m_all - jnp.max(m_all, axis=0, keepdims=True))
    wts = wts / jnp.sum(wts, axis=0, keepdims=True)
    return jnp.sum(wts[..., None] * o_all, axis=0)


def pool_mixer(p, pool_w, pool_scale):
    B, S, C = p.shape
    G = POOL_GROUP_DIM
    pf = p.astype(jnp.float32)
    cs = jnp.cumsum(pf, axis=1)
    count = jnp.arange(1, S + 1, dtype=jnp.float32)[None, :, None]
    outs = []
    for g, w in enumerate(POOL_WINDOWS):
        c_g = cs[:, :, g * G:(g + 1) * G]
        lag = jnp.pad(c_g[:, :S - w], ((0, 0), (w, 0), (0, 0)))
        outs.append((c_g - lag) / jnp.minimum(count, w) - pf[:, :, g * G:(g + 1) * G])
    d = jnp.stack(outs, axis=2)
    y = jnp.einsum('bsgc,gce->bsge', d, pool_w.astype(jnp.float32)).reshape(B, S, C)
    return (y * pool_scale).astype(p.dtype)


def causal_conv(x, w, b):
    K = w.shape[0]
    S = x.shape[1]
    xp = jnp.pad(x, ((0, 0), (K - 1, 0), (0, 0)))
    y = xp[:, 0:S] * w[0]
    for t in range(1, K):
        y = y + xp[:, t:t + S] * w[t]
    return y + b


def mlstm_chunkwise(q, k, v, i_pre, f_pre):
    B, H, S, dh = q.shape
    L = MLSTM_CHUNK
    NC = S // L
    qc = q.astype(jnp.float32).reshape(B, H, NC, L, dh) * (dh ** -0.5)
    kc = k.astype(jnp.float32).reshape(B, H, NC, L, dh)
    vc = v.astype(jnp.float32).reshape(B, H, NC, L, dh)
    ic = i_pre.astype(jnp.float32).reshape(B, H, NC, L)
    logf = jax.nn.log_sigmoid(f_pre.astype(jnp.float32)).reshape(B, H, NC, L)
    b = jnp.cumsum(logf, axis=-1)
    b_last = b[..., -1]
    causal = jnp.tril(jnp.ones((L, L), dtype=bool))
    logD = jnp.where(causal, b[..., :, None] - b[..., None, :] + ic[..., None, :], NEG)
    a = b_last[..., None] - b + ic
    a_max = jnp.max(a, axis=-1)
    wa = jnp.exp(a - a_max[..., None])
    dC = jnp.einsum('bhcl,bhcld,bhcle->bhcde', wa, kc, vc)
    dn = jnp.einsum('bhcl,bhcld->bhcd', wa, kc)

    def step(carry, xs):
        C, n, m = carry
        bl, am, dC_c, dn_c = xs
        m_new = jnp.maximum(bl + m, am)
        decay = jnp.exp(bl + m - m_new)
        inj = jnp.exp(am - m_new)
        C_new = decay[..., None, None] * C + inj[..., None, None] * dC_c
        n_new = decay[..., None] * n + inj[..., None] * dn_c
        return (C_new, n_new, m_new), (C, n, m)

    init = (jnp.zeros((B, H, dh, dh), jnp.float32), jnp.zeros((B, H, dh), jnp.float32),
            jnp.zeros((B, H), jnp.float32))
    xs = (jnp.moveaxis(b_last, 2, 0), jnp.moveaxis(a_max, 2, 0), jnp.moveaxis(dC, 2, 0), jnp.moveaxis(dn, 2, 0))
    _, (C_prev, n_prev, m_prev) = lax.scan(step, init, xs)
    C_prev = jnp.moveaxis(C_prev, 0, 2)
    n_prev = jnp.moveaxis(n_prev, 0, 2)
    m_prev = jnp.moveaxis(m_prev, 0, 2)
    inter_log = b + m_prev[..., None]
    m_t = jnp.maximum(inter_log, jnp.max(logD, axis=-1))
    inter_w = jnp.exp(inter_log - m_t)
    sw = jnp.einsum('bhcld,bhcsd->bhcls', qc, kc) * jnp.exp(logD - m_t[..., None])
    num = inter_w[..., None] * jnp.einsum('bhcld,bhcde->bhcle', qc, C_prev) + jnp.einsum('bhcls,bhcse->bhcle', sw, vc)
    den = inter_w * jnp.einsum('bhcld,bhcd->bhcl', qc, n_prev) + jnp.sum(sw, axis=-1)
    h = num / jnp.maximum(jnp.abs(den), jnp.exp(-m_t))[..., None]
    return h.reshape(B, H, S, dh)


def hybrid_mixer(u, w_in, w_out, pool_w, pool_scale, conv_w, conv_b, qkv_w, gate_b, norm_w, skip):
    B, S, _ = u.shape
    widths = [ATT_WIDTH, ATT_WIDTH, ATT_WIDTH, POOL_WIDTH, MLSTM_WIDTH, MLSTM_WIDTH, MLSTM_HEADS, MLSTM_HEADS]
    idx = []
    acc = 0
    for wd in widths[:-1]:
        acc += wd
        idx.append(acc)
    aq, ak, av, p_in, xm, og, ig, fg = jnp.split(u @ w_in, idx, axis=-1)

    def heads(t, n):
        return t.reshape(B, S, n, -1).transpose(0, 2, 1, 3)

    att = dilated_attention(heads(aq, ATT_HEADS), heads(ak, ATT_HEADS), heads(av, ATT_HEADS))
    att = att.transpose(0, 2, 1, 3).reshape(B, S, ATT_WIDTH).astype(u.dtype)
    pool = pool_mixer(p_in, pool_w, pool_scale)
    xc = jax.nn.silu(causal_conv(xm, conv_w, conv_b))
    xch = xc.reshape(B, S, MLSTM_HEADS, MLSTM_HEAD_DIM)
    xmh = xm.reshape(B, S, MLSTM_HEADS, MLSTM_HEAD_DIM)
    q = jnp.einsum('bshd,hde->bhse', xch, qkv_w[0])
    k = jnp.einsum('bshd,hde->bhse', xch, qkv_w[1])
    v = jnp.einsum('bshd,hde->bhse', xmh, qkv_w[2])
    i_pre = (ig + gate_b[0]).transpose(0, 2, 1)
    f_pre = (fg + gate_b[1]).transpose(0, 2, 1)
    h = mlstm_chunkwise(q, k, v, i_pre, f_pre)
    h = h * jax.nn.sigmoid(heads(og, MLSTM_HEADS).astype(jnp.float32))
    h = rmsnorm(h, norm_w.reshape(MLSTM_HEADS, 1, MLSTM_HEAD_DIM))
    h = h.transpose(0, 2, 1, 3).reshape(B, S, MLSTM_WIDTH).astype(u.dtype) + skip * xc
    mixed = jnp.concatenate([att, pool, h], axis=-1)
    return mixed @ w_out


def sublayer(h, mod_j, pre_w, post_w, fn, weight):
    shift, scale, gate = mod_j[:, 0], mod_j[:, 1], mod_j[:, 2]
    u = rmsnorm(h, pre_w) * (1 + scale) + shift
    return h + weight * gate * rmsnorm(fn(u), post_w)


def setup_inputs(seed: int = 0) -> dict:
    key = jax.random.key(seed)
    ks = jax.random.split(key, 20)

    def nrm(k, shape, s):
        return jax.random.normal(k, shape, jnp.float32) * s

    x = nrm(ks[0], (BATCH, SEQ, D_MODEL), 1.0)
    c = nrm(ks[1], (BATCH, D_MODEL), 1.0)
    ada_w = nrm(ks[2], (DEPTH, D_MODEL, 9 * D_MODEL), D_MODEL ** -0.5)
    ada_b = nrm(ks[3], (DEPTH, 9 * D_MODEL), 0.02)
    pre_norm_w = 1.0 + nrm(ks[4], (DEPTH, 3, D_MODEL), 0.05)
    post_norm_w = 1.0 + nrm(ks[5], (DEPTH, 3, D_MODEL), 0.05)
    ffn_up = nrm(ks[6], (DEPTH, 2, D_MODEL, 2 * D_FF), D_MODEL ** -0.5)
    ffn_down = nrm(ks[7], (DEPTH, 2, D_FF, D_MODEL), D_FF ** -0.5)
    mix_in_w = nrm(ks[8], (DEPTH, D_MODEL, IN_WIDTH), D_MODEL ** -0.5)
    mix_out_w = nrm(ks[9], (DEPTH, D_MIX, D_MODEL), D_MIX ** -0.5)
    pool_w = nrm(ks[10], (DEPTH, len(POOL_WINDOWS), POOL_GROUP_DIM, POOL_GROUP_DIM), POOL_GROUP_DIM ** -0.5)
    pool_scale = 1.0 + nrm(ks[11], (DEPTH, POOL_WIDTH), 0.1)
    mlstm_conv_w = nrm(ks[12], (DEPTH, MLSTM_CONV, MLSTM_WIDTH), MLSTM_CONV ** -0.5)
    mlstm_conv_b = nrm(ks[13], (DEPTH, MLSTM_WIDTH), 0.02)
    mlstm_qkv_w = nrm(ks[14], (DEPTH, 3, MLSTM_HEADS, MLSTM_HEAD_DIM, MLSTM_HEAD_DIM), MLSTM_HEAD_DIM ** -0.5)
    i_b = nrm(ks[15], (DEPTH, MLSTM_HEADS), 0.1)
    f_b = jnp.linspace(3.0, 6.0, MLSTM_HEADS, dtype=jnp.float32)[None, :] + nrm(ks[16], (DEPTH, MLSTM_HEADS), 0.1)
    mlstm_gate_b = jnp.stack([i_b, f_b], axis=1)
    mlstm_norm_w = 1.0 + nrm(ks[17], (DEPTH, MLSTM_WIDTH), 0.05)
    mlstm_skip = 1.0 + nrm(ks[18], (DEPTH, MLSTM_WIDTH), 0.1)
    return {"x": x, "c": c, "ada_w": ada_w, "ada_b": ada_b, "pre_norm_w": pre_norm_w,
            "post_norm_w": post_norm_w, "ffn_up": ffn_up, "ffn_down": ffn_down, "mix_in_w": mix_in_w,
            "mix_out_w": mix_out_w, "pool_w": pool_w, "pool_scale": pool_scale,
            "mlstm_conv_w": mlstm_conv_w, "mlstm_conv_b": mlstm_conv_b, "mlstm_qkv_w": mlstm_qkv_w,
            "mlstm_gate_b": mlstm_gate_b, "mlstm_norm_w": mlstm_norm_w, "mlstm_skip": mlstm_skip}


def reference(x, c, ada_w, ada_b, pre_norm_w, post_norm_w, ffn_up, ffn_down, mix_in_w, mix_out_w,
              pool_w, pool_scale, mlstm_conv_w, mlstm_conv_b, mlstm_qkv_w, mlstm_gate_b, mlstm_norm_w,
              mlstm_skip):
    B = x.shape[0]
    c_act = jax.nn.silu(c)
    h = x
    for l in range(DEPTH):
        mod = (c_act @ ada_w[l] + ada_b[l]).reshape(B, 3, 3, 1, D_MODEL)
        h = sublayer(h, mod[:, 0], pre_norm_w[l, 0], post_norm_w[l, 0],
                     lambda u: swiglu(u, ffn_up[l, 0], ffn_down[l, 0]), MACARON_WEIGHT)
        h = sublayer(h, mod[:, 1], pre_norm_w[l, 1], post_norm_w[l, 1],
                     lambda u: hybrid_mixer(u, mix_in_w[l], mix_out_w[l], pool_w[l], pool_scale[l],
                                            mlstm_conv_w[l], mlstm_conv_b[l], mlstm_qkv_w[l],
                                            mlstm_gate_b[l], mlstm_norm_w[l], mlstm_skip[l]), 1.0)
        h = sublayer(h, mod[:, 2], pre_norm_w[l, 2], post_norm_w[l, 2],
                     lambda u: swiglu(u, ffn_up[l, 1], ffn_down[l, 1]), MACARON_WEIGHT)
    return h
```

```python
import functools

import jax
import jax.numpy as jnp
from jax import lax
from jax.experimental import pallas as pl
from jax.experimental.pallas import tpu as pltpu

F32 = jnp.float32
BF16 = jnp.bfloat16

ATT_HEADS = 4
ATT_HEAD_DIM = 64
ATT_WIDTH = ATT_HEADS * ATT_HEAD_DIM
DILATIONS = (1, 4, 16)
ATT_SPAN = 128
POOL_WINDOWS = (2, 4, 8, 16)
POOL_GROUP_DIM = 64
POOL_WIDTH = len(POOL_WINDOWS) * POOL_GROUP_DIM
MLSTM_HEADS = 4
MLSTM_HEAD_DIM = 128
MLSTM_WIDTH = MLSTM_HEADS * MLSTM_HEAD_DIM
MLSTM_CONV = 4
MACARON_WEIGHT = 0.5
EPS = 1e-6
NEG = -1e30

LANES = 128
SUBLANES = 8
VMEM_LIMIT_BYTES = 56 * 1024 * 1024

ROW_TILE = 1024
FF_TILE = 256
ATT_TILE = 2048
ATT_BLOCK = 128
SEQ_CHUNK = 256
POOL_HIST = 16
CONV_HIST = 8
GATE_LANES = LANES


def _params(sem):
    return pltpu.CompilerParams(dimension_semantics=sem, vmem_limit_bytes=VMEM_LIMIT_BYTES)


def _rms(x):
    return x * lax.rsqrt(jnp.mean(x * x, axis=-1, keepdims=True) + EPS)


def _adaln_kernel(c_ref, w_ref, b_ref, o_ref):
    c = c_ref[...]
    c_act = (c * jax.nn.sigmoid(c)).astype(BF16)
    o_ref[...] = jnp.dot(c_act, w_ref[...].astype(BF16), preferred_element_type=F32) + b_ref[...]


def _adaln(c, ada_w, ada_b):
    depth, d, n = ada_w.shape
    b = c.shape[0]
    tn = n // 8
    return pl.pallas_call(
        _adaln_kernel,
        out_shape=jax.ShapeDtypeStruct((depth, b, n), F32),
        grid=(depth, n // tn),
        in_specs=[
            pl.BlockSpec((b, d), lambda l, j: (0, 0)),
            pl.BlockSpec((None, d, tn), lambda l, j: (l, 0, j)),
            pl.BlockSpec((None, 1, tn), lambda l, j: (l, 0, j)),
        ],
        out_specs=pl.BlockSpec((None, b, tn), lambda l, j: (l, 0, j)),
        compiler_params=_params(("parallel", "parallel")),
        name="adaln",
    )(c, ada_w, ada_b.reshape(depth, 1, n))


def _ffn_kernel(x_ref, mod_ref, prew_ref, postw_ref, wg_ref, wv_ref, wd_ref, o_ref, u_ref, acc_ref, *, weight):
    k = pl.program_id(1)

    @pl.when(k == 0)
    def _():
        u = _rms(x_ref[...]) * prew_ref[...] * (1.0 + mod_ref[1:2, :]) + mod_ref[0:1, :]
        u_ref[...] = u.astype(BF16)
        acc_ref[...] = jnp.zeros_like(acc_ref)

    u = u_ref[...]
    g = jnp.dot(u, wg_ref[...], preferred_element_type=F32)
    v = jnp.dot(u, wv_ref[...], preferred_element_type=F32)
    a = (g * jax.nn.sigmoid(g) * v).astype(BF16)
    acc_ref[...] += jnp.dot(a, wd_ref[...], preferred_element_type=F32)

    @pl.when(k == pl.num_programs(1) - 1)
    def _():
        y = _rms(acc_ref[...]) * postw_ref[...]
        o_ref[...] = x_ref[...] + (weight * mod_ref[2:3, :]) * y


def _ffn(h, mod3, pre_w, post_w, w_up, w_down, layer, slot, seq):
    t, d = h.shape
    ff = w_down.shape[2]
    tm, tf = ROW_TILE, FF_TILE
    per_batch = seq // tm
    nf = ff // tf
    return pl.pallas_call(
        functools.partial(_ffn_kernel, weight=MACARON_WEIGHT),
        out_shape=jax.ShapeDtypeStruct((t, d), F32),
        grid=(t // tm, nf),
        in_specs=[
            pl.BlockSpec((tm, d), lambda i, k: (i, 0)),
            pl.BlockSpec((None, 3, d), lambda i, k: (i // per_batch, 0, 0)),
            pl.BlockSpec((1, d), lambda i, k: (0, 0)),
            pl.BlockSpec((1, d), lambda i, k: (0, 0)),
            pl.BlockSpec((None, None, d, tf), lambda i, k: (layer, slot, 0, k)),
            pl.BlockSpec((None, None, d, tf), lambda i, k: (layer, slot, 0, k + nf)),
            pl.BlockSpec((None, None, tf, d), lambda i, k: (layer, slot, k, 0)),
        ],
        out_specs=pl.BlockSpec((tm, d), lambda i, k: (i, 0)),
        scratch_shapes=[pltpu.VMEM((tm, d), BF16), pltpu.VMEM((tm, d), F32)],
        compiler_params=_params(("parallel", "arbitrary")),
        name="ffn",
    )(h, mod3, pre_w.reshape(1, d), post_w.reshape(1, d), w_up, w_up, w_down)


def _inproj_kernel(x_ref, mod_ref, prew_ref, wqkv_ref, wrest_ref, wgate_ref, qkv_ref, rest_ref, gate_ref):
    u = _rms(x_ref[...]) * prew_ref[...] * (1.0 + mod_ref[1:2, :]) + mod_ref[0:1, :]
    u = u.astype(BF16)
    qkv_ref[...] = jnp.dot(u, wqkv_ref[...], preferred_element_type=F32).astype(BF16)
    rest_ref[...] = jnp.dot(u, wrest_ref[...], preferred_element_type=F32).astype(BF16)
    gate_ref[...] = jnp.dot(u, wgate_ref[...], preferred_element_type=F32)


def _inproj(h, mod3, pre_w, w_qkv, w_rest, w_gate, seq):
    t, d = h.shape
    tm = ROW_TILE
    per_batch = seq // tm
    n_qkv, n_rest, n_gate = w_qkv.shape[1], w_rest.shape[1], w_gate.shape[1]
    full = lambda i: (0, 0)
    return pl.pallas_call(
        _inproj_kernel,
        out_shape=(
            jax.ShapeDtypeStruct((t, n_qkv), BF16),
            jax.ShapeDtypeStruct((t, n_rest), BF16),
            jax.ShapeDtypeStruct((t, n_gate), F32),
        ),
        grid=(t // tm,),
        in_specs=[
            pl.BlockSpec((tm, d), lambda i: (i, 0)),
            pl.BlockSpec((None, 3, d), lambda i: (i // per_batch, 0, 0)),
            pl.BlockSpec((1, d), full),
            pl.BlockSpec((d, n_qkv), full),
            pl.BlockSpec((d, n_rest), full),
            pl.BlockSpec((d, n_gate), full),
        ],
        out_specs=(
            pl.BlockSpec((tm, n_qkv), lambda i: (i, 0)),
            pl.BlockSpec((tm, n_rest), lambda i: (i, 0)),
            pl.BlockSpec((tm, n_gate), lambda i: (i, 0)),
        ),
        compiler_params=_params(("parallel",)),
        name="inproj",
    )(h, mod3, pre_w.reshape(1, d), w_qkv, w_rest, w_gate)


QKV_W = 3 * ATT_WIDTH


def _attn_block(q, kwin, vwin, bias, head_of_lane):
    nq = q.shape[0]
    q32 = q.astype(F32) * (ATT_HEAD_DIM ** -0.5)
    lhs = jnp.concatenate(
        [jnp.where(head_of_lane == h, q32, 0.0) for h in range(ATT_HEADS)], axis=0).astype(BF16)
    s = lax.dot_general(lhs, kwin, (((1,), (1,)), ((), ())), preferred_element_type=F32) + bias
    m = jnp.max(s, axis=-1, keepdims=True)
    p = jnp.exp(s - m)
    l = jnp.sum(p, axis=-1, keepdims=True)
    pv = jnp.dot(p.astype(BF16), vwin, preferred_element_type=F32)
    num = jnp.zeros((nq, ATT_WIDTH), F32)
    mb = jnp.zeros((nq, ATT_WIDTH), F32)
    lb = jnp.zeros((nq, ATT_WIDTH), F32)
    for h in range(ATT_HEADS):
        rows = slice(h * nq, (h + 1) * nq)
        sel = head_of_lane == h
        num = jnp.where(sel, pv[rows], num)
        mb = jnp.where(sel, m[rows], mb)
        lb = jnp.where(sel, l[rows], lb)
    return num, mb, lb


def _attn_kernel(c1_ref, h1_ref, c4_ref, h4_ref, c16_ref, h16_ref, o_ref,
                 bias_ref, run_ref, tmp_ref):
    t = pl.program_id(1)
    nq = ATT_BLOCK
    head_of_lane = lax.broadcasted_iota(jnp.int32, (nq, ATT_WIDTH), 1) >> (ATT_HEAD_DIM.bit_length() - 1)

    qi = lax.broadcasted_iota(jnp.int32, (ATT_HEADS * nq, 2 * nq), 0) & (nq - 1)
    ki = lax.broadcasted_iota(jnp.int32, (ATT_HEADS * nq, 2 * nq), 1)
    ok = (ki >= qi) & (ki <= qi + ATT_SPAN)
    bias_ref[0] = jnp.where(ok, 0.0, NEG)
    bias_ref[1] = jnp.where(ok & (ki >= nq), 0.0, NEG)
    first = jnp.where(t == 0, 1, 0)

    def put(dst_ref, vals, start, stride):
        for q_idx, val in enumerate(vals):
            for half in range(2):
                piece = val[:, half * LANES:(half + 1) * LANES]
                if stride == 1:
                    dst_ref[q_idx, half, pl.ds(start, nq), :] = piece
                else:
                    dst_ref[q_idx, half, pl.ds(start, nq, stride=stride), :] = piece

    def branch(cur_ref, halo_ref, dil, dst_ref):
        blocks = ATT_TILE // dil // nq
        for r in range(dil):
            base = r * QKV_W
            qs = slice(base, base + ATT_WIDTH)
            ks = slice(base + ATT_WIDTH, base + 2 * ATT_WIDTH)
            vs = slice(base + 2 * ATT_WIDTH, base + 3 * ATT_WIDTH)

            kwin = jnp.concatenate([halo_ref[:, ks], cur_ref[0:nq, ks]], axis=0)
            vwin = jnp.concatenate([halo_ref[:, vs], cur_ref[0:nq, vs]], axis=0)
            put(dst_ref, _attn_block(cur_ref[0:nq, qs], kwin, vwin, bias_ref[first], head_of_lane), r, dil)

            if blocks > 1:
                def body(i, carry):
                    row0 = pl.multiple_of(i * nq, nq)
                    prev0 = pl.multiple_of((i - 1) * nq, nq)
                    kw = cur_ref[pl.ds(prev0, 2 * nq), ks]
                    vw = cur_ref[pl.ds(prev0, 2 * nq), vs]
                    res = _attn_block(cur_ref[pl.ds(row0, nq), qs], kw, vw, bias_ref[0], head_of_lane)
                    put(dst_ref, res, i * (nq * dil) + r, dil)
                    return carry
                lax.fori_loop(1, blocks, body, 0)

    rows = 256

    def merge(final):
        def body(c, carry):
            sl = pl.ds(pl.multiple_of(c * rows, rows), rows)
            for half in range(2):
                n_r, m_r, l_r = (run_ref[q, half, sl, :] for q in range(3))
                n_t, m_t, l_t = (tmp_ref[q, half, sl, :] for q in range(3))
                mx = jnp.maximum(m_r, m_t)
                e_r = jnp.exp(m_r - mx)
                e_t = jnp.exp(m_t - mx)
                n_new = e_r * n_r + e_t * n_t
                l_new = e_r * l_r + e_t * l_t
                if final:
                    o_ref[sl, half * LANES:(half + 1) * LANES] = (n_new / l_new).astype(o_ref.dtype)
                else:
                    run_ref[0, half, sl, :] = n_new
                    run_ref[1, half, sl, :] = mx
                    run_ref[2, half, sl, :] = l_new
            return carry
        lax.fori_loop(0, ATT_TILE // rows, body, 0)

    branch(c1_ref, h1_ref, DILATIONS[0], run_ref)
    branch(c4_ref, h4_ref, DILATIONS[1], tmp_ref)
    merge(False)
    branch(c16_ref, h16_ref, DILATIONS[2], tmp_ref)
    merge(True)


def _attention(qkv, batch, seq):
    tiles = seq // ATT_TILE
    in_specs = []
    operands = []
    for dil in DILATIONS:
        view = qkv.reshape(batch, seq // dil, dil * QKV_W)
        cur_rows = ATT_TILE // dil
        per_tile = cur_rows // ATT_BLOCK
        in_specs.append(pl.BlockSpec((None, cur_rows, dil * QKV_W), lambda b, t: (b, t, 0)))
        in_specs.append(pl.BlockSpec(
            (None, ATT_BLOCK, dil * QKV_W),
            functools.partial(lambda b, t, per_tile: (b, jnp.maximum(t * per_tile - 1, 0), 0), per_tile=per_tile)))
        operands += [view, view]
    return pl.pallas_call(
        _attn_kernel,
        out_shape=jax.ShapeDtypeStruct((batch, seq, ATT_WIDTH), BF16),
        grid=(batch, tiles),
        in_specs=in_specs,
        out_specs=pl.BlockSpec((None, ATT_TILE, ATT_WIDTH), lambda b, t: (b, t, 0)),
        scratch_shapes=[
            pltpu.VMEM((2, ATT_HEADS * ATT_BLOCK, 2 * ATT_BLOCK), F32),
            pltpu.VMEM((3, 2, ATT_TILE, LANES), F32),
            pltpu.VMEM((3, 2, ATT_TILE, LANES), F32),
        ],
        compiler_params=_params(("parallel", "parallel")),
        name="dilated_attention",
    )(*operands)


def _split3(x):
    hi = x.astype(BF16)
    r1 = x - hi.astype(F32)
    mid = r1.astype(BF16)
    lo = (r1 - mid.astype(F32)).astype(BF16)
    return hi, mid, lo


def _log_sigmoid(x):
    return -(jnp.maximum(-x, 0.0) + jnp.log1p(jnp.exp(-jnp.abs(x))))


def _mix_kernel(rest_ref, gate_ref, poolw_ref, pscale_ref, convw_ref, convb_ref, wqk_ref, wv_ref,
                gbias_ref, normw_ref, skip_ref, pool_ref, hm_ref,
                phist_ref, chist_ref, cstate_ref, mstate_ref):
    s_idx = pl.program_id(1)
    n = SEQ_CHUNK
    dh = MLSTM_HEAD_DIM

    @pl.when(s_idx == 0)
    def _():
        phist_ref[...] = jnp.zeros_like(phist_ref)
        chist_ref[...] = jnp.zeros_like(chist_ref)
        cstate_ref[...] = jnp.zeros_like(cstate_ref)
        mstate_ref[...] = jnp.zeros_like(mstate_ref)

    p_in = rest_ref[:, 0:POOL_WIDTH].astype(F32)
    ext = jnp.concatenate([phist_ref[...], p_in], axis=0)
    phist_ref[...] = p_in[n - POOL_HIST:, :]
    sums = {1: ext}
    w = 1
    while w < POOL_WINDOWS[-1]:
        sums[2 * w] = sums[w] + pltpu.roll(sums[w], w, 0)
        w *= 2
    lane = lax.broadcasted_iota(jnp.int32, (n, POOL_WIDTH), 1)
    pos = lax.broadcasted_iota(jnp.int32, (n, POOL_WIDTH), 0) + s_idx * n
    win_sum = jnp.zeros((n, POOL_WIDTH), F32)
    win_len = jnp.zeros((n, POOL_WIDTH), jnp.int32)
    for g, wlen in enumerate(POOL_WINDOWS):
        sel = (lane >> (POOL_GROUP_DIM.bit_length() - 1)) == g
        win_sum = jnp.where(sel, sums[wlen][POOL_HIST:, :], win_sum)
        win_len = jnp.where(sel, wlen, win_len)
    count = jnp.minimum(pos + 1, win_len).astype(F32)
    dlt = win_sum / count - p_in
    y = jnp.dot(dlt.astype(BF16), poolw_ref[...], preferred_element_type=F32)
    pool_ref[...] = (y * pscale_ref[...]).astype(pool_ref.dtype)

    xm = rest_ref[:, POOL_WIDTH:POOL_WIDTH + MLSTM_WIDTH].astype(F32)
    og = rest_ref[:, POOL_WIDTH + MLSTM_WIDTH:POOL_WIDTH + 2 * MLSTM_WIDTH].astype(F32)
    cext = jnp.concatenate([chist_ref[...], xm], axis=0)
    chist_ref[...] = xm[n - CONV_HIST:, :]
    conv = cext * convw_ref[MLSTM_CONV - 1:MLSTM_CONV, :]
    for back in range(1, MLSTM_CONV):
        tap = MLSTM_CONV - 1 - back
        conv = conv + pltpu.roll(cext, back, 0) * convw_ref[tap:tap + 1, :]
    conv = conv[CONV_HIST:, :] + convb_ref[...]
    xc = conv * jax.nn.sigmoid(conv)
    xc_b = xc.astype(BF16)
    xm_b = xm.astype(BF16)

    gi = gate_ref[...] + gbias_ref[...]
    f_pre = pltpu.roll(gi, GATE_LANES - MLSTM_HEADS, 1)
    logf = _log_sigmoid(f_pre)
    row = lax.broadcasted_iota(jnp.int32, (n, n), 0)
    col = lax.broadcasted_iota(jnp.int32, (n, n), 1)
    causal = col <= row
    tri = jnp.where(causal, 1.0, 0.0).astype(BF16)
    bcum = sum(jnp.dot(tri, part, preferred_element_type=F32) for part in _split3(logf))
    x_ib = gi - bcum
    b_last = bcum[n - 1:n, :]
    m_prev = mstate_ref[0:1, :]
    a = b_last + x_ib
    a_max = jnp.max(a, axis=0, keepdims=True)
    m_new = jnp.maximum(b_last + m_prev, a_max)
    decay = jnp.exp(b_last + m_prev - m_new)
    w_state = jnp.exp(a - m_new)
    inter_log = bcum + m_prev
    x_row = x_ib.T
    mstate_ref[0:1, :] = m_new

    ones = jnp.ones((n, dh), F32)
    for h in range(MLSTM_HEADS):
        hs = slice(h * dh, (h + 1) * dh)
        qk = jnp.dot(xc_b[:, hs], wqk_ref[h], preferred_element_type=F32)
        q = (qk[:, :dh] * (dh ** -0.5)).astype(BF16)
        k = qk[:, dh:].astype(BF16)
        v = jnp.dot(xm_b[:, hs], wv_ref[h], preferred_element_type=F32)
        v_aug = jnp.concatenate([v, ones], axis=1)

        log_d = jnp.where(causal, bcum[:, h:h + 1] + x_row[h:h + 1, :], NEG)
        m_t = jnp.maximum(inter_log[:, h:h + 1], jnp.max(log_d, axis=-1, keepdims=True))
        inter_w = jnp.exp(inter_log[:, h:h + 1] - m_t)
        s = lax.dot_general(q, k, (((1,), (1,)), ((), ())), preferred_element_type=F32)
        sw = (s * jnp.exp(log_d - m_t)).astype(BF16)
        c_prev = cstate_ref[h]
        out = inter_w * jnp.dot(q, c_prev.astype(BF16), preferred_element_type=F32) \
            + jnp.dot(sw, v_aug.astype(BF16), preferred_element_type=F32)
        hh = out[:, :dh] / jnp.maximum(jnp.abs(out[:, dh:]), jnp.exp(-m_t))

        wv_state = (w_state[:, h:h + 1] * v_aug).astype(BF16)
        d_c = lax.dot_general(k, wv_state, (((0,), (0,)), ((), ())), preferred_element_type=F32)
        cstate_ref[h] = decay[:, h:h + 1] * c_prev + d_c

        hh = hh * jax.nn.sigmoid(og[:, hs])
        hh = _rms(hh) * normw_ref[:, hs]
        hm_ref[:, hs] = (hh + skip_ref[:, hs] * xc[:, hs]).astype(hm_ref.dtype)


def _mix(rest, gates, pool_bd, pool_scale, conv_w, conv_b, w_qk, w_v, gate_bias, norm_w, skip, batch, seq):
    n = SEQ_CHUNK
    n_rest = rest.shape[-1]
    rest = rest.reshape(batch, seq, n_rest)
    gates = gates.reshape(batch, seq, GATE_LANES)
    c2 = lambda b, s: (0, 0)
    c3 = lambda b, s: (0, 0, 0)
    return pl.pallas_call(
        _mix_kernel,
        out_shape=(
            jax.ShapeDtypeStruct((batch, seq, POOL_WIDTH), BF16),
            jax.ShapeDtypeStruct((batch, seq, MLSTM_WIDTH), BF16),
        ),
        grid=(batch, seq // n),
        in_specs=[
            pl.BlockSpec((None, n, n_rest), lambda b, s: (b, s, 0)),
            pl.BlockSpec((None, n, GATE_LANES), lambda b, s: (b, s, 0)),
            pl.BlockSpec(pool_bd.shape, c2),
            pl.BlockSpec(pool_scale.shape, c2),
            pl.BlockSpec(conv_w.shape, c2),
            pl.BlockSpec(conv_b.shape, c2),
            pl.BlockSpec(w_qk.shape, c3),
            pl.BlockSpec(w_v.shape, c3),
            pl.BlockSpec(gate_bias.shape, c2),
            pl.BlockSpec(norm_w.shape, c2),
            pl.BlockSpec(skip.shape, c2),
        ],
        out_specs=(
            pl.BlockSpec((None, n, POOL_WIDTH), lambda b, s: (b, s, 0)),
            pl.BlockSpec((None, n, MLSTM_WIDTH), lambda b, s: (b, s, 0)),
        ),
        scratch_shapes=[
            pltpu.VMEM((POOL_HIST, POOL_WIDTH), F32),
            pltpu.VMEM((CONV_HIST, MLSTM_WIDTH), F32),
            pltpu.VMEM((MLSTM_HEADS, MLSTM_HEAD_DIM, 2 * MLSTM_HEAD_DIM), F32),
            pltpu.VMEM((SUBLANES, GATE_LANES), F32),
        ],
        compiler_params=_params(("parallel", "arbitrary")),
        name="pool_mlstm",
    )(rest, gates, pool_bd, pool_scale, conv_w, conv_b, w_qk, w_v, gate_bias, norm_w, skip)


def _outproj_kernel(x_ref, att_ref, pool_ref, hm_ref, wa_ref, wp_ref, wh_ref, mod_ref, postw_ref, o_ref):
    y = jnp.dot(att_ref[...], wa_ref[...], preferred_element_type=F32)
    y = y + jnp.dot(pool_ref[...], wp_ref[...], preferred_element_type=F32)
    y = y + jnp.dot(hm_ref[...], wh_ref[...], preferred_element_type=F32)
    o_ref[...] = x_ref[...] + mod_ref[2:3, :] * (_rms(y) * postw_ref[...])


def _outproj(h, att, pool, hm, w_out, mod3, post_w, layer, seq):
    t, d = h.shape
    tm = ROW_TILE
    per_batch = seq // tm
    widths = (att.shape[-1], pool.shape[-1], hm.shape[-1])
    w_specs = []
    off = 0
    for wdt in widths:
        w_specs.append(pl.BlockSpec((None, wdt, d), functools.partial(lambda i, blk: (layer, blk, 0), blk=off // wdt)))
        off += wdt
    return pl.pallas_call(
        _outproj_kernel,
        out_shape=jax.ShapeDtypeStruct((t, d), F32),
        grid=(t // tm,),
        in_specs=[
            pl.BlockSpec((tm, d), lambda i: (i, 0)),
            pl.BlockSpec((tm, widths[0]), lambda i: (i, 0)),
            pl.BlockSpec((tm, widths[1]), lambda i: (i, 0)),
            pl.BlockSpec((tm, widths[2]), lambda i: (i, 0)),
            *w_specs,
            pl.BlockSpec((None, 3, d), lambda i: (i // per_batch, 0, 0)),
            pl.BlockSpec((1, d), lambda i: (0, 0)),
        ],
        out_specs=pl.BlockSpec((tm, d), lambda i: (i, 0)),
        compiler_params=_params(("parallel",)),
        name="outproj",
    )(h, att.reshape(t, widths[0]), pool.reshape(t, widths[1]), hm.reshape(t, widths[2]),
      w_out, w_out, w_out, mod3, post_w.reshape(1, d))


def kernel(x, c, ada_w, ada_b, pre_norm_w, post_norm_w, ffn_up, ffn_down, mix_in_w, mix_out_w, pool_w, pool_scale,
           mlstm_conv_w, mlstm_conv_b, mlstm_qkv_w, mlstm_gate_b, mlstm_norm_w, mlstm_skip):
    batch, seq, d = x.shape
    depth = ada_w.shape[0]
    assert d == ATT_WIDTH + POOL_WIDTH + MLSTM_WIDTH
    assert seq % ATT_TILE == 0 and seq % ROW_TILE == 0 and seq % SEQ_CHUNK == 0
    assert ffn_down.shape[2] % FF_TILE == 0

    up_b = ffn_up.astype(BF16)
    down_b = ffn_down.astype(BF16)
    out_b = mix_out_w.astype(BF16)
    n_main = QKV_W + POOL_WIDTH + 2 * MLSTM_WIDTH
    w_qkv = mix_in_w[:, :, :QKV_W].astype(BF16)
    w_rest = mix_in_w[:, :, QKV_W:n_main].astype(BF16)
    w_gate = jnp.pad(mix_in_w[:, :, n_main:], ((0, 0), (0, 0), (0, GATE_LANES - 2 * MLSTM_HEADS))).astype(BF16)
    gate_bias = jnp.pad(mlstm_gate_b.reshape(depth, 1, 2 * MLSTM_HEADS),
                        ((0, 0), (0, 0), (0, GATE_LANES - 2 * MLSTM_HEADS)))
    groups = len(POOL_WINDOWS)
    eye = jnp.eye(groups, dtype=pool_w.dtype)
    pool_bd = (pool_w[:, :, :, None, :] * eye[None, :, None, :, None]).reshape(depth, POOL_WIDTH, POOL_WIDTH).astype(BF16)
    w_qk = jnp.concatenate([mlstm_qkv_w[:, 0], mlstm_qkv_w[:, 1]], axis=-1).astype(BF16)
    w_v = mlstm_qkv_w[:, 2].astype(BF16)

    mod = _adaln(c, ada_w, ada_b).reshape(depth, batch, 9, d)

    h = x.reshape(batch * seq, d)
    for l in range(depth):
        h = _ffn(h, mod[l, :, 0:3], pre_norm_w[l, 0], post_norm_w[l, 0], up_b, down_b, l, 0, seq)
        qkv, rest, gates = _inproj(h, mod[l, :, 3:6], pre_norm_w[l, 1], w_qkv[l], w_rest[l], w_gate[l], seq)
        att = _attention(qkv, batch, seq)
        pool, hm = _mix(rest, gates, pool_bd[l], pool_scale[l].reshape(1, -1), mlstm_conv_w[l],
                        mlstm_conv_b[l].reshape(1, -1), w_qk[l], w_v[l], gate_bias[l],
                        mlstm_norm_w[l].reshape(1, -1), mlstm_skip[l].reshape(1, -1), batch, seq)
        h = _outproj(h, att, pool, hm, out_b, mod[l, :, 3:6], post_norm_w[l, 1], l, seq)
        h = _ffn(h, mod[l, :, 6:9], pre_norm_w[l, 2], post_norm_w[l, 2], up_b, down_b, l, 1, seq)
    return h.reshape(batch, seq, d)
```

```python
import functools

import jax
import jax.numpy as jnp
from jax import lax
from jax.experimental import pallas as pl
from jax.experimental.pallas import tpu as pltpu

F32 = jnp.float32
BF16 = jnp.bfloat16

ATT_HEADS = 4
ATT_HEAD_DIM = 64
ATT_WIDTH = ATT_HEADS * ATT_HEAD_DIM
DILATIONS = (1, 4, 16)
ATT_SPAN = 128
POOL_WINDOWS = (2, 4, 8, 16)
POOL_GROUP_DIM = 64
POOL_WIDTH = len(POOL_WINDOWS) * POOL_GROUP_DIM
MLSTM_HEADS = 4
MLSTM_HEAD_DIM = 128
MLSTM_WIDTH = MLSTM_HEADS * MLSTM_HEAD_DIM
MLSTM_CONV = 4
MACARON_WEIGHT = 0.5
EPS = 1e-6
NEG = -1e30

LANES = 128
SUBLANES = 8
VMEM_LIMIT_BYTES = 56 * 1024 * 1024

ROW_TILE = 1024
FF_TILE = 256
ATT_TILE = 2048
ATT_BLOCK = 128
SEQ_CHUNK = 256
POOL_HIST = 16
CONV_HIST = 8
GATE_LANES = LANES


def _params(sem):
    return pltpu.CompilerParams(dimension_semantics=sem, vmem_limit_bytes=VMEM_LIMIT_BYTES)


def _rms(x):
    return x * lax.rsqrt(jnp.mean(x * x, axis=-1, keepdims=True) + EPS)


def _adaln_kernel(c_ref, w_ref, b_ref, o_ref):
    c = c_ref[...]
    c_act = (c * jax.nn.sigmoid(c)).astype(BF16)
    o_ref[...] = jnp.dot(c_act, w_ref[...].astype(BF16), preferred_element_type=F32) + b_ref[...]


def _adaln(c, ada_w, ada_b):
    depth, d, n = ada_w.shape
    b = c.shape[0]
    tn = n // 8
    return pl.pallas_call(
        _adaln_kernel,
        out_shape=jax.ShapeDtypeStruct((depth, b, n), F32),
        grid=(depth, n // tn),
        in_specs=[
            pl.BlockSpec((b, d), lambda l, j: (0, 0)),
            pl.BlockSpec((None, d, tn), lambda l, j: (l, 0, j)),
            pl.BlockSpec((None, 1, tn), lambda l, j: (l, 0, j)),
        ],
        out_specs=pl.BlockSpec((None, b, tn), lambda l, j: (l, 0, j)),
        compiler_params=_params(("parallel", "parallel")),
        name="adaln",
    )(c, ada_w, ada_b.reshape(depth, 1, n))


def _ffn_kernel(x_ref, mod_ref, prew_ref, postw_ref, wg_ref, wv_ref, wd_ref, o_ref, u_ref, acc_ref, *, weight):
    k = pl.program_id(1)

    @pl.when(k == 0)
    def _():
        u = _rms(x_ref[...]) * prew_ref[...] * (1.0 + mod_ref[1:2, :]) + mod_ref[0:1, :]
        u_ref[...] = u.astype(BF16)
        acc_ref[...] = jnp.zeros_like(acc_ref)

    u = u_ref[...]
    g = jnp.dot(u, wg_ref[...], preferred_element_type=F32)
    v = jnp.dot(u, wv_ref[...], preferred_element_type=F32)
    a = (g * jax.nn.sigmoid(g) * v).astype(BF16)
    acc_ref[...] += jnp.dot(a, wd_ref[...], preferred_element_type=F32)

    @pl.when(k == pl.num_programs(1) - 1)
    def _():
        y = _rms(acc_ref[...]) * postw_ref[...]
        o_ref[...] = x_ref[...] + (weight * mod_ref[2:3, :]) * y


def _ffn(h, mod3, pre_w, post_w, w_up, w_down, layer, slot, seq):
    t, d = h.shape
    ff = w_down.shape[2]
    tm, tf = ROW_TILE, FF_TILE
    per_batch = seq // tm
    nf = ff // tf
    return pl.pallas_call(
        functools.partial(_ffn_kernel, weight=MACARON_WEIGHT),
        out_shape=jax.ShapeDtypeStruct((t, d), F32),
        grid=(t // tm, nf),
        in_specs=[
            pl.BlockSpec((tm, d), lambda i, k: (i, 0)),
            pl.BlockSpec((None, 3, d), lambda i, k: (i // per_batch, 0, 0)),
            pl.BlockSpec((1, d), lambda i, k: (0, 0)),
            pl.BlockSpec((1, d), lambda i, k: (0, 0)),
            pl.BlockSpec((None, None, d, tf), lambda i, k: (layer, slot, 0, k)),
            pl.BlockSpec((None, None, d, tf), lambda i, k: (layer, slot, 0, k + nf)),
            pl.BlockSpec((None, None, tf, d), lambda i, k: (layer, slot, k, 0)),
        ],
        out_specs=pl.BlockSpec((tm, d), lambda i, k: (i, 0)),
        scratch_shapes=[pltpu.VMEM((tm, d), BF16), pltpu.VMEM((tm, d), F32)],
        compiler_params=_params(("parallel", "arbitrary")),
        name="ffn",
    )(h, mod3, pre_w.reshape(1, d), post_w.reshape(1, d), w_up, w_up, w_down)


def _inproj_kernel(x_ref, mod_ref, prew_ref, wqkv_ref, wrest_ref, wgate_ref,
                   qkv1_ref, qkv4_ref, qkv16_ref, rest_ref, gate_ref, z_ref):
    u = _rms(x_ref[...]) * prew_ref[...] * (1.0 + mod_ref[1:2, :]) + mod_ref[0:1, :]
    u = u.astype(BF16)
    z = jnp.dot(u, wqkv_ref[...], preferred_element_type=F32)
    qkv1_ref[0] = z.astype(BF16)
    rest_ref[...] = jnp.dot(u, wrest_ref[...], preferred_element_type=F32).astype(BF16)
    gate_ref[...] = jnp.dot(u, wgate_ref[...], preferred_element_type=F32)
    rows = z.shape[0]
    slabs = z.shape[1] // LANES
    for c in range(slabs):
        z_ref[c] = z[:, c * LANES:(c + 1) * LANES]
    for dil, out_ref in ((DILATIONS[1], qkv4_ref), (DILATIONS[2], qkv16_ref)):
        for r in range(dil):
            for c in range(slabs):
                out_ref[r, :, c * LANES:(c + 1) * LANES] = z_ref[c, pl.ds(r, rows // dil, stride=dil), :].astype(BF16)


def _inproj(h, mod3, pre_w, w_qkv, w_rest, w_gate, batch, seq):
    t, d = h.shape
    tm = ROW_TILE
    per_batch = seq // tm
    n_qkv, n_rest, n_gate = w_qkv.shape[1], w_rest.shape[1], w_gate.shape[1]
    full = lambda i: (0, 0)
    by_residue = lambda i: (i // per_batch, 0, i % per_batch, 0)
    qkv_shapes = [jax.ShapeDtypeStruct((batch, dil, seq // dil, n_qkv), BF16) for dil in DILATIONS]
    qkv_specs = [pl.BlockSpec((None, dil, tm // dil, n_qkv), by_residue) for dil in DILATIONS]
    return pl.pallas_call(
        _inproj_kernel,
        out_shape=(
            *qkv_shapes,
            jax.ShapeDtypeStruct((t, n_rest), BF16),
            jax.ShapeDtypeStruct((t, n_gate), F32),
        ),
        grid=(t // tm,),
        in_specs=[
            pl.BlockSpec((tm, d), lambda i: (i, 0)),
            pl.BlockSpec((None, 3, d), lambda i: (i // per_batch, 0, 0)),
            pl.BlockSpec((1, d), full),
            pl.BlockSpec((d, n_qkv), full),
            pl.BlockSpec((d, n_rest), full),
            pl.BlockSpec((d, n_gate), full),
        ],
        out_specs=(
            *qkv_specs,
            pl.BlockSpec((tm, n_rest), lambda i: (i, 0)),
            pl.BlockSpec((tm, n_gate), lambda i: (i, 0)),
        ),
        scratch_shapes=[pltpu.VMEM((n_qkv // LANES, tm, LANES), F32)],
        compiler_params=_params(("parallel",)),
        name="inproj",
    )(h, mod3, pre_w.reshape(1, d), w_qkv, w_rest, w_gate)


QKV_W = 3 * ATT_WIDTH


def _attn_scores(q, kwin, bias, head_of_lane):
    q32 = q.astype(F32) * (ATT_HEAD_DIM ** -0.5)
    lhs = jnp.concatenate(
        [jnp.where(head_of_lane == h, q32, 0.0) for h in range(ATT_HEADS)], axis=0).astype(BF16)
    return lax.dot_general(lhs, kwin, (((1,), (1,)), ((), ())), preferred_element_type=F32) + bias


def _attn_softmax(s, head_of_lane):
    nq = s.shape[0] // ATT_HEADS
    m = jnp.max(s, axis=-1, keepdims=True)
    p = jnp.exp(s - m)
    l = jnp.sum(p, axis=-1, keepdims=True)
    mb = jnp.zeros((nq, ATT_WIDTH), F32)
    lb = jnp.zeros((nq, ATT_WIDTH), F32)
    for h in range(ATT_HEADS):
        rows = slice(h * nq, (h + 1) * nq)
        sel = head_of_lane == h
        mb = jnp.where(sel, m[rows], mb)
        lb = jnp.where(sel, l[rows], lb)
    return p.astype(BF16), mb, lb


def _attn_values(p, vwin, head_of_lane):
    nq = p.shape[0] // ATT_HEADS
    pv = jnp.dot(p, vwin, preferred_element_type=F32)
    num = pv[0:nq]
    for h in range(1, ATT_HEADS):
        num = jnp.where(head_of_lane == h, pv[h * nq:(h + 1) * nq], num)
    return num


def _attn_kernel(c1_ref, h1_ref, c4_ref, h4_ref, c16_ref, h16_ref, o_ref,
                 bias_ref, run_ref, tmp_ref, s_ref, p_ref):
    t = pl.program_id(1)
    nq = ATT_BLOCK
    head_of_lane = lax.broadcasted_iota(jnp.int32, (nq, ATT_WIDTH), 1) >> (ATT_HEAD_DIM.bit_length() - 1)

    qi = lax.broadcasted_iota(jnp.int32, (ATT_HEADS * nq, 2 * nq), 0) & (nq - 1)
    ki = lax.broadcasted_iota(jnp.int32, (ATT_HEADS * nq, 2 * nq), 1)
    ok = (ki >= qi) & (ki <= qi + ATT_SPAN)
    bias_ref[0] = jnp.where(ok, 0.0, NEG)
    bias_ref[1] = jnp.where(ok & (ki >= nq), 0.0, NEG)
    qs = slice(0, ATT_WIDTH)
    ks = slice(ATT_WIDTH, 2 * ATT_WIDTH)
    vs = slice(2 * ATT_WIDTH, 3 * ATT_WIDTH)

    def put(dst_ref, q_idx, val, start, stride):
        for half in range(2):
            piece = val[:, half * LANES:(half + 1) * LANES]
            if stride == 1:
                dst_ref[q_idx, half, pl.ds(start, nq), :] = piece
            else:
                dst_ref[q_idx, half, pl.ds(start, nq, stride=stride), :] = piece

    def branch(cur_ref, halo_ref, dil, dst_ref):
        blocks = ATT_TILE // dil // nq
        items = dil * blocks
        shift = blocks.bit_length() - 1

        def window(idx, cols):
            r = idx >> shift
            i = idx & (blocks - 1)
            own = cur_ref[r, pl.ds(pl.multiple_of(i * nq, nq), nq), cols]
            if blocks == 1:
                prev = halo_ref[r, :, cols]
            else:
                prev0 = pl.multiple_of(jnp.maximum(i - 1, 0) * nq, nq)
                prev = jnp.where(i == 0, halo_ref[r, :, cols], cur_ref[r, pl.ds(prev0, nq), cols])
            return jnp.concatenate([prev, own], axis=0)

        def start_row(idx):
            return (idx & (blocks - 1)) * (nq * dil) + (idx >> shift)

        s_ref[1] = jnp.zeros(s_ref.shape[1:], s_ref.dtype)
        p_ref[1] = jnp.zeros(p_ref.shape[1:], p_ref.dtype)

        def body(j, carry):
            slot, other = j & 1, (j + 1) & 1
            a = jnp.minimum(j, items - 1)
            b = jnp.clip(j - 1, 0, items - 1)
            c = jnp.clip(j - 2, 0, items - 1)

            put(dst_ref, 0, _attn_values(p_ref[other], window(c, vs), head_of_lane), start_row(c), dil)

            p, mb, lb = _attn_softmax(s_ref[other], head_of_lane)
            p_ref[slot] = p
            put(dst_ref, 1, mb, start_row(b), dil)
            put(dst_ref, 2, lb, start_row(b), dil)

            r = a >> shift
            i = a & (blocks - 1)
            no_past = jnp.where((i == 0) & (t == 0), 1, 0)
            q = cur_ref[r, pl.ds(pl.multiple_of(i * nq, nq), nq), qs]
            s_ref[slot] = _attn_scores(q, window(a, ks), bias_ref[no_past], head_of_lane)
            return carry
        lax.fori_loop(0, items + 2, body, 0)

    rows = 256

    def merge(final):
        def body(c, carry):
            sl = pl.ds(pl.multiple_of(c * rows, rows), rows)
            for half in range(2):
                n_r, m_r, l_r = (run_ref[q, half, sl, :] for q in range(3))
                n_t, m_t, l_t = (tmp_ref[q, half, sl, :] for q in range(3))
                mx = jnp.maximum(m_r, m_t)
                e_r = jnp.exp(m_r - mx)
                e_t = jnp.exp(m_t - mx)
                n_new = e_r * n_r + e_t * n_t
                l_new = e_r * l_r + e_t * l_t
                if final:
                    o_ref[sl, half * LANES:(half + 1) * LANES] = (n_new / l_new).astype(o_ref.dtype)
                else:
                    run_ref[0, half, sl, :] = n_new
                    run_ref[1, half, sl, :] = mx
                    run_ref[2, half, sl, :] = l_new
            return carry
        lax.fori_loop(0, ATT_TILE // rows, body, 0)

    branch(c1_ref, h1_ref, DILATIONS[0], run_ref)
    branch(c4_ref, h4_ref, DILATIONS[1], tmp_ref)
    merge(False)
    branch(c16_ref, h16_ref, DILATIONS[2], tmp_ref)
    merge(True)


def _attention(qkv_by_dilation, batch, seq):
    tiles = seq // ATT_TILE
    in_specs = []
    operands = []
    for dil, view in zip(DILATIONS, qkv_by_dilation):
        cur_rows = ATT_TILE // dil
        per_tile = cur_rows // ATT_BLOCK
        in_specs.append(pl.BlockSpec((None, dil, cur_rows, QKV_W), lambda b, t: (b, 0, t, 0)))
        in_specs.append(pl.BlockSpec(
            (None, dil, ATT_BLOCK, QKV_W),
            functools.partial(lambda b, t, per_tile: (b, 0, jnp.maximum(t * per_tile - 1, 0), 0), per_tile=per_tile)))
        operands += [view, view]
    return pl.pallas_call(
        _attn_kernel,
        out_shape=jax.ShapeDtypeStruct((batch, seq, ATT_WIDTH), BF16),
        grid=(batch, tiles),
        in_specs=in_specs,
        out_specs=pl.BlockSpec((None, ATT_TILE, ATT_WIDTH), lambda b, t: (b, t, 0)),
        scratch_shapes=[
            pltpu.VMEM((2, ATT_HEADS * ATT_BLOCK, 2 * ATT_BLOCK), F32),
            pltpu.VMEM((3, 2, ATT_TILE, LANES), F32),
            pltpu.VMEM((3, 2, ATT_TILE, LANES), F32),
            pltpu.VMEM((2, ATT_HEADS * ATT_BLOCK, 2 * ATT_BLOCK), F32),
            pltpu.VMEM((2, ATT_HEADS * ATT_BLOCK, 2 * ATT_BLOCK), BF16),
        ],
        compiler_params=_params(("parallel", "parallel")),
        name="dilated_attention",
    )(*operands)


def _split3(x):
    hi = x.astype(BF16)
    r1 = x - hi.astype(F32)
    mid = r1.astype(BF16)
    lo = (r1 - mid.astype(F32)).astype(BF16)
    return hi, mid, lo


def _log_sigmoid(x):
    return -(jnp.maximum(-x, 0.0) + jnp.log1p(jnp.exp(-jnp.abs(x))))


def _mix_kernel(rest_ref, gate_ref, poolw_ref, pscale_ref, convw_ref, convb_ref, wqk_ref, wv_ref,
                gbias_ref, normw_ref, skip_ref, pool_ref, hm_ref,
                phist_ref, chist_ref, cstate_ref, mstate_ref):
    s_idx = pl.program_id(1)
    n = SEQ_CHUNK
    dh = MLSTM_HEAD_DIM

    @pl.when(s_idx == 0)
    def _():
        phist_ref[...] = jnp.zeros_like(phist_ref)
        chist_ref[...] = jnp.zeros_like(chist_ref)
        cstate_ref[...] = jnp.zeros_like(cstate_ref)
        mstate_ref[...] = jnp.zeros_like(mstate_ref)

    p_in = rest_ref[:, 0:POOL_WIDTH].astype(F32)
    ext = jnp.concatenate([phist_ref[...], p_in], axis=0)
    phist_ref[...] = p_in[n - POOL_HIST:, :]
    sums = {1: ext}
    w = 1
    while w < POOL_WINDOWS[-1]:
        sums[2 * w] = sums[w] + pltpu.roll(sums[w], w, 0)
        w *= 2
    lane = lax.broadcasted_iota(jnp.int32, (n, POOL_WIDTH), 1)
    pos = lax.broadcasted_iota(jnp.int32, (n, POOL_WIDTH), 0) + s_idx * n
    win_sum = jnp.zeros((n, POOL_WIDTH), F32)
    win_len = jnp.zeros((n, POOL_WIDTH), jnp.int32)
    for g, wlen in enumerate(POOL_WINDOWS):
        sel = (lane >> (POOL_GROUP_DIM.bit_length() - 1)) == g
        win_sum = jnp.where(sel, sums[wlen][POOL_HIST:, :], win_sum)
        win_len = jnp.where(sel, wlen, win_len)
    count = jnp.minimum(pos + 1, win_len).astype(F32)
    dlt = win_sum / count - p_in
    y = jnp.dot(dlt.astype(BF16), poolw_ref[...], preferred_element_type=F32)
    pool_ref[...] = (y * pscale_ref[...]).astype(pool_ref.dtype)

    xm = rest_ref[:, POOL_WIDTH:POOL_WIDTH + MLSTM_WIDTH].astype(F32)
    og = rest_ref[:, POOL_WIDTH + MLSTM_WIDTH:POOL_WIDTH + 2 * MLSTM_WIDTH].astype(F32)
    cext = jnp.concatenate([chist_ref[...], xm], axis=0)
    chist_ref[...] = xm[n - CONV_HIST:, :]
    conv = cext * convw_ref[MLSTM_CONV - 1:MLSTM_CONV, :]
    for back in range(1, MLSTM_CONV):
        tap = MLSTM_CONV - 1 - back
        conv = conv + pltpu.roll(cext, back, 0) * convw_ref[tap:tap + 1, :]
    conv = conv[CONV_HIST:, :] + convb_ref[...]
    xc = conv * jax.nn.sigmoid(conv)
    xc_b = xc.astype(BF16)
    xm_b = xm.astype(BF16)

    gi = gate_ref[...] + gbias_ref[...]
    f_pre = pltpu.roll(gi, GATE_LANES - MLSTM_HEADS, 1)
    logf = _log_sigmoid(f_pre)
    row = lax.broadcasted_iota(jnp.int32, (n, n), 0)
    col = lax.broadcasted_iota(jnp.int32, (n, n), 1)
    causal = col <= row
    tri = jnp.where(causal, 1.0, 0.0).astype(BF16)
    bcum = sum(jnp.dot(tri, part, preferred_element_type=F32) for part in _split3(logf))
    x_ib = gi - bcum
    b_last = bcum[n - 1:n, :]
    m_prev = mstate_ref[0:1, :]
    a = b_last + x_ib
    a_max = jnp.max(a, axis=0, keepdims=True)
    m_new = jnp.maximum(b_last + m_prev, a_max)
    decay = jnp.exp(b_last + m_prev - m_new)
    w_state = jnp.exp(a - m_new)
    inter_log = bcum + m_prev
    x_row = x_ib.T
    mstate_ref[0:1, :] = m_new

    ones = jnp.ones((n, dh), F32)
    for h in range(MLSTM_HEADS):
        hs = slice(h * dh, (h + 1) * dh)
        qk = jnp.dot(xc_b[:, hs], wqk_ref[h], preferred_element_type=F32)
        q = (qk[:, :dh] * (dh ** -0.5)).astype(BF16)
        k = qk[:, dh:].astype(BF16)
        v = jnp.dot(xm_b[:, hs], wv_ref[h], preferred_element_type=F32)
        v_aug = jnp.concatenate([v, ones], axis=1)

        log_d = jnp.where(causal, bcum[:, h:h + 1] + x_row[h:h + 1, :], NEG)
        m_t = jnp.maximum(inter_log[:, h:h + 1], jnp.max(log_d, axis=-1, keepdims=True))
        inter_w = jnp.exp(inter_log[:, h:h + 1] - m_t)
        s = lax.dot_general(q, k, (((1,), (1,)), ((), ())), preferred_element_type=F32)
        sw = (s * jnp.exp(log_d - m_t)).astype(BF16)
        c_prev = cstate_ref[h]
        out = inter_w * jnp.dot(q, c_prev.astype(BF16), preferred_element_type=F32) \
            + jnp.dot(sw, v_aug.astype(BF16), preferred_element_type=F32)
        hh = out[:, :dh] / jnp.maximum(jnp.abs(out[:, dh:]), jnp.exp(-m_t))

        wv_state = (w_state[:, h:h + 1] * v_aug).astype(BF16)
        d_c = lax.dot_general(k, wv_state, (((0,), (0,)), ((), ())), preferred_element_type=F32)
        cstate_ref[h] = decay[:, h:h + 1] * c_prev + d_c

        hh = hh * jax.nn.sigmoid(og[:, hs])
        hh = _rms(hh) * normw_ref[:, hs]
        hm_ref[:, hs] = (hh + skip_ref[:, hs] * xc[:, hs]).astype(hm_ref.dtype)


def _mix(rest, gates, pool_bd, pool_scale, conv_w, conv_b, w_qk, w_v, gate_bias, norm_w, skip, batch, seq):
    n = SEQ_CHUNK
    n_rest = rest.shape[-1]
    rest = rest.reshape(batch, seq, n_rest)
    gates = gates.reshape(batch, seq, GATE_LANES)
    c2 = lambda b, s: (0, 0)
    c3 = lambda b, s: (0, 0, 0)
    return pl.pallas_call(
        _mix_kernel,
        out_shape=(
            jax.ShapeDtypeStruct((batch, seq, POOL_WIDTH), BF16),
            jax.ShapeDtypeStruct((batch, seq, MLSTM_WIDTH), BF16),
        ),
        grid=(batch, seq // n),
        in_specs=[
            pl.BlockSpec((None, n, n_rest), lambda b, s: (b, s, 0)),
            pl.BlockSpec((None, n, GATE_LANES), lambda b, s: (b, s, 0)),
            pl.BlockSpec(pool_bd.shape, c2),
            pl.BlockSpec(pool_scale.shape, c2),
            pl.BlockSpec(conv_w.shape, c2),
            pl.BlockSpec(conv_b.shape, c2),
            pl.BlockSpec(w_qk.shape, c3),
            pl.BlockSpec(w_v.shape, c3),
            pl.BlockSpec(gate_bias.shape, c2),
            pl.BlockSpec(norm_w.shape, c2),
            pl.BlockSpec(skip.shape, c2),
        ],
        out_specs=(
            pl.BlockSpec((None, n, POOL_WIDTH), lambda b, s: (b, s, 0)),
            pl.BlockSpec((None, n, MLSTM_WIDTH), lambda b, s: (b, s, 0)),
        ),
        scratch_shapes=[
            pltpu.VMEM((POOL_HIST, POOL_WIDTH), F32),
            pltpu.VMEM((CONV_HIST, MLSTM_WIDTH), F32),
            pltpu.VMEM((MLSTM_HEADS, MLSTM_HEAD_DIM, 2 * MLSTM_HEAD_DIM), F32),
            pltpu.VMEM((SUBLANES, GATE_LANES), F32),
        ],
        compiler_params=_params(("parallel", "arbitrary")),
        name="pool_mlstm",
    )(rest, gates, pool_bd, pool_scale, conv_w, conv_b, w_qk, w_v, gate_bias, norm_w, skip)


def _outproj_kernel(x_ref, att_ref, pool_ref, hm_ref, wa_ref, wp_ref, wh_ref, mod_ref, postw_ref, o_ref):
    y = jnp.dot(att_ref[...], wa_ref[...], preferred_element_type=F32)
    y = y + jnp.dot(pool_ref[...], wp_ref[...], preferred_element_type=F32)
    y = y + jnp.dot(hm_ref[...], wh_ref[...], preferred_element_type=F32)
    o_ref[...] = x_ref[...] + mod_ref[2:3, :] * (_rms(y) * postw_ref[...])


def _outproj(h, att, pool, hm, w_out, mod3, post_w, layer, seq):
    t, d = h.shape
    tm = ROW_TILE
    per_batch = seq // tm
    widths = (att.shape[-1], pool.shape[-1], hm.shape[-1])
    w_specs = []
    off = 0
    for wdt in widths:
        w_specs.append(pl.BlockSpec((None, wdt, d), functools.partial(lambda i, blk: (layer, blk, 0), blk=off // wdt)))
        off += wdt
    return pl.pallas_call(
        _outproj_kernel,
        out_shape=jax.ShapeDtypeStruct((t, d), F32),
        grid=(t // tm,),
        in_specs=[
            pl.BlockSpec((tm, d), lambda i: (i, 0)),
            pl.BlockSpec((tm, widths[0]), lambda i: (i, 0)),
            pl.BlockSpec((tm, widths[1]), lambda i: (i, 0)),
            pl.BlockSpec((tm, widths[2]), lambda i: (i, 0)),
            *w_specs,
            pl.BlockSpec((None, 3, d), lambda i: (i // per_batch, 0, 0)),
            pl.BlockSpec((1, d), lambda i: (0, 0)),
        ],
        out_specs=pl.BlockSpec((tm, d), lambda i: (i, 0)),
        compiler_params=_params(("parallel",)),
        name="outproj",
    )(h, att.reshape(t, widths[0]), pool.reshape(t, widths[1]), hm.reshape(t, widths[2]),
      w_out, w_out, w_out, mod3, post_w.reshape(1, d))


def kernel(x, c, ada_w, ada_b, pre_norm_w, post_norm_w, ffn_up, ffn_down, mix_in_w, mix_out_w, pool_w, pool_scale,
           mlstm_conv_w, mlstm_conv_b, mlstm_qkv_w, mlstm_gate_b, mlstm_norm_w, mlstm_skip):
    batch, seq, d = x.shape
    depth = ada_w.shape[0]
    assert d == ATT_WIDTH + POOL_WIDTH + MLSTM_WIDTH
    assert seq % ATT_TILE == 0 and seq % ROW_TILE == 0 and seq % SEQ_CHUNK == 0
    assert ffn_down.shape[2] % FF_TILE == 0

    up_b = ffn_up.astype(BF16)
    down_b = ffn_down.astype(BF16)
    out_b = mix_out_w.astype(BF16)
    n_main = QKV_W + POOL_WIDTH + 2 * MLSTM_WIDTH
    w_qkv = mix_in_w[:, :, :QKV_W].astype(BF16)
    w_rest = mix_in_w[:, :, QKV_W:n_main].astype(BF16)
    w_gate = jnp.pad(mix_in_w[:, :, n_main:], ((0, 0), (0, 0), (0, GATE_LANES - 2 * MLSTM_HEADS))).astype(BF16)
    gate_bias = jnp.pad(mlstm_gate_b.reshape(depth, 1, 2 * MLSTM_HEADS),
                        ((0, 0), (0, 0), (0, GATE_LANES - 2 * MLSTM_HEADS)))
    groups = len(POOL_WINDOWS)
    eye = jnp.eye(groups, dtype=pool_w.dtype)
    pool_bd = (pool_w[:, :, :, None, :] * eye[None, :, None, :, None]).reshape(depth, POOL_WIDTH, POOL_WIDTH).astype(BF16)
    w_qk = jnp.concatenate([mlstm_qkv_w[:, 0], mlstm_qkv_w[:, 1]], axis=-1).astype(BF16)
    w_v = mlstm_qkv_w[:, 2].astype(BF16)

    mod = _adaln(c, ada_w, ada_b).reshape(depth, batch, 9, d)

    h = x.reshape(batch * seq, d)
    for l in range(depth):
        h = _ffn(h, mod[l, :, 0:3], pre_norm_w[l, 0], post_norm_w[l, 0], up_b, down_b, l, 0, seq)
        *qkv, rest, gates = _inproj(h, mod[l, :, 3:6], pre_norm_w[l, 1], w_qkv[l], w_rest[l], w_gate[l], batch, seq)
        att = _attention(qkv, batch, seq)
        pool, hm = _mix(rest, gates, pool_bd[l], pool_scale[l].reshape(1, -1), mlstm_conv_w[l],
                        mlstm_conv_b[l].reshape(1, -1), w_qk[l], w_v[l], gate_bias[l],
                        mlstm_norm_w[l].reshape(1, -1), mlstm_skip[l].reshape(1, -1), batch, seq)
        h = _outproj(h, att, pool, hm, out_b, mod[l, :, 3:6], post_norm_w[l, 1], l, seq)
        h = _ffn(h, mod[l, :, 6:9], pre_norm_w[l, 2], post_norm_w[l, 2], up_b, down_b, l, 1, seq)
    return h.reshape(batch, seq, d)
```

```python
import functools

import jax
import jax.numpy as jnp
from jax import lax
from jax.experimental import pallas as pl
from jax.experimental.pallas import tpu as pltpu

F32 = jnp.float32
BF16 = jnp.bfloat16

ATT_HEADS = 4
ATT_HEAD_DIM = 64
ATT_WIDTH = ATT_HEADS * ATT_HEAD_DIM
DILATIONS = (1, 4, 16)
ATT_SPAN = 128
POOL_WINDOWS = (2, 4, 8, 16)
POOL_GROUP_DIM = 64
POOL_WIDTH = len(POOL_WINDOWS) * POOL_GROUP_DIM
MLSTM_HEADS = 4
MLSTM_HEAD_DIM = 128
MLSTM_WIDTH = MLSTM_HEADS * MLSTM_HEAD_DIM
MLSTM_CONV = 4
MACARON_WEIGHT = 0.5
EPS = 1e-6
NEG = -1e30

LANES = 128
SUBLANES = 8
VMEM_LIMIT_BYTES = 56 * 1024 * 1024

ROW_TILE = 1024
FFN_ROW_TILE = 1024
ATT_TILE = 2048
ATT_BLOCK = 128
SEQ_CHUNK = 256
POOL_HIST = 16
CONV_HIST = 8
GATE_LANES = LANES


def _params(sem):
    return pltpu.CompilerParams(dimension_semantics=sem, vmem_limit_bytes=VMEM_LIMIT_BYTES)


def _rms(x):
    return x * lax.rsqrt(jnp.mean(x * x, axis=-1, keepdims=True) + EPS)


def _adaln_kernel(c_ref, w_ref, b_ref, o_ref):
    c = c_ref[...]
    c_act = (c * jax.nn.sigmoid(c)).astype(BF16)
    o_ref[...] = jnp.dot(c_act, w_ref[...].astype(BF16), preferred_element_type=F32) + b_ref[...]


def _adaln(c, ada_w, ada_b):
    depth, d, n = ada_w.shape
    b = c.shape[0]
    tn = n // 8
    return pl.pallas_call(
        _adaln_kernel,
        out_shape=jax.ShapeDtypeStruct((depth, b, n), F32),
        grid=(depth, n // tn),
        in_specs=[
            pl.BlockSpec((b, d), lambda l, j: (0, 0)),
            pl.BlockSpec((None, d, tn), lambda l, j: (l, 0, j)),
            pl.BlockSpec((None, 1, tn), lambda l, j: (l, 0, j)),
        ],
        out_specs=pl.BlockSpec((None, b, tn), lambda l, j: (l, 0, j)),
        compiler_params=_params(("parallel", "parallel")),
        name="adaln",
    )(c, ada_w, ada_b.reshape(depth, 1, n))


def _ffn_kernel(x_ref, mod_ref, prew_ref, postw_ref, wup_ref, wdown_ref, o_ref, *, weight):
    x = x_ref[...]
    u = (_rms(x) * (prew_ref[...] * (1.0 + mod_ref[1:2, :])) + mod_ref[0:1, :]).astype(BF16)
    gv = jnp.dot(u, wup_ref[...], preferred_element_type=F32)
    ff = gv.shape[1] // 2
    g, v = gv[:, :ff], gv[:, ff:]
    a = (g * jax.nn.sigmoid(g) * v).astype(BF16)
    y = jnp.dot(a, wdown_ref[...], preferred_element_type=F32)
    o_ref[...] = x + _rms(y) * (postw_ref[...] * (weight * mod_ref[2:3, :]))


def _ffn(h, mod3, pre_w, post_w, w_up, w_down, layer, slot, seq):
    t, d = h.shape
    ff = w_down.shape[2]
    tm = FFN_ROW_TILE
    per_batch = seq // tm
    resident = dict(pipeline_mode=pl.Buffered(1))
    return pl.pallas_call(
        functools.partial(_ffn_kernel, weight=MACARON_WEIGHT),
        out_shape=jax.ShapeDtypeStruct((t, d), F32),
        grid=(t // tm,),
        in_specs=[
            pl.BlockSpec((tm, d), lambda i: (i, 0)),
            pl.BlockSpec((None, 3, d), lambda i: (i // per_batch, 0, 0)),
            pl.BlockSpec((1, d), lambda i: (0, 0)),
            pl.BlockSpec((1, d), lambda i: (0, 0)),
            pl.BlockSpec((None, None, d, 2 * ff), lambda i: (layer, slot, 0, 0), **resident),
            pl.BlockSpec((None, None, ff, d), lambda i: (layer, slot, 0, 0), **resident),
        ],
        out_specs=pl.BlockSpec((tm, d), lambda i: (i, 0)),
        compiler_params=_params(("parallel",)),
        name="ffn",
    )(h, mod3, pre_w.reshape(1, d), post_w.reshape(1, d), w_up, w_down)


def _inproj_kernel(x_ref, mod_ref, prew_ref, wqkv_ref, wrest_ref, wgate_ref,
                   qkv1_ref, qkv4_ref, qkv16_ref, rest_ref, gate_ref, z_ref):
    u = _rms(x_ref[...]) * prew_ref[...] * (1.0 + mod_ref[1:2, :]) + mod_ref[0:1, :]
    u = u.astype(BF16)
    z = jnp.dot(u, wqkv_ref[...], preferred_element_type=F32)
    qkv1_ref[0] = z.astype(BF16)
    rest_ref[...] = jnp.dot(u, wrest_ref[...], preferred_element_type=F32).astype(BF16)
    gate_ref[...] = jnp.dot(u, wgate_ref[...], preferred_element_type=F32)
    rows = z.shape[0]
    slabs = z.shape[1] // LANES
    for c in range(slabs):
        z_ref[c] = z[:, c * LANES:(c + 1) * LANES]
    for dil, out_ref in ((DILATIONS[1], qkv4_ref), (DILATIONS[2], qkv16_ref)):
        for r in range(dil):
            for c in range(slabs):
                out_ref[r, :, c * LANES:(c + 1) * LANES] = z_ref[c, pl.ds(r, rows // dil, stride=dil), :].astype(BF16)


def _inproj(h, mod3, pre_w, w_qkv, w_rest, w_gate, batch, seq):
    t, d = h.shape
    tm = ROW_TILE
    per_batch = seq // tm
    n_qkv, n_rest, n_gate = w_qkv.shape[1], w_rest.shape[1], w_gate.shape[1]
    full = lambda i: (0, 0)
    by_residue = lambda i: (i // per_batch, 0, i % per_batch, 0)
    qkv_shapes = [jax.ShapeDtypeStruct((batch, dil, seq // dil, n_qkv), BF16) for dil in DILATIONS]
    qkv_specs = [pl.BlockSpec((None, dil, tm // dil, n_qkv), by_residue) for dil in DILATIONS]
    return pl.pallas_call(
        _inproj_kernel,
        out_shape=(
            *qkv_shapes,
            jax.ShapeDtypeStruct((t, n_rest), BF16),
            jax.ShapeDtypeStruct((t, n_gate), F32),
        ),
        grid=(t // tm,),
        in_specs=[
            pl.BlockSpec((tm, d), lambda i: (i, 0)),
            pl.BlockSpec((None, 3, d), lambda i: (i // per_batch, 0, 0)),
            pl.BlockSpec((1, d), full),
            pl.BlockSpec((d, n_qkv), full),
            pl.BlockSpec((d, n_rest), full),
            pl.BlockSpec((d, n_gate), full),
        ],
        out_specs=(
            *qkv_specs,
            pl.BlockSpec((tm, n_rest), lambda i: (i, 0)),
            pl.BlockSpec((tm, n_gate), lambda i: (i, 0)),
        ),
        scratch_shapes=[pltpu.VMEM((n_qkv // LANES, tm, LANES), F32)],
        compiler_params=_params(("parallel",)),
        name="inproj",
    )(h, mod3, pre_w.reshape(1, d), w_qkv, w_rest, w_gate)


QKV_W = 3 * ATT_WIDTH


def _attn_scores(q, kwin, bias, head_of_lane):
    q32 = q.astype(F32) * (ATT_HEAD_DIM ** -0.5)
    lhs = jnp.concatenate(
        [jnp.where(head_of_lane == h, q32, 0.0) for h in range(ATT_HEADS)], axis=0).astype(BF16)
    return lax.dot_general(lhs, kwin, (((1,), (1,)), ((), ())), preferred_element_type=F32) + bias


def _attn_softmax(s, head_of_lane):
    nq = s.shape[0] // ATT_HEADS
    m = jnp.max(s, axis=-1, keepdims=True)
    p = jnp.exp(s - m)
    l = jnp.sum(p, axis=-1, keepdims=True)
    mb = jnp.zeros((nq, ATT_WIDTH), F32)
    lb = jnp.zeros((nq, ATT_WIDTH), F32)
    for h in range(ATT_HEADS):
        rows = slice(h * nq, (h + 1) * nq)
        sel = head_of_lane == h
        mb = jnp.where(sel, m[rows], mb)
        lb = jnp.where(sel, l[rows], lb)
    return p.astype(BF16), mb, lb


def _attn_values(p, vwin, head_of_lane):
    nq = p.shape[0] // ATT_HEADS
    pv = jnp.dot(p, vwin, preferred_element_type=F32)
    num = pv[0:nq]
    for h in range(1, ATT_HEADS):
        num = jnp.where(head_of_lane == h, pv[h * nq:(h + 1) * nq], num)
    return num


def _attn_kernel(c1_ref, h1_ref, c4_ref, h4_ref, c16_ref, h16_ref, o_ref,
                 bias_ref, run_ref, tmp_ref, s_ref, p_ref):
    t = pl.program_id(1)
    nq = ATT_BLOCK
    head_of_lane = lax.broadcasted_iota(jnp.int32, (nq, ATT_WIDTH), 1) >> (ATT_HEAD_DIM.bit_length() - 1)

    qi = lax.broadcasted_iota(jnp.int32, (ATT_HEADS * nq, 2 * nq), 0) & (nq - 1)
    ki = lax.broadcasted_iota(jnp.int32, (ATT_HEADS * nq, 2 * nq), 1)
    ok = (ki >= qi) & (ki <= qi + ATT_SPAN)
    bias_ref[0] = jnp.where(ok, 0.0, NEG)
    bias_ref[1] = jnp.where(ok & (ki >= nq), 0.0, NEG)
    qs = slice(0, ATT_WIDTH)
    ks = slice(ATT_WIDTH, 2 * ATT_WIDTH)
    vs = slice(2 * ATT_WIDTH, 3 * ATT_WIDTH)

    def put(dst_ref, q_idx, val, start, stride):
        for half in range(2):
            piece = val[:, half * LANES:(half + 1) * LANES]
            if stride == 1:
                dst_ref[q_idx, half, pl.ds(start, nq), :] = piece
            else:
                dst_ref[q_idx, half, pl.ds(start, nq, stride=stride), :] = piece

    def branch(cur_ref, halo_ref, dil, dst_ref):
        blocks = ATT_TILE // dil // nq
        items = dil * blocks
        shift = blocks.bit_length() - 1

        def window(idx, cols):
            r = idx >> shift
            i = idx & (blocks - 1)
            own = cur_ref[r, pl.ds(pl.multiple_of(i * nq, nq), nq), cols]
            if blocks == 1:
                prev = halo_ref[r, :, cols]
            else:
                prev0 = pl.multiple_of(jnp.maximum(i - 1, 0) * nq, nq)
                prev = jnp.where(i == 0, halo_ref[r, :, cols], cur_ref[r, pl.ds(prev0, nq), cols])
            return jnp.concatenate([prev, own], axis=0)

        def start_row(idx):
            return (idx & (blocks - 1)) * (nq * dil) + (idx >> shift)

        s_ref[1] = jnp.zeros(s_ref.shape[1:], s_ref.dtype)
        p_ref[1] = jnp.zeros(p_ref.shape[1:], p_ref.dtype)

        def body(j, carry):
            slot, other = j & 1, (j + 1) & 1
            a = jnp.minimum(j, items - 1)
            b = jnp.clip(j - 1, 0, items - 1)
            c = jnp.clip(j - 2, 0, items - 1)

            put(dst_ref, 0, _attn_values(p_ref[other], window(c, vs), head_of_lane), start_row(c), dil)

            p, mb, lb = _attn_softmax(s_ref[other], head_of_lane)
            p_ref[slot] = p
            put(dst_ref, 1, mb, start_row(b), dil)
            put(dst_ref, 2, lb, start_row(b), dil)

            r = a >> shift
            i = a & (blocks - 1)
            no_past = jnp.where((i == 0) & (t == 0), 1, 0)
            q = cur_ref[r, pl.ds(pl.multiple_of(i * nq, nq), nq), qs]
            s_ref[slot] = _attn_scores(q, window(a, ks), bias_ref[no_past], head_of_lane)
            return carry
        lax.fori_loop(0, items + 2, body, 0)

    rows = 256

    def merge(final):
        def body(c, carry):
            sl = pl.ds(pl.multiple_of(c * rows, rows), rows)
            for half in range(2):
                n_r, m_r, l_r = (run_ref[q, half, sl, :] for q in range(3))
                n_t, m_t, l_t = (tmp_ref[q, half, sl, :] for q in range(3))
                mx = jnp.maximum(m_r, m_t)
                e_r = jnp.exp(m_r - mx)
                e_t = jnp.exp(m_t - mx)
                n_new = e_r * n_r + e_t * n_t
                l_new = e_r * l_r + e_t * l_t
                if final:
                    o_ref[sl, half * LANES:(half + 1) * LANES] = (n_new / l_new).astype(o_ref.dtype)
                else:
                    run_ref[0, half, sl, :] = n_new
                    run_ref[1, half, sl, :] = mx
                    run_ref[2, half, sl, :] = l_new
            return carry
        lax.fori_loop(0, ATT_TILE // rows, body, 0)

    branch(c1_ref, h1_ref, DILATIONS[0], run_ref)
    branch(c4_ref, h4_ref, DILATIONS[1], tmp_ref)
    merge(False)
    branch(c16_ref, h16_ref, DILATIONS[2], tmp_ref)
    merge(True)


def _attention(qkv_by_dilation, batch, seq):
    tiles = seq // ATT_TILE
    in_specs = []
    operands = []
    for dil, view in zip(DILATIONS, qkv_by_dilation):
        cur_rows = ATT_TILE // dil
        per_tile = cur_rows // ATT_BLOCK
        in_specs.append(pl.BlockSpec((None, dil, cur_rows, QKV_W), lambda b, t: (b, 0, t, 0)))
        in_specs.append(pl.BlockSpec(
            (None, dil, ATT_BLOCK, QKV_W),
            functools.partial(lambda b, t, per_tile: (b, 0, jnp.maximum(t * per_tile - 1, 0), 0), per_tile=per_tile)))
        operands += [view, view]
    return pl.pallas_call(
        _attn_kernel,
        out_shape=jax.ShapeDtypeStruct((batch, seq, ATT_WIDTH), BF16),
        grid=(batch, tiles),
        in_specs=in_specs,
        out_specs=pl.BlockSpec((None, ATT_TILE, ATT_WIDTH), lambda b, t: (b, t, 0)),
        scratch_shapes=[
            pltpu.VMEM((2, ATT_HEADS * ATT_BLOCK, 2 * ATT_BLOCK), F32),
            pltpu.VMEM((3, 2, ATT_TILE, LANES), F32),
            pltpu.VMEM((3, 2, ATT_TILE, LANES), F32),
            pltpu.VMEM((2, ATT_HEADS * ATT_BLOCK, 2 * ATT_BLOCK), F32),
            pltpu.VMEM((2, ATT_HEADS * ATT_BLOCK, 2 * ATT_BLOCK), BF16),
        ],
        compiler_params=_params(("parallel", "parallel")),
        name="dilated_attention",
    )(*operands)


def _split3(x):
    hi = x.astype(BF16)
    r1 = x - hi.astype(F32)
    mid = r1.astype(BF16)
    lo = (r1 - mid.astype(F32)).astype(BF16)
    return hi, mid, lo


def _log_sigmoid(x):
    return -(jnp.maximum(-x, 0.0) + jnp.log1p(jnp.exp(-jnp.abs(x))))


def _mix_kernel(rest_ref, gate_ref, poolw_ref, pscale_ref, convw_ref, convb_ref, wqk_ref, wv_ref,
                gbias_ref, normw_ref, skip_ref, pool_ref, hm_ref,
                phist_ref, chist_ref, cstate_ref, mstate_ref):
    s_idx = pl.program_id(1)
    n = SEQ_CHUNK
    dh = MLSTM_HEAD_DIM

    @pl.when(s_idx == 0)
    def _():
        phist_ref[...] = jnp.zeros_like(phist_ref)
        chist_ref[...] = jnp.zeros_like(chist_ref)
        cstate_ref[...] = jnp.zeros_like(cstate_ref)
        mstate_ref[...] = jnp.zeros_like(mstate_ref)

    p_in = rest_ref[:, 0:POOL_WIDTH].astype(F32)
    ext = jnp.concatenate([phist_ref[...], p_in], axis=0)
    phist_ref[...] = p_in[n - POOL_HIST:, :]
    sums = {1: ext}
    w = 1
    while w < POOL_WINDOWS[-1]:
        sums[2 * w] = sums[w] + pltpu.roll(sums[w], w, 0)
        w *= 2
    lane = lax.broadcasted_iota(jnp.int32, (n, POOL_WIDTH), 1)
    pos = lax.broadcasted_iota(jnp.int32, (n, POOL_WIDTH), 0) + s_idx * n
    win_sum = jnp.zeros((n, POOL_WIDTH), F32)
    win_len = jnp.zeros((n, POOL_WIDTH), jnp.int32)
    for g, wlen in enumerate(POOL_WINDOWS):
        sel = (lane >> (POOL_GROUP_DIM.bit_length() - 1)) == g
        win_sum = jnp.where(sel, sums[wlen][POOL_HIST:, :], win_sum)
        win_len = jnp.where(sel, wlen, win_len)
    count = jnp.minimum(pos + 1, win_len).astype(F32)
    dlt = win_sum / count - p_in
    y = jnp.dot(dlt.astype(BF16), poolw_ref[...], preferred_element_type=F32)
    pool_ref[...] = (y * pscale_ref[...]).astype(pool_ref.dtype)

    xm = rest_ref[:, POOL_WIDTH:POOL_WIDTH + MLSTM_WIDTH].astype(F32)
    og = rest_ref[:, POOL_WIDTH + MLSTM_WIDTH:POOL_WIDTH + 2 * MLSTM_WIDTH].astype(F32)
    cext = jnp.concatenate([chist_ref[...], xm], axis=0)
    chist_ref[...] = xm[n - CONV_HIST:, :]
    conv = cext * convw_ref[MLSTM_CONV - 1:MLSTM_CONV, :]
    for back in range(1, MLSTM_CONV):
        tap = MLSTM_CONV - 1 - back
        conv = conv + pltpu.roll(cext, back, 0) * convw_ref[tap:tap + 1, :]
    conv = conv[CONV_HIST:, :] + convb_ref[...]
    xc = conv * jax.nn.sigmoid(conv)
    xc_b = xc.astype(BF16)
    xm_b = xm.astype(BF16)

    gi = gate_ref[...] + gbias_ref[...]
    f_pre = pltpu.roll(gi, GATE_LANES - MLSTM_HEADS, 1)
    logf = _log_sigmoid(f_pre)
    row = lax.broadcasted_iota(jnp.int32, (n, n), 0)
    col = lax.broadcasted_iota(jnp.int32, (n, n), 1)
    causal = col <= row
    tri = jnp.where(causal, 1.0, 0.0).astype(BF16)
    bcum = sum(jnp.dot(tri, part, preferred_element_type=F32) for part in _split3(logf))
    x_ib = gi - bcum
    b_last = bcum[n - 1:n, :]
    m_prev = mstate_ref[0:1, :]
    a = b_last + x_ib
    a_max = jnp.max(a, axis=0, keepdims=True)
    m_new = jnp.maximum(b_last + m_prev, a_max)
    decay = jnp.exp(b_last + m_prev - m_new)
    w_state = jnp.exp(a - m_new)
    inter_log = bcum + m_prev
    x_row = x_ib.T
    mstate_ref[0:1, :] = m_new

    ones = jnp.ones((n, dh), F32)
    for h in range(MLSTM_HEADS):
        hs = slice(h * dh, (h + 1) * dh)
        qk = jnp.dot(xc_b[:, hs], wqk_ref[h], preferred_element_type=F32)
        q = (qk[:, :dh] * (dh ** -0.5)).astype(BF16)
        k = qk[:, dh:].astype(BF16)
        v = jnp.dot(xm_b[:, hs], wv_ref[h], preferred_element_type=F32)
        v_aug = jnp.concatenate([v, ones], axis=1)

        log_d = jnp.where(causal, bcum[:, h:h + 1] + x_row[h:h + 1, :], NEG)
        m_t = jnp.maximum(inter_log[:, h:h + 1], jnp.max(log_d, axis=-1, keepdims=True))
        inter_w = jnp.exp(inter_log[:, h:h + 1] - m_t)
        s = lax.dot_general(q, k, (((1,), (1,)), ((), ())), preferred_element_type=F32)
        sw = (s * jnp.exp(log_d - m_t)).astype(BF16)
        c_prev = cstate_ref[h]
        out = inter_w * jnp.dot(q, c_prev.astype(BF16), preferred_element_type=F32) \
            + jnp.dot(sw, v_aug.astype(BF16), preferred_element_type=F32)
        hh = out[:, :dh] / jnp.maximum(jnp.abs(out[:, dh:]), jnp.exp(-m_t))

        wv_state = (w_state[:, h:h + 1] * v_aug).astype(BF16)
        d_c = lax.dot_general(k, wv_state, (((0,), (0,)), ((), ())), preferred_element_type=F32)
        cstate_ref[h] = decay[:, h:h + 1] * c_prev + d_c

        hh = hh * jax.nn.sigmoid(og[:, hs])
        hh = _rms(hh) * normw_ref[:, hs]
        hm_ref[:, hs] = (hh + skip_ref[:, hs] * xc[:, hs]).astype(hm_ref.dtype)


def _mix(rest, gates, pool_bd, pool_scale, conv_w, conv_b, w_qk, w_v, gate_bias, norm_w, skip, batch, seq):
    n = SEQ_CHUNK
    n_rest = rest.shape[-1]
    rest = rest.reshape(batch, seq, n_rest)
    gates = gates.reshape(batch, seq, GATE_LANES)
    c2 = lambda b, s: (0, 0)
    c3 = lambda b, s: (0, 0, 0)
    return pl.pallas_call(
        _mix_kernel,
        out_shape=(
            jax.ShapeDtypeStruct((batch, seq, POOL_WIDTH), BF16),
            jax.ShapeDtypeStruct((batch, seq, MLSTM_WIDTH), BF16),
        ),
        grid=(batch, seq // n),
        in_specs=[
            pl.BlockSpec((None, n, n_rest), lambda b, s: (b, s, 0)),
            pl.BlockSpec((None, n, GATE_LANES), lambda b, s: (b, s, 0)),
            pl.BlockSpec(pool_bd.shape, c2),
            pl.BlockSpec(pool_scale.shape, c2),
            pl.BlockSpec(conv_w.shape, c2),
            pl.BlockSpec(conv_b.shape, c2),
            pl.BlockSpec(w_qk.shape, c3),
            pl.BlockSpec(w_v.shape, c3),
            pl.BlockSpec(gate_bias.shape, c2),
            pl.BlockSpec(norm_w.shape, c2),
            pl.BlockSpec(skip.shape, c2),
        ],
        out_specs=(
            pl.BlockSpec((None, n, POOL_WIDTH), lambda b, s: (b, s, 0)),
            pl.BlockSpec((None, n, MLSTM_WIDTH), lambda b, s: (b, s, 0)),
        ),
        scratch_shapes=[
            pltpu.VMEM((POOL_HIST, POOL_WIDTH), F32),
            pltpu.VMEM((CONV_HIST, MLSTM_WIDTH), F32),
            pltpu.VMEM((MLSTM_HEADS, MLSTM_HEAD_DIM, 2 * MLSTM_HEAD_DIM), F32),
            pltpu.VMEM((SUBLANES, GATE_LANES), F32),
        ],
        compiler_params=_params(("parallel", "arbitrary")),
        name="pool_mlstm",
    )(rest, gates, pool_bd, pool_scale, conv_w, conv_b, w_qk, w_v, gate_bias, norm_w, skip)


def _outproj_kernel(x_ref, att_ref, pool_ref, hm_ref, wa_ref, wp_ref, wh_ref, mod_ref, postw_ref, o_ref):
    y = jnp.dot(att_ref[...], wa_ref[...], preferred_element_type=F32)
    y = y + jnp.dot(pool_ref[...], wp_ref[...], preferred_element_type=F32)
    y = y + jnp.dot(hm_ref[...], wh_ref[...], preferred_element_type=F32)
    o_ref[...] = x_ref[...] + mod_ref[2:3, :] * (_rms(y) * postw_ref[...])


def _outproj(h, att, pool, hm, w_out, mod3, post_w, layer, seq):
    t, d = h.shape
    tm = ROW_TILE
    per_batch = seq // tm
    widths = (att.shape[-1], pool.shape[-1], hm.shape[-1])
    w_specs = []
    off = 0
    for wdt in widths:
        w_specs.append(pl.BlockSpec((None, wdt, d), functools.partial(lambda i, blk: (layer, blk, 0), blk=off // wdt)))
        off += wdt
    return pl.pallas_call(
        _outproj_kernel,
        out_shape=jax.ShapeDtypeStruct((t, d), F32),
        grid=(t // tm,),
        in_specs=[
            pl.BlockSpec((tm, d), lambda i: (i, 0)),
            pl.BlockSpec((tm, widths[0]), lambda i: (i, 0)),
            pl.BlockSpec((tm, widths[1]), lambda i: (i, 0)),
            pl.BlockSpec((tm, widths[2]), lambda i: (i, 0)),
            *w_specs,
            pl.BlockSpec((None, 3, d), lambda i: (i // per_batch, 0, 0)),
            pl.BlockSpec((1, d), lambda i: (0, 0)),
        ],
        out_specs=pl.BlockSpec((tm, d), lambda i: (i, 0)),
        compiler_params=_params(("parallel",)),
        name="outproj",
    )(h, att.reshape(t, widths[0]), pool.reshape(t, widths[1]), hm.reshape(t, widths[2]),
      w_out, w_out, w_out, mod3, post_w.reshape(1, d))


def kernel(x, c, ada_w, ada_b, pre_norm_w, post_norm_w, ffn_up, ffn_down, mix_in_w, mix_out_w, pool_w, pool_scale,
           mlstm_conv_w, mlstm_conv_b, mlstm_qkv_w, mlstm_gate_b, mlstm_norm_w, mlstm_skip):
    batch, seq, d = x.shape
    depth = ada_w.shape[0]
    assert d == ATT_WIDTH + POOL_WIDTH + MLSTM_WIDTH
    assert seq % ATT_TILE == 0 and seq % ROW_TILE == 0 and seq % FFN_ROW_TILE == 0 and seq % SEQ_CHUNK == 0

    up_b = ffn_up.astype(BF16)
    down_b = ffn_down.astype(BF16)
    out_b = mix_out_w.astype(BF16)
    n_main = QKV_W + POOL_WIDTH + 2 * MLSTM_WIDTH
    w_qkv = mix_in_w[:, :, :QKV_W].astype(BF16)
    w_rest = mix_in_w[:, :, QKV_W:n_main].astype(BF16)
    w_gate = jnp.pad(mix_in_w[:, :, n_main:], ((0, 0), (0, 0), (0, GATE_LANES - 2 * MLSTM_HEADS))).astype(BF16)
    gate_bias = jnp.pad(mlstm_gate_b.reshape(depth, 1, 2 * MLSTM_HEADS),
                        ((0, 0), (0, 0), (0, GATE_LANES - 2 * MLSTM_HEADS)))
    groups = len(POOL_WINDOWS)
    eye = jnp.eye(groups, dtype=pool_w.dtype)
    pool_bd = (pool_w[:, :, :, None, :] * eye[None, :, None, :, None]).reshape(depth, POOL_WIDTH, POOL_WIDTH).astype(BF16)
    w_qk = jnp.concatenate([mlstm_qkv_w[:, 0], mlstm_qkv_w[:, 1]], axis=-1).astype(BF16)
    w_v = mlstm_qkv_w[:, 2].astype(BF16)

    mod = _adaln(c, ada_w, ada_b).reshape(depth, batch, 9, d)

    h = x.reshape(batch * seq, d)
    for l in range(depth):
        h = _ffn(h, mod[l, :, 0:3], pre_norm_w[l, 0], post_norm_w[l, 0], up_b, down_b, l, 0, seq)
        *qkv, rest, gates = _inproj(h, mod[l, :, 3:6], pre_norm_w[l, 1], w_qkv[l], w_rest[l], w_gate[l], batch, seq)
        att = _attention(qkv, batch, seq)
        pool, hm = _mix(rest, gates, pool_bd[l], pool_scale[l].reshape(1, -1), mlstm_conv_w[l],
                        mlstm_conv_b[l].reshape(1, -1), w_qk[l], w_v[l], gate_bias[l],
                        mlstm_norm_w[l].reshape(1, -1), mlstm_skip[l].reshape(1, -1), batch, seq)
        h = _outproj(h, att, pool, hm, out_b, mod[l, :, 3:6], post_norm_w[l, 1], l, seq)
        h = _ffn(h, mod[l, :, 6:9], pre_norm_w[l, 2], post_norm_w[l, 2], up_b, down_b, l, 1, seq)
    return h.reshape(batch, seq, d)
```

```python
import functools

import jax
import jax.numpy as jnp
from jax import lax
from jax.experimental import pallas as pl
from jax.experimental.pallas import tpu as pltpu

F32 = jnp.float32
BF16 = jnp.bfloat16

ATT_HEADS = 4
ATT_HEAD_DIM = 64
ATT_WIDTH = ATT_HEADS * ATT_HEAD_DIM
DILATIONS = (1, 4, 16)
ATT_SPAN = 128
POOL_WINDOWS = (2, 4, 8, 16)
POOL_GROUP_DIM = 64
POOL_WIDTH = len(POOL_WINDOWS) * POOL_GROUP_DIM
MLSTM_HEADS = 4
MLSTM_HEAD_DIM = 128
MLSTM_WIDTH = MLSTM_HEADS * MLSTM_HEAD_DIM
MLSTM_CONV = 4
MACARON_WEIGHT = 0.5
EPS = 1e-6
NEG = -1e30

LANES = 128
SUBLANES = 8
VMEM_LIMIT_BYTES = 56 * 1024 * 1024

ROW_TILE = 1024
FFN_ROW_TILE = 1024
ATT_TILE = 2048
ATT_BLOCK = 128
SEQ_CHUNK = 256
POOL_HIST = 16
CONV_HIST = 8
GATE_LANES = LANES


def _params(sem, **flags):
    return pltpu.CompilerParams(dimension_semantics=sem, vmem_limit_bytes=VMEM_LIMIT_BYTES, flags=flags or None)


def _rms(x):
    return x * lax.rsqrt(jnp.mean(x * x, axis=-1, keepdims=True) + EPS)


def _adaln_kernel(c_ref, w_ref, b_ref, o_ref):
    c = c_ref[...]
    c_act = (c * jax.nn.sigmoid(c)).astype(BF16)
    o_ref[...] = jnp.dot(c_act, w_ref[...].astype(BF16), preferred_element_type=F32) + b_ref[...]


def _adaln(c, ada_w, ada_b):
    depth, d, n = ada_w.shape
    b = c.shape[0]
    tn = n // 8
    return pl.pallas_call(
        _adaln_kernel,
        out_shape=jax.ShapeDtypeStruct((depth, b, n), F32),
        grid=(depth, n // tn),
        in_specs=[
            pl.BlockSpec((b, d), lambda l, j: (0, 0)),
            pl.BlockSpec((None, d, tn), lambda l, j: (l, 0, j)),
            pl.BlockSpec((None, 1, tn), lambda l, j: (l, 0, j)),
        ],
        out_specs=pl.BlockSpec((None, b, tn), lambda l, j: (l, 0, j)),
        compiler_params=_params(("parallel", "parallel")),
        name="adaln",
    )(c, ada_w, ada_b.reshape(depth, 1, n))


def _ffn_kernel(x_ref, mod_ref, prew_ref, postw_ref, wup_ref, wdown_ref, o_ref, *, weight):
    x = x_ref[...]
    u = (_rms(x) * (prew_ref[...] * (1.0 + mod_ref[1:2, :])) + mod_ref[0:1, :]).astype(BF16)
    gv = jnp.dot(u, wup_ref[...], preferred_element_type=F32)
    ff = gv.shape[1] // 2
    g, v = gv[:, :ff], gv[:, ff:]
    a = (g * jax.nn.sigmoid(g) * v).astype(BF16)
    y = jnp.dot(a, wdown_ref[...], preferred_element_type=F32)
    o_ref[...] = x + _rms(y) * (postw_ref[...] * (weight * mod_ref[2:3, :]))


def _ffn(h, mod3, pre_w, post_w, w_up, w_down, layer, slot, seq):
    t, d = h.shape
    ff = w_down.shape[2]
    tm = FFN_ROW_TILE
    per_batch = seq // tm
    resident = dict(pipeline_mode=pl.Buffered(1))
    return pl.pallas_call(
        functools.partial(_ffn_kernel, weight=MACARON_WEIGHT),
        out_shape=jax.ShapeDtypeStruct((t, d), F32),
        grid=(t // tm,),
        in_specs=[
            pl.BlockSpec((tm, d), lambda i: (i, 0)),
            pl.BlockSpec((None, 3, d), lambda i: (i // per_batch, 0, 0)),
            pl.BlockSpec((1, d), lambda i: (0, 0)),
            pl.BlockSpec((1, d), lambda i: (0, 0)),
            pl.BlockSpec((None, None, d, 2 * ff), lambda i: (layer, slot, 0, 0), **resident),
            pl.BlockSpec((None, None, ff, d), lambda i: (layer, slot, 0, 0), **resident),
        ],
        out_specs=pl.BlockSpec((tm, d), lambda i: (i, 0)),
        compiler_params=_params(("parallel",)),
        name="ffn",
    )(h, mod3, pre_w.reshape(1, d), post_w.reshape(1, d), w_up, w_down)


def _inproj_kernel(x_ref, mod_ref, prew_ref, wqkv_ref, wrest_ref, wgate_ref,
                   qkv1_ref, qkv4_ref, qkv16_ref, rest_ref, gate_ref, z_ref):
    u = _rms(x_ref[...]) * prew_ref[...] * (1.0 + mod_ref[1:2, :]) + mod_ref[0:1, :]
    u = u.astype(BF16)
    z = jnp.dot(u, wqkv_ref[...], preferred_element_type=F32)
    qkv1_ref[0] = z.astype(BF16)
    rest_ref[...] = jnp.dot(u, wrest_ref[...], preferred_element_type=F32).astype(BF16)
    gate_ref[...] = jnp.dot(u, wgate_ref[...], preferred_element_type=F32)
    rows = z.shape[0]
    slabs = z.shape[1] // LANES
    for c in range(slabs):
        z_ref[c] = z[:, c * LANES:(c + 1) * LANES]
    for dil, out_ref in ((DILATIONS[1], qkv4_ref), (DILATIONS[2], qkv16_ref)):
        for r in range(dil):
            for c in range(slabs):
                out_ref[r, :, c * LANES:(c + 1) * LANES] = z_ref[c, pl.ds(r, rows // dil, stride=dil), :].astype(BF16)


def _inproj(h, mod3, pre_w, w_qkv, w_rest, w_gate, batch, seq):
    t, d = h.shape
    tm = ROW_TILE
    per_batch = seq // tm
    n_qkv, n_rest, n_gate = w_qkv.shape[1], w_rest.shape[1], w_gate.shape[1]
    full = lambda i: (0, 0)
    by_residue = lambda i: (i // per_batch, 0, i % per_batch, 0)
    qkv_shapes = [jax.ShapeDtypeStruct((batch, dil, seq // dil, n_qkv), BF16) for dil in DILATIONS]
    qkv_specs = [pl.BlockSpec((None, dil, tm // dil, n_qkv), by_residue) for dil in DILATIONS]
    return pl.pallas_call(
        _inproj_kernel,
        out_shape=(
            *qkv_shapes,
            jax.ShapeDtypeStruct((t, n_rest), BF16),
            jax.ShapeDtypeStruct((t, n_gate), F32),
        ),
        grid=(t // tm,),
        in_specs=[
            pl.BlockSpec((tm, d), lambda i: (i, 0)),
            pl.BlockSpec((None, 3, d), lambda i: (i // per_batch, 0, 0)),
            pl.BlockSpec((1, d), full),
            pl.BlockSpec((d, n_qkv), full),
            pl.BlockSpec((d, n_rest), full),
            pl.BlockSpec((d, n_gate), full),
        ],
        out_specs=(
            *qkv_specs,
            pl.BlockSpec((tm, n_rest), lambda i: (i, 0)),
            pl.BlockSpec((tm, n_gate), lambda i: (i, 0)),
        ),
        scratch_shapes=[pltpu.VMEM((n_qkv // LANES, tm, LANES), F32)],
        compiler_params=_params(("parallel",)),
        name="inproj",
    )(h, mod3, pre_w.reshape(1, d), w_qkv, w_rest, w_gate)


QKV_W = 3 * ATT_WIDTH


def _attn_scores(q, kwin, bias, head_of_lane):
    q32 = q.astype(F32) * (ATT_HEAD_DIM ** -0.5)
    lhs = jnp.concatenate(
        [jnp.where(head_of_lane == h, q32, 0.0) for h in range(ATT_HEADS)], axis=0).astype(BF16)
    return lax.dot_general(lhs, kwin, (((1,), (1,)), ((), ())), preferred_element_type=F32) + bias


def _by_lane_half(x, low):
    nq = x.shape[0] // ATT_HEADS
    out = []
    for half in range(2):
        first = x[(2 * half) * nq:(2 * half + 1) * nq]
        second = x[(2 * half + 1) * nq:(2 * half + 2) * nq]
        if x.shape[1] != 1:
            first = first[:, half * LANES:(half + 1) * LANES]
            second = second[:, half * LANES:(half + 1) * LANES]
        out.append(jnp.where(low, first, second))
    return out


def _attn_kernel(c1_ref, h1_ref, c4_ref, h4_ref, c16_ref, h16_ref, o_ref, bias_ref, dst_ref, s_ref, p_ref):
    t = pl.program_id(1)
    nq = ATT_BLOCK
    items = ATT_TILE // nq
    head_of_lane = lax.broadcasted_iota(jnp.int32, (nq, ATT_WIDTH), 1) >> (ATT_HEAD_DIM.bit_length() - 1)
    low = lax.broadcasted_iota(jnp.int32, (nq, LANES), 1) < ATT_HEAD_DIM

    qi = lax.broadcasted_iota(jnp.int32, (ATT_HEADS * nq, 2 * nq), 0) & (nq - 1)
    ki = lax.broadcasted_iota(jnp.int32, (ATT_HEADS * nq, 2 * nq), 1)
    ok = (ki >= qi) & (ki <= qi + ATT_SPAN)
    bias_ref[0] = jnp.where(ok, 0.0, NEG)
    bias_ref[1] = jnp.where(ok & (ki >= nq), 0.0, NEG)
    qs = slice(0, ATT_WIDTH)
    ks = slice(ATT_WIDTH, 2 * ATT_WIDTH)
    vs = slice(2 * ATT_WIDTH, 3 * ATT_WIDTH)

    branches = tuple(zip((c1_ref, c4_ref, c16_ref), (h1_ref, h4_ref, h16_ref), DILATIONS))

    def split(dil, idx):
        blocks = items // dil
        return idx >> (blocks.bit_length() - 1), idx & (blocks - 1)

    def window(cur_ref, halo_ref, dil, idx, cols):
        r, i = split(dil, idx)
        own = cur_ref[r, pl.ds(pl.multiple_of(i * nq, nq), nq), cols]
        if items == dil:
            prev = halo_ref[r, :, cols]
        else:
            prev0 = pl.multiple_of(jnp.maximum(i - 1, 0) * nq, nq)
            prev = jnp.where(i == 0, halo_ref[r, :, cols], cur_ref[r, pl.ds(prev0, nq), cols])
        return jnp.concatenate([prev, own], axis=0)

    def put(branch, quantity, halves, dil, idx):
        r, i = split(dil, idx)
        start = i * (nq * dil) + r
        rows = pl.ds(start, nq) if dil == 1 else pl.ds(start, nq, stride=dil)
        for half, piece in enumerate(halves):
            dst_ref[branch, quantity, half, rows, :] = piece

    for branch in range(len(branches)):
        s_ref[branch, 1] = jnp.zeros(s_ref.shape[2:], s_ref.dtype)
        p_ref[branch, 1] = jnp.zeros(p_ref.shape[2:], p_ref.dtype)

    def trip(j, slot):
        other = 1 - slot
        item_a = jnp.minimum(j, items - 1)
        item_b = jnp.clip(j - 1, 0, items - 1)
        item_c = jnp.clip(j - 2, 0, items - 1)

        for branch, (cur_ref, halo_ref, dil) in enumerate(branches):
            pv = jnp.dot(p_ref[branch, other], window(cur_ref, halo_ref, dil, item_c, vs),
                         preferred_element_type=F32)
            put(branch, 0, _by_lane_half(pv, low), dil, item_c)

        for branch, (cur_ref, halo_ref, dil) in enumerate(branches):
            s = s_ref[branch, other]
            m = jnp.max(s, axis=-1, keepdims=True)
            p = jnp.exp(s - m)
            l = jnp.sum(p, axis=-1, keepdims=True)
            p_ref[branch, slot] = p.astype(BF16)
            put(branch, 1, _by_lane_half(m, low), dil, item_b)
            put(branch, 2, _by_lane_half(l, low), dil, item_b)

        for branch, (cur_ref, halo_ref, dil) in enumerate(branches):
            r, i = split(dil, item_a)
            no_past = jnp.where((i == 0) & (t == 0), 1, 0)
            q = cur_ref[r, pl.ds(pl.multiple_of(i * nq, nq), nq), qs]
            s_ref[branch, slot] = _attn_scores(q, window(cur_ref, halo_ref, dil, item_a, ks),
                                               bias_ref[no_past], head_of_lane)

    def body(pair, carry):
        trip(2 * pair, 0)
        trip(2 * pair + 1, 1)
        return carry
    lax.fori_loop(0, (items + 2) // 2, body, 0)

    rows = ATT_BLOCK

    def merge(c, carry):
        sl = pl.ds(pl.multiple_of(c * rows, rows), rows)
        for half in range(2):
            maxes = [dst_ref[branch, 1, half, sl, :] for branch in range(len(branches))]
            top = functools.reduce(jnp.maximum, maxes)
            num = 0.0
            den = 0.0
            for branch, m_b in enumerate(maxes):
                e = jnp.exp(m_b - top)
                num = num + e * dst_ref[branch, 0, half, sl, :]
                den = den + e * dst_ref[branch, 2, half, sl, :]
            o_ref[sl, half * LANES:(half + 1) * LANES] = (num / den).astype(o_ref.dtype)
        return carry
    lax.fori_loop(0, ATT_TILE // rows, merge, 0)


def _attention(qkv_by_dilation, batch, seq):
    tiles = seq // ATT_TILE
    in_specs = []
    operands = []
    for dil, view in zip(DILATIONS, qkv_by_dilation):
        cur_rows = ATT_TILE // dil
        per_tile = cur_rows // ATT_BLOCK
        in_specs.append(pl.BlockSpec((None, dil, cur_rows, QKV_W), lambda b, t: (b, 0, t, 0)))
        in_specs.append(pl.BlockSpec(
            (None, dil, ATT_BLOCK, QKV_W),
            functools.partial(lambda b, t, per_tile: (b, 0, jnp.maximum(t * per_tile - 1, 0), 0), per_tile=per_tile)))
        operands += [view, view]
    stacked = (ATT_HEADS * ATT_BLOCK, 2 * ATT_BLOCK)
    return pl.pallas_call(
        _attn_kernel,
        out_shape=jax.ShapeDtypeStruct((batch, seq, ATT_WIDTH), BF16),
        grid=(batch, tiles),
        in_specs=in_specs,
        out_specs=pl.BlockSpec((None, ATT_TILE, ATT_WIDTH), lambda b, t: (b, t, 0)),
        scratch_shapes=[
            pltpu.VMEM((2,) + stacked, F32),
            pltpu.VMEM((len(DILATIONS), 3, 2, ATT_TILE, LANES), F32),
            pltpu.VMEM((len(DILATIONS), 2) + stacked, F32),
            pltpu.VMEM((len(DILATIONS), 2) + stacked, BF16),
        ],
        compiler_params=_params(("parallel", "parallel")),
        name="dilated_attention",
    )(*operands)


def _split3(x):
    hi = x.astype(BF16)
    r1 = x - hi.astype(F32)
    mid = r1.astype(BF16)
    lo = (r1 - mid.astype(F32)).astype(BF16)
    return hi, mid, lo


def _log_sigmoid(x):
    return -(jnp.maximum(-x, 0.0) + jnp.log1p(jnp.exp(-jnp.abs(x))))


def _mix_kernel(rest_ref, gate_ref, poolw_ref, pscale_ref, convw_ref, convb_ref, wqk_ref, wv_ref,
                gbias_ref, normw_ref, skip_ref, pool_ref, hm_ref,
                phist_ref, chist_ref, cstate_ref, mstate_ref):
    s_idx = pl.program_id(1)
    n = SEQ_CHUNK
    dh = MLSTM_HEAD_DIM

    @pl.when(s_idx == 0)
    def _():
        phist_ref[...] = jnp.zeros_like(phist_ref)
        chist_ref[...] = jnp.zeros_like(chist_ref)
        cstate_ref[...] = jnp.zeros_like(cstate_ref)
        mstate_ref[...] = jnp.zeros_like(mstate_ref)

    p_in = rest_ref[:, 0:POOL_WIDTH].astype(F32)
    ext = jnp.concatenate([phist_ref[...], p_in], axis=0)
    phist_ref[...] = p_in[n - POOL_HIST:, :]
    sums = {1: ext}
    w = 1
    while w < POOL_WINDOWS[-1]:
        sums[2 * w] = sums[w] + pltpu.roll(sums[w], w, 0)
        w *= 2
    lane = lax.broadcasted_iota(jnp.int32, (n, POOL_WIDTH), 1)
    pos = lax.broadcasted_iota(jnp.int32, (n, POOL_WIDTH), 0) + s_idx * n
    win_sum = jnp.zeros((n, POOL_WIDTH), F32)
    win_len = jnp.zeros((n, POOL_WIDTH), jnp.int32)
    for g, wlen in enumerate(POOL_WINDOWS):
        sel = (lane >> (POOL_GROUP_DIM.bit_length() - 1)) == g
        win_sum = jnp.where(sel, sums[wlen][POOL_HIST:, :], win_sum)
        win_len = jnp.where(sel, wlen, win_len)
    count = jnp.minimum(pos + 1, win_len).astype(F32)
    dlt = win_sum / count - p_in
    y = jnp.dot(dlt.astype(BF16), poolw_ref[...], preferred_element_type=F32)
    pool_ref[...] = (y * pscale_ref[...]).astype(pool_ref.dtype)

    xm = rest_ref[:, POOL_WIDTH:POOL_WIDTH + MLSTM_WIDTH].astype(F32)
    og = rest_ref[:, POOL_WIDTH + MLSTM_WIDTH:POOL_WIDTH + 2 * MLSTM_WIDTH].astype(F32)
    cext = jnp.concatenate([chist_ref[...], xm], axis=0)
    chist_ref[...] = xm[n - CONV_HIST:, :]
    conv = cext * convw_ref[MLSTM_CONV - 1:MLSTM_CONV, :]
    for back in range(1, MLSTM_CONV):
        tap = MLSTM_CONV - 1 - back
        conv = conv + pltpu.roll(cext, back, 0) * convw_ref[tap:tap + 1, :]
    conv = conv[CONV_HIST:, :] + convb_ref[...]
    xc = conv * jax.nn.sigmoid(conv)
    xc_b = xc.astype(BF16)
    xm_b = xm.astype(BF16)

    gi = gate_ref[...] + gbias_ref[...]
    f_pre = pltpu.roll(gi, GATE_LANES - MLSTM_HEADS, 1)
    logf = _log_sigmoid(f_pre)
    row = lax.broadcasted_iota(jnp.int32, (n, n), 0)
    col = lax.broadcasted_iota(jnp.int32, (n, n), 1)
    causal = col <= row
    tri = jnp.where(causal, 1.0, 0.0).astype(BF16)
    bcum = sum(jnp.dot(tri, part, preferred_element_type=F32) for part in _split3(logf))
    x_ib = gi - bcum
    b_last = bcum[n - 1:n, :]
    m_prev = mstate_ref[0:1, :]
    a = b_last + x_ib
    a_max = jnp.max(a, axis=0, keepdims=True)
    m_new = jnp.maximum(b_last + m_prev, a_max)
    decay = jnp.exp(b_last + m_prev - m_new)
    w_state = jnp.exp(a - m_new)
    inter_log = bcum + m_prev
    x_row = x_ib.T
    mstate_ref[0:1, :] = m_new

    ones = jnp.ones((n, dh), F32)
    for h in range(MLSTM_HEADS):
        hs = slice(h * dh, (h + 1) * dh)
        qk = jnp.dot(xc_b[:, hs], wqk_ref[h], preferred_element_type=F32)
        q = (qk[:, :dh] * (dh ** -0.5)).astype(BF16)
        k = qk[:, dh:].astype(BF16)
        v = jnp.dot(xm_b[:, hs], wv_ref[h], preferred_element_type=F32)
        v_aug = jnp.concatenate([v, ones], axis=1)

        log_d = jnp.where(causal, bcum[:, h:h + 1] + x_row[h:h + 1, :], NEG)
        m_t = jnp.maximum(inter_log[:, h:h + 1], jnp.max(log_d, axis=-1, keepdims=True))
        inter_w = jnp.exp(inter_log[:, h:h + 1] - m_t)
        s = lax.dot_general(q, k, (((1,), (1,)), ((), ())), preferred_element_type=F32)
        sw = (s * jnp.exp(log_d - m_t)).astype(BF16)
        c_prev = cstate_ref[h]
        out = inter_w * jnp.dot(q, c_prev.astype(BF16), preferred_element_type=F32) \
            + jnp.dot(sw, v_aug.astype(BF16), preferred_element_type=F32)
        hh = out[:, :dh] / jnp.maximum(jnp.abs(out[:, dh:]), jnp.exp(-m_t))

        wv_state = (w_state[:, h:h + 1] * v_aug).astype(BF16)
        d_c = lax.dot_general(k, wv_state, (((0,), (0,)), ((), ())), preferred_element_type=F32)
        cstate_ref[h] = decay[:, h:h + 1] * c_prev + d_c

        hh = hh * jax.nn.sigmoid(og[:, hs])
        hh = _rms(hh) * normw_ref[:, hs]
        hm_ref[:, hs] = (hh + skip_ref[:, hs] * xc[:, hs]).astype(hm_ref.dtype)


def _mix(rest, gates, pool_bd, pool_scale, conv_w, conv_b, w_qk, w_v, gate_bias, norm_w, skip, batch, seq):
    n = SEQ_CHUNK
    n_rest = rest.shape[-1]
    rest = rest.reshape(batch, seq, n_rest)
    gates = gates.reshape(batch, seq, GATE_LANES)
    c2 = lambda b, s: (0, 0)
    c3 = lambda b, s: (0, 0, 0)
    return pl.pallas_call(
        _mix_kernel,
        out_shape=(
            jax.ShapeDtypeStruct((batch, seq, POOL_WIDTH), BF16),
            jax.ShapeDtypeStruct((batch, seq, MLSTM_WIDTH), BF16),
        ),
        grid=(batch, seq // n),
        in_specs=[
            pl.BlockSpec((None, n, n_rest), lambda b, s: (b, s, 0)),
            pl.BlockSpec((None, n, GATE_LANES), lambda b, s: (b, s, 0)),
            pl.BlockSpec(pool_bd.shape, c2),
            pl.BlockSpec(pool_scale.shape, c2),
            pl.BlockSpec(conv_w.shape, c2),
            pl.BlockSpec(conv_b.shape, c2),
            pl.BlockSpec(w_qk.shape, c3),
            pl.BlockSpec(w_v.shape, c3),
            pl.BlockSpec(gate_bias.shape, c2),
            pl.BlockSpec(norm_w.shape, c2),
            pl.BlockSpec(skip.shape, c2),
        ],
        out_specs=(
            pl.BlockSpec((None, n, POOL_WIDTH), lambda b, s: (b, s, 0)),
            pl.BlockSpec((None, n, MLSTM_WIDTH), lambda b, s: (b, s, 0)),
        ),
        scratch_shapes=[
            pltpu.VMEM((POOL_HIST, POOL_WIDTH), F32),
            pltpu.VMEM((CONV_HIST, MLSTM_WIDTH), F32),
            pltpu.VMEM((MLSTM_HEADS, MLSTM_HEAD_DIM, 2 * MLSTM_HEAD_DIM), F32),
            pltpu.VMEM((SUBLANES, GATE_LANES), F32),
        ],
        compiler_params=_params(("parallel", "arbitrary")),
        name="pool_mlstm",
    )(rest, gates, pool_bd, pool_scale, conv_w, conv_b, w_qk, w_v, gate_bias, norm_w, skip)


def _outproj_kernel(x_ref, att_ref, pool_ref, hm_ref, wa_ref, wp_ref, wh_ref, mod_ref, postw_ref, o_ref):
    y = jnp.dot(att_ref[...], wa_ref[...], preferred_element_type=F32)
    y = y + jnp.dot(pool_ref[...], wp_ref[...], preferred_element_type=F32)
    y = y + jnp.dot(hm_ref[...], wh_ref[...], preferred_element_type=F32)
    o_ref[...] = x_ref[...] + mod_ref[2:3, :] * (_rms(y) * postw_ref[...])


def _outproj(h, att, pool, hm, w_out, mod3, post_w, layer, seq):
    t, d = h.shape
    tm = ROW_TILE
    per_batch = seq // tm
    widths = (att.shape[-1], pool.shape[-1], hm.shape[-1])
    w_specs = []
    off = 0
    for wdt in widths:
        w_specs.append(pl.BlockSpec((None, wdt, d), functools.partial(lambda i, blk: (layer, blk, 0), blk=off // wdt)))
        off += wdt
    return pl.pallas_call(
        _outproj_kernel,
        out_shape=jax.ShapeDtypeStruct((t, d), F32),
        grid=(t // tm,),
        in_specs=[
            pl.BlockSpec((tm, d), lambda i: (i, 0)),
            pl.BlockSpec((tm, widths[0]), lambda i: (i, 0)),
            pl.BlockSpec((tm, widths[1]), lambda i: (i, 0)),
            pl.BlockSpec((tm, widths[2]), lambda i: (i, 0)),
            *w_specs,
            pl.BlockSpec((None, 3, d), lambda i: (i // per_batch, 0, 0)),
            pl.BlockSpec((1, d), lambda i: (0, 0)),
        ],
        out_specs=pl.BlockSpec((tm, d), lambda i: (i, 0)),
        compiler_params=_params(("parallel",)),
        name="outproj",
    )(h, att.reshape(t, widths[0]), pool.reshape(t, widths[1]), hm.reshape(t, widths[2]),
      w_out, w_out, w_out, mod3, post_w.reshape(1, d))


def kernel(x, c, ada_w, ada_b, pre_norm_w, post_norm_w, ffn_up, ffn_down, mix_in_w, mix_out_w, pool_w, pool_scale,
           mlstm_conv_w, mlstm_conv_b, mlstm_qkv_w, mlstm_gate_b, mlstm_norm_w, mlstm_skip):
    batch, seq, d = x.shape
    depth = ada_w.shape[0]
    assert d == ATT_WIDTH + POOL_WIDTH + MLSTM_WIDTH
    assert seq % ATT_TILE == 0 and seq % ROW_TILE == 0 and seq % FFN_ROW_TILE == 0 and seq % SEQ_CHUNK == 0

    up_b = ffn_up.astype(BF16)
    down_b = ffn_down.astype(BF16)
    out_b = mix_out_w.astype(BF16)
    n_main = QKV_W + POOL_WIDTH + 2 * MLSTM_WIDTH
    w_qkv = mix_in_w[:, :, :QKV_W].astype(BF16)
    w_rest = mix_in_w[:, :, QKV_W:n_main].astype(BF16)
    w_gate = jnp.pad(mix_in_w[:, :, n_main:], ((0, 0), (0, 0), (0, GATE_LANES - 2 * MLSTM_HEADS))).astype(BF16)
    gate_bias = jnp.pad(mlstm_gate_b.reshape(depth, 1, 2 * MLSTM_HEADS),
                        ((0, 0), (0, 0), (0, GATE_LANES - 2 * MLSTM_HEADS)))
    groups = len(POOL_WINDOWS)
    eye = jnp.eye(groups, dtype=pool_w.dtype)
    pool_bd = (pool_w[:, :, :, None, :] * eye[None, :, None, :, None]).reshape(depth, POOL_WIDTH, POOL_WIDTH).astype(BF16)
    w_qk = jnp.concatenate([mlstm_qkv_w[:, 0], mlstm_qkv_w[:, 1]], axis=-1).astype(BF16)
    w_v = mlstm_qkv_w[:, 2].astype(BF16)

    mod = _adaln(c, ada_w, ada_b).reshape(depth, batch, 9, d)

    h = x.reshape(batch * seq, d)
    for l in range(depth):
        h = _ffn(h, mod[l, :, 0:3], pre_norm_w[l, 0], post_norm_w[l, 0], up_b, down_b, l, 0, seq)
        *qkv, rest, gates = _inproj(h, mod[l, :, 3:6], pre_norm_w[l, 1], w_qkv[l], w_rest[l], w_gate[l], batch, seq)
        att = _attention(qkv, batch, seq)
        pool, hm = _mix(rest, gates, pool_bd[l], pool_scale[l].reshape(1, -1), mlstm_conv_w[l],
                        mlstm_conv_b[l].reshape(1, -1), w_qk[l], w_v[l], gate_bias[l],
                        mlstm_norm_w[l].reshape(1, -1), mlstm_skip[l].reshape(1, -1), batch, seq)
        h = _outproj(h, att, pool, hm, out_b, mod[l, :, 3:6], post_norm_w[l, 1], l, seq)
        h = _ffn(h, mod[l, :, 6:9], pre_norm_w[l, 2], post_norm_w[l, 2], up_b, down_b, l, 1, seq)
    return h.reshape(batch, seq, d)
```

```python
import functools

import jax
import jax.numpy as jnp
from jax import lax
from jax.experimental import pallas as pl
from jax.experimental.pallas import tpu as pltpu

F32 = jnp.float32
BF16 = jnp.bfloat16

ATT_HEADS = 4
ATT_HEAD_DIM = 64
ATT_WIDTH = ATT_HEADS * ATT_HEAD_DIM
DILATIONS = (1, 4, 16)
ATT_SPAN = 128
POOL_WINDOWS = (2, 4, 8, 16)
POOL_GROUP_DIM = 64
POOL_WIDTH = len(POOL_WINDOWS) * POOL_GROUP_DIM
MLSTM_HEADS = 4
MLSTM_HEAD_DIM = 128
MLSTM_WIDTH = MLSTM_HEADS * MLSTM_HEAD_DIM
MLSTM_CONV = 4
MACARON_WEIGHT = 0.5
EPS = 1e-6
NEG = -1e30

LANES = 128
SUBLANES = 8
VMEM_LIMIT_BYTES = 56 * 1024 * 1024

ROW_TILE = 1024
ROW_PARTS = 2
ATT_TILE = 2048
ATT_BLOCK = 128
SEQ_CHUNK = 256
POOL_HIST = 16
CONV_HIST = 8
GATE_LANES = LANES


def _params(sem, **flags):
    return pltpu.CompilerParams(dimension_semantics=sem, vmem_limit_bytes=VMEM_LIMIT_BYTES, flags=flags or None)


def _rms(x):
    return x * lax.rsqrt(jnp.mean(x * x, axis=-1, keepdims=True) + EPS)


def _adaln_kernel(c_ref, w_ref, b_ref, o_ref):
    c = c_ref[...]
    c_act = (c * jax.nn.sigmoid(c)).astype(BF16)
    o_ref[...] = jnp.dot(c_act, w_ref[...].astype(BF16), preferred_element_type=F32) + b_ref[...]


def _adaln(c, ada_w, ada_b):
    depth, d, n = ada_w.shape
    b = c.shape[0]
    tn = n // 8
    return pl.pallas_call(
        _adaln_kernel,
        out_shape=jax.ShapeDtypeStruct((depth, b, n), F32),
        grid=(depth, n // tn),
        in_specs=[
            pl.BlockSpec((b, d), lambda l, j: (0, 0)),
            pl.BlockSpec((None, d, tn), lambda l, j: (l, 0, j)),
            pl.BlockSpec((None, 1, tn), lambda l, j: (l, 0, j)),
        ],
        out_specs=pl.BlockSpec((None, b, tn), lambda l, j: (l, 0, j)),
        compiler_params=_params(("parallel", "parallel")),
        name="adaln",
    )(c, ada_w, ada_b.reshape(depth, 1, n))


def _modulated(x, mod_ref, prew_ref, sub):
    scale = prew_ref[sub:sub + 1, :] * (1.0 + mod_ref[3 * sub + 1:3 * sub + 2, :])
    return (_rms(x) * scale + mod_ref[3 * sub:3 * sub + 1, :]).astype(BF16)


def _part(ref, part):
    rows = ref.shape[0] // ROW_PARTS
    return slice(part * rows, (part + 1) * rows)


def _ffn_kernel(x_ref, mod_ref, prew_ref, postw_ref, wup_ref, wdown_ref, o_ref, *, sub):
    gate = postw_ref[sub:sub + 1, :] * (MACARON_WEIGHT * mod_ref[3 * sub + 2:3 * sub + 3, :])
    u = _modulated(x_ref[_part(x_ref, 0), :], mod_ref, prew_ref, sub)
    for part in range(ROW_PARTS):
        rows = _part(x_ref, part)
        gv = jnp.dot(u, wup_ref[...], preferred_element_type=F32)
        if part + 1 < ROW_PARTS:
            u = _modulated(x_ref[_part(x_ref, part + 1), :], mod_ref, prew_ref, sub)
        ff = gv.shape[1] // 2
        g, v = gv[:, :ff], gv[:, ff:]
        a = (g * jax.nn.sigmoid(g) * v).astype(BF16)
        y = jnp.dot(a, wdown_ref[...], preferred_element_type=F32)
        o_ref[rows, :] = x_ref[rows, :] + _rms(y) * gate


def _inproj_kernel(x_ref, mod_ref, prew_ref, wqkv_ref, wrest_ref, wgate_ref,
                   qkv1_ref, qkv4_ref, qkv16_ref, rest_ref, gate_ref, z_ref):
    slabs = z_ref.shape[0]

    def by_residue(part):
        rows = _part(x_ref, part)
        n = rows.stop - rows.start
        for dil, out_ref in ((DILATIONS[1], qkv4_ref), (DILATIONS[2], qkv16_ref)):
            per = n // dil
            for r in range(dil):
                for c in range(slabs):
                    out_ref[r, part * per:(part + 1) * per, c * LANES:(c + 1) * LANES] = (
                        z_ref[c, pl.ds(rows.start + r, per, stride=dil), :].astype(BF16))

    u = _modulated(x_ref[_part(x_ref, 0), :], mod_ref, prew_ref, 1)
    for part in range(ROW_PARTS):
        rows = _part(x_ref, part)
        z = jnp.dot(u, wqkv_ref[...], preferred_element_type=F32)
        qkv1_ref[0, rows, :] = z.astype(BF16)
        for c in range(slabs):
            z_ref[c, rows, :] = z[:, c * LANES:(c + 1) * LANES]
        rest_ref[rows, :] = jnp.dot(u, wrest_ref[...], preferred_element_type=F32).astype(BF16)
        gate_ref[rows, :] = jnp.dot(u, wgate_ref[...], preferred_element_type=F32)
        if part + 1 < ROW_PARTS:
            u = _modulated(x_ref[_part(x_ref, part + 1), :], mod_ref, prew_ref, 1)
        by_residue(part)


def _outproj_kernel(x_ref, att_ref, pool_ref, hm_ref, mod_ref, postw_ref, wa_ref, wp_ref, wh_ref, o_ref):
    gate = postw_ref[1:2, :] * mod_ref[5:6, :]
    ys = []
    for part in range(ROW_PARTS):
        rows = _part(x_ref, part)
        y = jnp.dot(att_ref[rows, :], wa_ref[...], preferred_element_type=F32)
        y = y + jnp.dot(pool_ref[rows, :], wp_ref[...], preferred_element_type=F32)
        ys.append(y + jnp.dot(hm_ref[rows, :], wh_ref[...], preferred_element_type=F32))
    for part in range(ROW_PARTS):
        rows = _part(x_ref, part)
        o_ref[rows, :] = x_ref[rows, :] + _rms(ys[part]) * gate


def _resident(shape, index):
    return pl.BlockSpec(shape, index, pipeline_mode=pl.Buffered(1))


def _row_specs(d, seq):
    tm = ROW_TILE
    per_batch = seq // tm
    return (pl.BlockSpec((tm, d), lambda i: (i, 0)),
            pl.BlockSpec((None, 9, d), lambda i: (i // per_batch, 0, 0)),
            pl.BlockSpec((3, d), lambda i: (0, 0)))


def _ffn(h, mod, pre_w, post_w, w_up, w_down, layer, slot, seq):
    t, d = h.shape
    ff = w_down.shape[2]
    rows, mods, gains = _row_specs(d, seq)
    return pl.pallas_call(
        functools.partial(_ffn_kernel, sub=2 * slot),
        out_shape=jax.ShapeDtypeStruct((t, d), F32),
        grid=(t // ROW_TILE,),
        in_specs=[
            rows, mods, gains, gains,
            _resident((None, None, d, 2 * ff), lambda i: (layer, slot, 0, 0)),
            _resident((None, None, ff, d), lambda i: (layer, slot, 0, 0)),
        ],
        out_specs=rows,
        compiler_params=_params(("parallel",)),
        name="ffn",
    )(h, mod, pre_w, post_w, w_up, w_down)


def _inproj(h, mod, pre_w, w_qkv, w_rest, w_gate, layer, batch, seq):
    t, d = h.shape
    tm = ROW_TILE
    per_batch = seq // tm
    n_qkv, n_rest, n_gate = w_qkv.shape[2], w_rest.shape[2], w_gate.shape[2]
    rows, mods, gains = _row_specs(d, seq)
    by_residue = lambda i: (i // per_batch, 0, i % per_batch, 0)
    qkv_shapes = [jax.ShapeDtypeStruct((batch, dil, seq // dil, n_qkv), BF16) for dil in DILATIONS]
    qkv_specs = [pl.BlockSpec((None, dil, tm // dil, n_qkv), by_residue) for dil in DILATIONS]
    return pl.pallas_call(
        _inproj_kernel,
        out_shape=(
            *qkv_shapes,
            jax.ShapeDtypeStruct((t, n_rest), BF16),
            jax.ShapeDtypeStruct((t, n_gate), F32),
        ),
        grid=(t // tm,),
        in_specs=[
            rows, mods, gains,
            _resident((None, d, n_qkv), lambda i: (layer, 0, 0)),
            _resident((None, d, n_rest), lambda i: (layer, 0, 0)),
            _resident((None, d, n_gate), lambda i: (layer, 0, 0)),
        ],
        out_specs=(
            *qkv_specs,
            pl.BlockSpec((tm, n_rest), lambda i: (i, 0)),
            pl.BlockSpec((tm, n_gate), lambda i: (i, 0)),
        ),
        scratch_shapes=[pltpu.VMEM((n_qkv // LANES, tm, LANES), F32)],
        compiler_params=_params(("parallel",)),
        name="inproj",
    )(h, mod, pre_w, w_qkv, w_rest, w_gate)


def _outproj(h, att, pool, hm, mod, post_w, w_out, layer, seq):
    t, d = h.shape
    tm = ROW_TILE
    rows, mods, gains = _row_specs(d, seq)
    widths = (att.shape[-1], pool.shape[-1], hm.shape[-1])
    w_specs = []
    off = 0
    for wdt in widths:
        w_specs.append(_resident((None, wdt, d), functools.partial(lambda i, blk: (layer, blk, 0), blk=off // wdt)))
        off += wdt
    return pl.pallas_call(
        _outproj_kernel,
        out_shape=jax.ShapeDtypeStruct((t, d), F32),
        grid=(t // tm,),
        in_specs=[
            rows,
            pl.BlockSpec((tm, widths[0]), lambda i: (i, 0)),
            pl.BlockSpec((tm, widths[1]), lambda i: (i, 0)),
            pl.BlockSpec((tm, widths[2]), lambda i: (i, 0)),
            mods, gains,
            *w_specs,
        ],
        out_specs=rows,
        compiler_params=_params(("parallel",)),
        name="outproj",
    )(h, att.reshape(t, widths[0]), pool.reshape(t, widths[1]), hm.reshape(t, widths[2]), mod, post_w,
      w_out, w_out, w_out)


QKV_W = 3 * ATT_WIDTH


def _attn_scores(q, kwin, bias, head_of_lane):
    q32 = q.astype(F32) * (ATT_HEAD_DIM ** -0.5)
    lhs = jnp.concatenate(
        [jnp.where(head_of_lane == h, q32, 0.0) for h in range(ATT_HEADS)], axis=0).astype(BF16)
    return lax.dot_general(lhs, kwin, (((1,), (1,)), ((), ())), preferred_element_type=F32) + bias


def _by_lane_half(x, low):
    nq = x.shape[0] // ATT_HEADS
    out = []
    for half in range(2):
        first = x[(2 * half) * nq:(2 * half + 1) * nq]
        second = x[(2 * half + 1) * nq:(2 * half + 2) * nq]
        if x.shape[1] != 1:
            first = first[:, half * LANES:(half + 1) * LANES]
            second = second[:, half * LANES:(half + 1) * LANES]
        out.append(jnp.where(low, first, second))
    return out


def _attn_kernel(c1_ref, h1_ref, c4_ref, h4_ref, c16_ref, h16_ref, o_ref, bias_ref, dst_ref, s_ref, p_ref):
    t = pl.program_id(1)
    nq = ATT_BLOCK
    items = ATT_TILE // nq
    head_of_lane = lax.broadcasted_iota(jnp.int32, (nq, ATT_WIDTH), 1) >> (ATT_HEAD_DIM.bit_length() - 1)
    low = lax.broadcasted_iota(jnp.int32, (nq, LANES), 1) < ATT_HEAD_DIM

    qi = lax.broadcasted_iota(jnp.int32, (ATT_HEADS * nq, 2 * nq), 0) & (nq - 1)
    ki = lax.broadcasted_iota(jnp.int32, (ATT_HEADS * nq, 2 * nq), 1)
    ok = (ki >= qi) & (ki <= qi + ATT_SPAN)
    bias_ref[0] = jnp.where(ok, 0.0, NEG)
    bias_ref[1] = jnp.where(ok & (ki >= nq), 0.0, NEG)
    qs = slice(0, ATT_WIDTH)
    ks = slice(ATT_WIDTH, 2 * ATT_WIDTH)
    vs = slice(2 * ATT_WIDTH, 3 * ATT_WIDTH)

    branches = tuple(zip((c1_ref, c4_ref, c16_ref), (h1_ref, h4_ref, h16_ref), DILATIONS))

    def split(dil, idx):
        blocks = items // dil
        return idx >> (blocks.bit_length() - 1), idx & (blocks - 1)

    def window(cur_ref, halo_ref, dil, idx, cols):
        r, i = split(dil, idx)
        own = cur_ref[r, pl.ds(pl.multiple_of(i * nq, nq), nq), cols]
        if items == dil:
            prev = halo_ref[r, :, cols]
        else:
            prev0 = pl.multiple_of(jnp.maximum(i - 1, 0) * nq, nq)
            prev = jnp.where(i == 0, halo_ref[r, :, cols], cur_ref[r, pl.ds(prev0, nq), cols])
        return jnp.concatenate([prev, own], axis=0)

    def put(branch, quantity, halves, dil, idx):
        r, i = split(dil, idx)
        start = i * (nq * dil) + r
        rows = pl.ds(start, nq) if dil == 1 else pl.ds(start, nq, stride=dil)
        for half, piece in enumerate(halves):
            dst_ref[branch, quantity, half, rows, :] = piece

    for branch in range(len(branches)):
        s_ref[branch, 1] = jnp.zeros(s_ref.shape[2:], s_ref.dtype)
        p_ref[branch, 1] = jnp.zeros(p_ref.shape[2:], p_ref.dtype)

    def trip(j, slot):
        other = 1 - slot
        item_a = jnp.minimum(j, items - 1)
        item_b = jnp.clip(j - 1, 0, items - 1)
        item_c = jnp.clip(j - 2, 0, items - 1)

        for branch, (cur_ref, halo_ref, dil) in enumerate(branches):
            pv = jnp.dot(p_ref[branch, other], window(cur_ref, halo_ref, dil, item_c, vs),
                         preferred_element_type=F32)
            put(branch, 0, _by_lane_half(pv, low), dil, item_c)

        for branch, (cur_ref, halo_ref, dil) in enumerate(branches):
            s = s_ref[branch, other]
            m = jnp.max(s, axis=-1, keepdims=True)
            p = jnp.exp(s - m)
            l = jnp.sum(p, axis=-1, keepdims=True)
            p_ref[branch, slot] = p.astype(BF16)
            put(branch, 1, _by_lane_half(m, low), dil, item_b)
            put(branch, 2, _by_lane_half(l, low), dil, item_b)

        for branch, (cur_ref, halo_ref, dil) in enumerate(branches):
            r, i = split(dil, item_a)
            no_past = jnp.where((i == 0) & (t == 0), 1, 0)
            q = cur_ref[r, pl.ds(pl.multiple_of(i * nq, nq), nq), qs]
            s_ref[branch, slot] = _attn_scores(q, window(cur_ref, halo_ref, dil, item_a, ks),
                                               bias_ref[no_past], head_of_lane)

    def body(pair, carry):
        trip(2 * pair, 0)
        trip(2 * pair + 1, 1)
        return carry
    lax.fori_loop(0, (items + 2) // 2, body, 0)

    rows = ATT_BLOCK

    def merge(c, carry):
        sl = pl.ds(pl.multiple_of(c * rows, rows), rows)
        for half in range(2):
            maxes = [dst_ref[branch, 1, half, sl, :] for branch in range(len(branches))]
            top = functools.reduce(jnp.maximum, maxes)
            num = 0.0
            den = 0.0
            for branch, m_b in enumerate(maxes):
                e = jnp.exp(m_b - top)
                num = num + e * dst_ref[branch, 0, half, sl, :]
                den = den + e * dst_ref[branch, 2, half, sl, :]
            o_ref[sl, half * LANES:(half + 1) * LANES] = (num / den).astype(o_ref.dtype)
        return carry
    lax.fori_loop(0, ATT_TILE // rows, merge, 0)


def _attention(qkv_by_dilation, batch, seq):
    tiles = seq // ATT_TILE
    in_specs = []
    operands = []
    for dil, view in zip(DILATIONS, qkv_by_dilation):
        cur_rows = ATT_TILE // dil
        per_tile = cur_rows // ATT_BLOCK
        in_specs.append(pl.BlockSpec((None, dil, cur_rows, QKV_W), lambda b, t: (b, 0, t, 0)))
        in_specs.append(pl.BlockSpec(
            (None, dil, ATT_BLOCK, QKV_W),
            functools.partial(lambda b, t, per_tile: (b, 0, jnp.maximum(t * per_tile - 1, 0), 0), per_tile=per_tile)))
        operands += [view, view]
    stacked = (ATT_HEADS * ATT_BLOCK, 2 * ATT_BLOCK)
    return pl.pallas_call(
        _attn_kernel,
        out_shape=jax.ShapeDtypeStruct((batch, seq, ATT_WIDTH), BF16),
        grid=(batch, tiles),
        in_specs=in_specs,
        out_specs=pl.BlockSpec((None, ATT_TILE, ATT_WIDTH), lambda b, t: (b, t, 0)),
        scratch_shapes=[
            pltpu.VMEM((2,) + stacked, F32),
            pltpu.VMEM((len(DILATIONS), 3, 2, ATT_TILE, LANES), F32),
            pltpu.VMEM((len(DILATIONS), 2) + stacked, F32),
            pltpu.VMEM((len(DILATIONS), 2) + stacked, BF16),
        ],
        compiler_params=_params(("parallel", "parallel")),
        name="dilated_attention",
    )(*operands)


def _split3(x):
    hi = x.astype(BF16)
    r1 = x - hi.astype(F32)
    mid = r1.astype(BF16)
    lo = (r1 - mid.astype(F32)).astype(BF16)
    return hi, mid, lo


def _log_sigmoid(x):
    return -(jnp.maximum(-x, 0.0) + jnp.log1p(jnp.exp(-jnp.abs(x))))


def _mix_kernel(rest_ref, gate_ref, poolw_ref, pscale_ref, convw_ref, convb_ref, wqk_ref, wv_ref,
                gbias_ref, normw_ref, skip_ref, pool_ref, hm_ref,
                phist_ref, chist_ref, cstate_ref, mstate_ref):
    s_idx = pl.program_id(1)
    n = SEQ_CHUNK
    dh = MLSTM_HEAD_DIM

    @pl.when(s_idx == 0)
    def _():
        phist_ref[...] = jnp.zeros_like(phist_ref)
        chist_ref[...] = jnp.zeros_like(chist_ref)
        cstate_ref[...] = jnp.zeros_like(cstate_ref)
        mstate_ref[...] = jnp.zeros_like(mstate_ref)

    p_in = rest_ref[:, 0:POOL_WIDTH].astype(F32)
    ext = jnp.concatenate([phist_ref[...], p_in], axis=0)
    phist_ref[...] = p_in[n - POOL_HIST:, :]
    sums = {1: ext}
    w = 1
    while w < POOL_WINDOWS[-1]:
        sums[2 * w] = sums[w] + pltpu.roll(sums[w], w, 0)
        w *= 2
    lane = lax.broadcasted_iota(jnp.int32, (n, POOL_WIDTH), 1)
    pos = lax.broadcasted_iota(jnp.int32, (n, POOL_WIDTH), 0) + s_idx * n
    win_sum = jnp.zeros((n, POOL_WIDTH), F32)
    win_len = jnp.zeros((n, POOL_WIDTH), jnp.int32)
    for g, wlen in enumerate(POOL_WINDOWS):
        sel = (lane >> (POOL_GROUP_DIM.bit_length() - 1)) == g
        win_sum = jnp.where(sel, sums[wlen][POOL_HIST:, :], win_sum)
        win_len = jnp.where(sel, wlen, win_len)
    count = jnp.minimum(pos + 1, win_len).astype(F32)
    dlt = win_sum / count - p_in
    y = jnp.dot(dlt.astype(BF16), poolw_ref[...], preferred_element_type=F32)
    pool_ref[...] = (y * pscale_ref[...]).astype(pool_ref.dtype)

    xm = rest_ref[:, POOL_WIDTH:POOL_WIDTH + MLSTM_WIDTH].astype(F32)
    og = rest_ref[:, POOL_WIDTH + MLSTM_WIDTH:POOL_WIDTH + 2 * MLSTM_WIDTH].astype(F32)
    cext = jnp.concatenate([chist_ref[...], xm], axis=0)
    chist_ref[...] = xm[n - CONV_HIST:, :]
    conv = cext * convw_ref[MLSTM_CONV - 1:MLSTM_CONV, :]
    for back in range(1, MLSTM_CONV):
        tap = MLSTM_CONV - 1 - back
        conv = conv + pltpu.roll(cext, back, 0) * convw_ref[tap:tap + 1, :]
    conv = conv[CONV_HIST:, :] + convb_ref[...]
    xc = conv * jax.nn.sigmoid(conv)
    xc_b = xc.astype(BF16)
    xm_b = xm.astype(BF16)

    gi = gate_ref[...] + gbias_ref[...]
    f_pre = pltpu.roll(gi, GATE_LANES - MLSTM_HEADS, 1)
    logf = _log_sigmoid(f_pre)
    row = lax.broadcasted_iota(jnp.int32, (n, n), 0)
    col = lax.broadcasted_iota(jnp.int32, (n, n), 1)
    causal = col <= row
    tri = jnp.where(causal, 1.0, 0.0).astype(BF16)
    bcum = sum(jnp.dot(tri, part, preferred_element_type=F32) for part in _split3(logf))
    x_ib = gi - bcum
    b_last = bcum[n - 1:n, :]
    m_prev = mstate_ref[0:1, :]
    a = b_last + x_ib
    a_max = jnp.max(a, axis=0, keepdims=True)
    m_new = jnp.maximum(b_last + m_prev, a_max)
    decay = jnp.exp(b_last + m_prev - m_new)
    w_state = jnp.exp(a - m_new)
    inter_log = bcum + m_prev
    x_row = x_ib.T
    mstate_ref[0:1, :] = m_new

    ones = jnp.ones((n, dh), F32)
    for h in range(MLSTM_HEADS):
        hs = slice(h * dh, (h + 1) * dh)
        qk = jnp.dot(xc_b[:, hs], wqk_ref[h], preferred_element_type=F32)
        q = (qk[:, :dh] * (dh ** -0.5)).astype(BF16)
        k = qk[:, dh:].astype(BF16)
        v = jnp.dot(xm_b[:, hs], wv_ref[h], preferred_element_type=F32)
        v_aug = jnp.concatenate([v, ones], axis=1)

        log_d = jnp.where(causal, bcum[:, h:h + 1] + x_row[h:h + 1, :], NEG)
        m_t = jnp.maximum(inter_log[:, h:h + 1], jnp.max(log_d, axis=-1, keepdims=True))
        inter_w = jnp.exp(inter_log[:, h:h + 1] - m_t)
        s = lax.dot_general(q, k, (((1,), (1,)), ((), ())), preferred_element_type=F32)
        sw = (s * jnp.exp(log_d - m_t)).astype(BF16)
        c_prev = cstate_ref[h]
        out = inter_w * jnp.dot(q, c_prev.astype(BF16), preferred_element_type=F32) \
            + jnp.dot(sw, v_aug.astype(BF16), preferred_element_type=F32)
        hh = out[:, :dh] / jnp.maximum(jnp.abs(out[:, dh:]), jnp.exp(-m_t))

        wv_state = (w_state[:, h:h + 1] * v_aug).astype(BF16)
        d_c = lax.dot_general(k, wv_state, (((0,), (0,)), ((), ())), preferred_element_type=F32)
        cstate_ref[h] = decay[:, h:h + 1] * c_prev + d_c

        hh = hh * jax.nn.sigmoid(og[:, hs])
        hh = _rms(hh) * normw_ref[:, hs]
        hm_ref[:, hs] = (hh + skip_ref[:, hs] * xc[:, hs]).astype(hm_ref.dtype)


def _mix(rest, gates, pool_bd, pool_scale, conv_w, conv_b, w_qk, w_v, gate_bias, norm_w, skip, batch, seq):
    n = SEQ_CHUNK
    n_rest = rest.shape[-1]
    rest = rest.reshape(batch, seq, n_rest)
    gates = gates.reshape(batch, seq, GATE_LANES)
    c2 = lambda b, s: (0, 0)
    c3 = lambda b, s: (0, 0, 0)
    return pl.pallas_call(
        _mix_kernel,
        out_shape=(
            jax.ShapeDtypeStruct((batch, seq, POOL_WIDTH), BF16),
            jax.ShapeDtypeStruct((batch, seq, MLSTM_WIDTH), BF16),
        ),
        grid=(batch, seq // n),
        in_specs=[
            pl.BlockSpec((None, n, n_rest), lambda b, s: (b, s, 0)),
            pl.BlockSpec((None, n, GATE_LANES), lambda b, s: (b, s, 0)),
            pl.BlockSpec(pool_bd.shape, c2),
            pl.BlockSpec(pool_scale.shape, c2),
            pl.BlockSpec(conv_w.shape, c2),
            pl.BlockSpec(conv_b.shape, c2),
            pl.BlockSpec(w_qk.shape, c3),
            pl.BlockSpec(w_v.shape, c3),
            pl.BlockSpec(gate_bias.shape, c2),
            pl.BlockSpec(norm_w.shape, c2),
            pl.BlockSpec(skip.shape, c2),
        ],
        out_specs=(
            pl.BlockSpec((None, n, POOL_WIDTH), lambda b, s: (b, s, 0)),
            pl.BlockSpec((None, n, MLSTM_WIDTH), lambda b, s: (b, s, 0)),
        ),
        scratch_shapes=[
            pltpu.VMEM((POOL_HIST, POOL_WIDTH), F32),
            pltpu.VMEM((CONV_HIST, MLSTM_WIDTH), F32),
            pltpu.VMEM((MLSTM_HEADS, MLSTM_HEAD_DIM, 2 * MLSTM_HEAD_DIM), F32),
            pltpu.VMEM((SUBLANES, GATE_LANES), F32),
        ],
        compiler_params=_params(("parallel", "arbitrary")),
        name="pool_mlstm",
    )(rest, gates, pool_bd, pool_scale, conv_w, conv_b, w_qk, w_v, gate_bias, norm_w, skip)


def kernel(x, c, ada_w, ada_b, pre_norm_w, post_norm_w, ffn_up, ffn_down, mix_in_w, mix_out_w, pool_w, pool_scale,
           mlstm_conv_w, mlstm_conv_b, mlstm_qkv_w, mlstm_gate_b, mlstm_norm_w, mlstm_skip):
    batch, seq, d = x.shape
    depth = ada_w.shape[0]
    assert d == ATT_WIDTH + POOL_WIDTH + MLSTM_WIDTH
    assert seq % ATT_TILE == 0 and seq % ROW_TILE == 0 and seq % SEQ_CHUNK == 0

    up_b = ffn_up.astype(BF16)
    down_b = ffn_down.astype(BF16)
    out_b = mix_out_w.astype(BF16)
    n_main = QKV_W + POOL_WIDTH + 2 * MLSTM_WIDTH
    w_qkv = mix_in_w[:, :, :QKV_W].astype(BF16)
    w_rest = mix_in_w[:, :, QKV_W:n_main].astype(BF16)
    w_gate = jnp.pad(mix_in_w[:, :, n_main:], ((0, 0), (0, 0), (0, GATE_LANES - 2 * MLSTM_HEADS))).astype(BF16)
    gate_bias = jnp.pad(mlstm_gate_b.reshape(depth, 1, 2 * MLSTM_HEADS),
                        ((0, 0), (0, 0), (0, GATE_LANES - 2 * MLSTM_HEADS)))
    groups = len(POOL_WINDOWS)
    eye = jnp.eye(groups, dtype=pool_w.dtype)
    pool_bd = (pool_w[:, :, :, None, :] * eye[None, :, None, :, None]).reshape(depth, POOL_WIDTH, POOL_WIDTH).astype(BF16)
    w_qk = jnp.concatenate([mlstm_qkv_w[:, 0], mlstm_qkv_w[:, 1]], axis=-1).astype(BF16)
    w_v = mlstm_qkv_w[:, 2].astype(BF16)

    mod = _adaln(c, ada_w, ada_b).reshape(depth, batch, 9, d)

    h = x.reshape(batch * seq, d)
    for l in range(depth):
        h = _ffn(h, mod[l], pre_norm_w[l], post_norm_w[l], up_b, down_b, l, 0, seq)
        *qkv, rest, gates = _inproj(h, mod[l], pre_norm_w[l], w_qkv, w_rest, w_gate, l, batch, seq)
        att = _attention(qkv, batch, seq)
        pool, hm = _mix(rest, gates, pool_bd[l], pool_scale[l].reshape(1, -1), mlstm_conv_w[l],
                        mlstm_conv_b[l].reshape(1, -1), w_qk[l], w_v[l], gate_bias[l],
                        mlstm_norm_w[l].reshape(1, -1), mlstm_skip[l].reshape(1, -1), batch, seq)
        h = _outproj(h, att, pool, hm, mod[l], post_norm_w[l], out_b, l, seq)
        h = _ffn(h, mod[l], pre_norm_w[l], post_norm_w[l], up_b, down_b, l, 1, seq)
    return h.reshape(batch, seq, d)
```

```python
import functools

import jax
import jax.numpy as jnp
from jax import lax
from jax.experimental import pallas as pl
from jax.experimental.pallas import tpu as pltpu

F32 = jnp.float32
BF16 = jnp.bfloat16

ATT_HEADS = 4
ATT_HEAD_DIM = 64
ATT_WIDTH = ATT_HEADS * ATT_HEAD_DIM
DILATIONS = (1, 4, 16)
ATT_SPAN = 128
POOL_WINDOWS = (2, 4, 8, 16)
POOL_GROUP_DIM = 64
POOL_WIDTH = len(POOL_WINDOWS) * POOL_GROUP_DIM
MLSTM_HEADS = 4
MLSTM_HEAD_DIM = 128
MLSTM_WIDTH = MLSTM_HEADS * MLSTM_HEAD_DIM
MLSTM_CONV = 4
MACARON_WEIGHT = 0.5
EPS = 1e-6
NEG = -1e30

LANES = 128
SUBLANES = 8
VMEM_LIMIT_BYTES = 56 * 1024 * 1024

ROW_TILE = 1024
ROW_PARTS = 2
ATT_TILE = 2048
ATT_BLOCK = 128
SEQ_CHUNK = 256
SEQ_GROUP = 2
HIST = 16
GATE_LANES = LANES


def _params(sem, **flags):
    return pltpu.CompilerParams(dimension_semantics=sem, vmem_limit_bytes=VMEM_LIMIT_BYTES, flags=flags or None)


def _rms(x):
    return x * lax.rsqrt(jnp.mean(x * x, axis=-1, keepdims=True) + EPS)


def _adaln_kernel(c_ref, w_ref, b_ref, o_ref):
    c = c_ref[...]
    c_act = (c * jax.nn.sigmoid(c)).astype(BF16)
    o_ref[...] = jnp.dot(c_act, w_ref[...].astype(BF16), preferred_element_type=F32) + b_ref[...]


def _adaln(c, ada_w, ada_b):
    depth, d, n = ada_w.shape
    b = c.shape[0]
    tn = n // 8
    return pl.pallas_call(
        _adaln_kernel,
        out_shape=jax.ShapeDtypeStruct((depth, b, n), F32),
        grid=(depth, n // tn),
        in_specs=[
            pl.BlockSpec((b, d), lambda l, j: (0, 0)),
            pl.BlockSpec((None, d, tn), lambda l, j: (l, 0, j)),
            pl.BlockSpec((None, 1, tn), lambda l, j: (l, 0, j)),
        ],
        out_specs=pl.BlockSpec((None, b, tn), lambda l, j: (l, 0, j)),
        compiler_params=_params(("parallel", "parallel")),
        name="adaln",
    )(c, ada_w, ada_b.reshape(depth, 1, n))


def _modulated(x, mod_ref, prew_ref, sub):
    scale = prew_ref[sub:sub + 1, :] * (1.0 + mod_ref[3 * sub + 1:3 * sub + 2, :])
    return (_rms(x) * scale + mod_ref[3 * sub:3 * sub + 1, :]).astype(BF16)


def _part(ref, part):
    rows = ref.shape[0] // ROW_PARTS
    return slice(part * rows, (part + 1) * rows)


def _ffn_kernel(x_ref, mod_ref, prew_ref, postw_ref, wup_ref, wdown_ref, o_ref, *, sub):
    gate = postw_ref[sub:sub + 1, :] * (MACARON_WEIGHT * mod_ref[3 * sub + 2:3 * sub + 3, :])
    u = _modulated(x_ref[_part(x_ref, 0), :], mod_ref, prew_ref, sub)
    for part in range(ROW_PARTS):
        rows = _part(x_ref, part)
        gv = jnp.dot(u, wup_ref[...], preferred_element_type=F32)
        if part + 1 < ROW_PARTS:
            u = _modulated(x_ref[_part(x_ref, part + 1), :], mod_ref, prew_ref, sub)
        ff = gv.shape[1] // 2
        g, v = gv[:, :ff], gv[:, ff:]
        a = (g * jax.nn.sigmoid(g) * v).astype(BF16)
        y = jnp.dot(a, wdown_ref[...], preferred_element_type=F32)
        o_ref[rows, :] = x_ref[rows, :] + _rms(y) * gate


def _inproj_kernel(x_ref, mod_ref, prew_ref, wqkv_ref, wrest_ref, wgate_ref,
                   qkv1_ref, qkv4_ref, qkv16_ref, rest_ref, gate_ref, z_ref):
    slabs = z_ref.shape[0]

    def by_residue(part):
        rows = _part(x_ref, part)
        n = rows.stop - rows.start
        for dil, out_ref in ((DILATIONS[1], qkv4_ref), (DILATIONS[2], qkv16_ref)):
            per = n // dil
            for r in range(dil):
                for c in range(slabs):
                    out_ref[r, part * per:(part + 1) * per, c * LANES:(c + 1) * LANES] = (
                        z_ref[c, pl.ds(rows.start + r, per, stride=dil), :].astype(BF16))

    u = _modulated(x_ref[_part(x_ref, 0), :], mod_ref, prew_ref, 1)
    for part in range(ROW_PARTS):
        rows = _part(x_ref, part)
        z = jnp.dot(u, wqkv_ref[...], preferred_element_type=F32)
        qkv1_ref[0, rows, :] = z.astype(BF16)
        for c in range(slabs):
            z_ref[c, rows, :] = z[:, c * LANES:(c + 1) * LANES]
        rest_ref[rows, :] = jnp.dot(u, wrest_ref[...], preferred_element_type=F32).astype(BF16)
        gate_ref[rows, :] = jnp.dot(u, wgate_ref[...], preferred_element_type=F32)
        if part + 1 < ROW_PARTS:
            u = _modulated(x_ref[_part(x_ref, part + 1), :], mod_ref, prew_ref, 1)
        by_residue(part)


def _outproj_kernel(x_ref, att_ref, pool_ref, hm_ref, mod_ref, postw_ref, wa_ref, wp_ref, wh_ref, o_ref):
    gate = postw_ref[1:2, :] * mod_ref[5:6, :]
    ys = []
    for part in range(ROW_PARTS):
        rows = _part(x_ref, part)
        y = jnp.dot(att_ref[rows, :], wa_ref[...], preferred_element_type=F32)
        y = y + jnp.dot(pool_ref[rows, :], wp_ref[...], preferred_element_type=F32)
        ys.append(y + jnp.dot(hm_ref[rows, :], wh_ref[...], preferred_element_type=F32))
    for part in range(ROW_PARTS):
        rows = _part(x_ref, part)
        o_ref[rows, :] = x_ref[rows, :] + _rms(ys[part]) * gate


def _resident(shape, index):
    return pl.BlockSpec(shape, index, pipeline_mode=pl.Buffered(1))


def _row_specs(d, seq):
    tm = ROW_TILE
    per_batch = seq // tm
    return (pl.BlockSpec((tm, d), lambda i: (i, 0)),
            pl.BlockSpec((None, 9, d), lambda i: (i // per_batch, 0, 0)),
            pl.BlockSpec((3, d), lambda i: (0, 0)))


def _ffn(h, mod, pre_w, post_w, w_up, w_down, layer, slot, seq):
    t, d = h.shape
    ff = w_down.shape[2]
    rows, mods, gains = _row_specs(d, seq)
    return pl.pallas_call(
        functools.partial(_ffn_kernel, sub=2 * slot),
        out_shape=jax.ShapeDtypeStruct((t, d), F32),
        grid=(t // ROW_TILE,),
        in_specs=[
            rows, mods, gains, gains,
            _resident((None, None, d, 2 * ff), lambda i: (layer, slot, 0, 0)),
            _resident((None, None, ff, d), lambda i: (layer, slot, 0, 0)),
        ],
        out_specs=rows,
        compiler_params=_params(("parallel",)),
        name="ffn",
    )(h, mod, pre_w, post_w, w_up, w_down)


def _inproj(h, mod, pre_w, w_qkv, w_rest, w_gate, layer, batch, seq):
    t, d = h.shape
    tm = ROW_TILE
    per_batch = seq // tm
    n_qkv, n_rest, n_gate = w_qkv.shape[2], w_rest.shape[2], w_gate.shape[2]
    rows, mods, gains = _row_specs(d, seq)
    by_residue = lambda i: (i // per_batch, 0, i % per_batch, 0)
    qkv_shapes = [jax.ShapeDtypeStruct((batch, dil, seq // dil, n_qkv), BF16) for dil in DILATIONS]
    qkv_specs = [pl.BlockSpec((None, dil, tm // dil, n_qkv), by_residue) for dil in DILATIONS]
    return pl.pallas_call(
        _inproj_kernel,
        out_shape=(
            *qkv_shapes,
            jax.ShapeDtypeStruct((t, n_rest), BF16),
            jax.ShapeDtypeStruct((t, n_gate), F32),
        ),
        grid=(t // tm,),
        in_specs=[
            rows, mods, gains,
            _resident((None, d, n_qkv), lambda i: (layer, 0, 0)),
            _resident((None, d, n_rest), lambda i: (layer, 0, 0)),
            _resident((None, d, n_gate), lambda i: (layer, 0, 0)),
        ],
        out_specs=(
            *qkv_specs,
            pl.BlockSpec((tm, n_rest), lambda i: (i, 0)),
            pl.BlockSpec((tm, n_gate), lambda i: (i, 0)),
        ),
        scratch_shapes=[pltpu.VMEM((n_qkv // LANES, tm, LANES), F32)],
        compiler_params=_params(("parallel",)),
        name="inproj",
    )(h, mod, pre_w, w_qkv, w_rest, w_gate)


def _outproj(h, att, pool, hm, mod, post_w, w_out, layer, seq):
    t, d = h.shape
    tm = ROW_TILE
    rows, mods, gains = _row_specs(d, seq)
    widths = (att.shape[-1], pool.shape[-1], hm.shape[-1])
    w_specs = []
    off = 0
    for wdt in widths:
        w_specs.append(_resident((None, wdt, d), functools.partial(lambda i, blk: (layer, blk, 0), blk=off // wdt)))
        off += wdt
    return pl.pallas_call(
        _outproj_kernel,
        out_shape=jax.ShapeDtypeStruct((t, d), F32),
        grid=(t // tm,),
        in_specs=[
            rows,
            pl.BlockSpec((tm, widths[0]), lambda i: (i, 0)),
            pl.BlockSpec((tm, widths[1]), lambda i: (i, 0)),
            pl.BlockSpec((tm, widths[2]), lambda i: (i, 0)),
            mods, gains,
            *w_specs,
        ],
        out_specs=rows,
        compiler_params=_params(("parallel",)),
        name="outproj",
    )(h, att.reshape(t, widths[0]), pool.reshape(t, widths[1]), hm.reshape(t, widths[2]), mod, post_w,
      w_out, w_out, w_out)


QKV_W = 3 * ATT_WIDTH


def _attn_scores(q, kwin, bias, head_of_lane):
    q32 = q.astype(F32) * (ATT_HEAD_DIM ** -0.5)
    lhs = jnp.concatenate(
        [jnp.where(head_of_lane == h, q32, 0.0) for h in range(ATT_HEADS)], axis=0).astype(BF16)
    return lax.dot_general(lhs, kwin, (((1,), (1,)), ((), ())), preferred_element_type=F32) + bias


def _by_lane_half(x, low):
    nq = x.shape[0] // ATT_HEADS
    out = []
    for half in range(2):
        first = x[(2 * half) * nq:(2 * half + 1) * nq]
        second = x[(2 * half + 1) * nq:(2 * half + 2) * nq]
        if x.shape[1] != 1:
            first = first[:, half * LANES:(half + 1) * LANES]
            second = second[:, half * LANES:(half + 1) * LANES]
        out.append(jnp.where(low, first, second))
    return out


def _attn_kernel(c1_ref, h1_ref, c4_ref, h4_ref, c16_ref, h16_ref, o_ref, bias_ref, dst_ref, s_ref, p_ref):
    t = pl.program_id(1)
    nq = ATT_BLOCK
    items = ATT_TILE // nq
    head_of_lane = lax.broadcasted_iota(jnp.int32, (nq, ATT_WIDTH), 1) >> (ATT_HEAD_DIM.bit_length() - 1)
    low = lax.broadcasted_iota(jnp.int32, (nq, LANES), 1) < ATT_HEAD_DIM

    qi = lax.broadcasted_iota(jnp.int32, (ATT_HEADS * nq, 2 * nq), 0) & (nq - 1)
    ki = lax.broadcasted_iota(jnp.int32, (ATT_HEADS * nq, 2 * nq), 1)
    ok = (ki >= qi) & (ki <= qi + ATT_SPAN)
    bias_ref[0] = jnp.where(ok, 0.0, NEG)
    bias_ref[1] = jnp.where(ok & (ki >= nq), 0.0, NEG)
    qs = slice(0, ATT_WIDTH)
    ks = slice(ATT_WIDTH, 2 * ATT_WIDTH)
    vs = slice(2 * ATT_WIDTH, 3 * ATT_WIDTH)

    branches = tuple(zip((c1_ref, c4_ref, c16_ref), (h1_ref, h4_ref, h16_ref), DILATIONS))

    def split(dil, idx):
        blocks = items // dil
        return idx >> (blocks.bit_length() - 1), idx & (blocks - 1)

    def window(cur_ref, halo_ref, dil, idx, cols):
        r, i = split(dil, idx)
        own = cur_ref[r, pl.ds(pl.multiple_of(i * nq, nq), nq), cols]
        if items == dil:
            prev = halo_ref[r, :, cols]
        else:
            prev0 = pl.multiple_of(jnp.maximum(i - 1, 0) * nq, nq)
            prev = jnp.where(i == 0, halo_ref[r, :, cols], cur_ref[r, pl.ds(prev0, nq), cols])
        return jnp.concatenate([prev, own], axis=0)

    def put(branch, quantity, halves, dil, idx):
        r, i = split(dil, idx)
        start = i * (nq * dil) + r
        rows = pl.ds(start, nq) if dil == 1 else pl.ds(start, nq, stride=dil)
        for half, piece in enumerate(halves):
            dst_ref[branch, quantity, half, rows, :] = piece

    for branch in range(len(branches)):
        s_ref[branch, 1] = jnp.zeros(s_ref.shape[2:], s_ref.dtype)
        p_ref[branch, 1] = jnp.zeros(p_ref.shape[2:], p_ref.dtype)

    def trip(j, slot):
        other = 1 - slot
        item_a = jnp.minimum(j, items - 1)
        item_b = jnp.clip(j - 1, 0, items - 1)
        item_c = jnp.clip(j - 2, 0, items - 1)

        for branch, (cur_ref, halo_ref, dil) in enumerate(branches):
            pv = jnp.dot(p_ref[branch, other], window(cur_ref, halo_ref, dil, item_c, vs),
                         preferred_element_type=F32)
            put(branch, 0, _by_lane_half(pv, low), dil, item_c)

        for branch, (cur_ref, halo_ref, dil) in enumerate(branches):
            s = s_ref[branch, other]
            m = jnp.max(s, axis=-1, keepdims=True)
            p = jnp.exp(s - m)
            l = jnp.sum(p, axis=-1, keepdims=True)
            p_ref[branch, slot] = p.astype(BF16)
            put(branch, 1, _by_lane_half(m, low), dil, item_b)
            put(branch, 2, _by_lane_half(l, low), dil, item_b)

        for branch, (cur_ref, halo_ref, dil) in enumerate(branches):
            r, i = split(dil, item_a)
            no_past = jnp.where((i == 0) & (t == 0), 1, 0)
            q = cur_ref[r, pl.ds(pl.multiple_of(i * nq, nq), nq), qs]
            s_ref[branch, slot] = _attn_scores(q, window(cur_ref, halo_ref, dil, item_a, ks),
                                               bias_ref[no_past], head_of_lane)

    def body(pair, carry):
        trip(2 * pair, 0)
        trip(2 * pair + 1, 1)
        return carry
    lax.fori_loop(0, (items + 2) // 2, body, 0)

    rows = ATT_BLOCK

    def merge(c, carry):
        sl = pl.ds(pl.multiple_of(c * rows, rows), rows)
        for half in range(2):
            maxes = [dst_ref[branch, 1, half, sl, :] for branch in range(len(branches))]
            top = functools.reduce(jnp.maximum, maxes)
            num = 0.0
            den = 0.0
            for branch, m_b in enumerate(maxes):
                e = jnp.exp(m_b - top)
                num = num + e * dst_ref[branch, 0, half, sl, :]
                den = den + e * dst_ref[branch, 2, half, sl, :]
            o_ref[sl, half * LANES:(half + 1) * LANES] = (num / den).astype(o_ref.dtype)
        return carry
    lax.fori_loop(0, ATT_TILE // rows, merge, 0)


def _attention(qkv_by_dilation, batch, seq):
    tiles = seq // ATT_TILE
    in_specs = []
    operands = []
    for dil, view in zip(DILATIONS, qkv_by_dilation):
        cur_rows = ATT_TILE // dil
        per_tile = cur_rows // ATT_BLOCK
        in_specs.append(pl.BlockSpec((None, dil, cur_rows, QKV_W), lambda b, t: (b, 0, t, 0)))
        in_specs.append(pl.BlockSpec(
            (None, dil, ATT_BLOCK, QKV_W),
            functools.partial(lambda b, t, per_tile: (b, 0, jnp.maximum(t * per_tile - 1, 0), 0), per_tile=per_tile)))
        operands += [view, view]
    stacked = (ATT_HEADS * ATT_BLOCK, 2 * ATT_BLOCK)
    return pl.pallas_call(
        _attn_kernel,
        out_shape=jax.ShapeDtypeStruct((batch, seq, ATT_WIDTH), BF16),
        grid=(batch, tiles),
        in_specs=in_specs,
        out_specs=pl.BlockSpec((None, ATT_TILE, ATT_WIDTH), lambda b, t: (b, t, 0)),
        scratch_shapes=[
            pltpu.VMEM((2,) + stacked, F32),
            pltpu.VMEM((len(DILATIONS), 3, 2, ATT_TILE, LANES), F32),
            pltpu.VMEM((len(DILATIONS), 2) + stacked, F32),
            pltpu.VMEM((len(DILATIONS), 2) + stacked, BF16),
        ],
        compiler_params=_params(("parallel", "parallel")),
        name="dilated_attention",
    )(*operands)


def _split3(x):
    hi = x.astype(BF16)
    r1 = x - hi.astype(F32)
    mid = r1.astype(BF16)
    lo = (r1 - mid.astype(F32)).astype(BF16)
    return hi, mid, lo


def _log_sigmoid(x):
    return -(jnp.maximum(-x, 0.0) + jnp.log1p(jnp.exp(-jnp.abs(x))))


def _mix_kernel(rest_ref, gate_ref, poolw_ref, pscale_ref, convw_ref, convb_ref, wqk_ref, wv_ref,
                gbias_ref, normw_ref, skip_ref, pool_ref, hm_ref,
                phist_ref, chist_ref, cstate_ref, mstate_ref, band_ref, shift_ref, upper_ref):
    s_idx = pl.program_id(1)
    n = SEQ_CHUNK
    dh = MLSTM_HEAD_DIM
    heads = MLSTM_HEADS
    group = range(SEQ_GROUP)

    @pl.when(s_idx == 0)
    def _():
        phist_ref[...] = jnp.zeros_like(phist_ref)
        chist_ref[...] = jnp.zeros_like(chist_ref)
        cstate_ref[...] = jnp.zeros_like(cstate_ref)
        mstate_ref[...] = jnp.zeros_like(mstate_ref)
        t_row = lax.broadcasted_iota(jnp.int32, (n, HIST + n), 0) + HIST
        col = lax.broadcasted_iota(jnp.int32, (n, HIST + n), 1)
        for g, wlen in enumerate(POOL_WINDOWS):
            band_ref[g * n:(g + 1) * n, :] = jnp.where((col <= t_row) & (col > t_row - wlen), 1.0, 0.0).astype(BF16)
        for back in range(1, MLSTM_CONV):
            shift_ref[(back - 1) * n:back * n, :] = jnp.where(col == t_row - back, 1.0, 0.0).astype(BF16)
        src = lax.broadcasted_iota(jnp.int32, (n, n), 0)
        dst = lax.broadcasted_iota(jnp.int32, (n, n), 1)
        upper_ref[...] = jnp.where(src <= dst, 1.0, 0.0).astype(BF16)

    lane = lax.broadcasted_iota(jnp.int32, (n, POOL_WIDTH), 1)
    pool_group = lane >> (POOL_GROUP_DIM.bit_length() - 1)
    pos = lax.broadcasted_iota(jnp.int32, (n, POOL_WIDTH), 0) + s_idx * n
    win_len = jnp.full((n, POOL_WIDTH), POOL_WINDOWS[0], jnp.int32)
    for g, wlen in enumerate(POOL_WINDOWS):
        win_len = jnp.where(pool_group == g, wlen, win_len)
    count = jnp.minimum(pos + 1, win_len).astype(F32)
    row = lax.broadcasted_iota(jnp.int32, (n, n), 0)
    col = lax.broadcasted_iota(jnp.int32, (n, n), 1)
    causal = col <= row
    lane_g = lax.broadcasted_iota(jnp.int32, (n, GATE_LANES), 1)
    row8 = lax.broadcasted_iota(jnp.int32, (SUBLANES, n), 0)
    ones = jnp.ones((n, dh), F32)

    def window_delta(b):
        p_b = rest_ref[b, :, 0:POOL_WIDTH]
        ext = jnp.concatenate([phist_ref[b].astype(BF16), p_b], axis=0)
        phist_ref[b] = p_b[n - HIST:, :].astype(F32)
        sums = jnp.dot(band_ref[...], ext, preferred_element_type=F32)
        win_sum = sums[0:n]
        for g in range(1, len(POOL_WINDOWS)):
            win_sum = jnp.where(pool_group == g, sums[g * n:(g + 1) * n], win_sum)
        return (win_sum / count - p_b.astype(F32)).astype(BF16)

    def front(b):
        xm_b = rest_ref[b, :, POOL_WIDTH:POOL_WIDTH + MLSTM_WIDTH]
        cext = jnp.concatenate([chist_ref[b].astype(BF16), xm_b], axis=0)
        chist_ref[b] = xm_b[n - HIST:, :].astype(F32)
        shifted = jnp.dot(shift_ref[...], cext, preferred_element_type=F32)
        conv = xm_b.astype(F32) * convw_ref[MLSTM_CONV - 1:MLSTM_CONV, :] + convb_ref[...]
        for back in range(1, MLSTM_CONV):
            tap = MLSTM_CONV - 1 - back
            conv = conv + shifted[(back - 1) * n:back * n] * convw_ref[tap:tap + 1, :]
        xc = conv * jax.nn.sigmoid(conv)
        return xc, xc.astype(BF16), xm_b

    def gate_rows(b):
        gates_t = (gate_ref[b] + gbias_ref[...]).T
        i_rows = gates_t[0:SUBLANES, :]
        return i_rows, _log_sigmoid(pltpu.roll(i_rows, SUBLANES - heads, 0))

    def gate_stats(b, i_rows, bcum):
        x_ib = i_rows - bcum
        b_last = bcum[:, n - 1:n]
        m_prev = mstate_ref[b, :, 0:1]
        a = b_last + x_ib
        m_new = jnp.maximum(b_last + m_prev, jnp.max(a, axis=1, keepdims=True))
        decay = jnp.exp(b_last + m_prev - m_new)
        w_state = jnp.exp(a - m_new)
        mstate_ref[b] = jnp.broadcast_to(m_new, mstate_ref.shape[1:])
        packed = jnp.concatenate(
            [jnp.where(row8 < heads, bcum, pltpu.roll(w_state, heads, 0)), bcum + m_prev,
             jnp.zeros((LANES - 2 * SUBLANES, n), F32)], axis=0)
        return x_ib, packed.T, decay

    def decay_log(x_ib, cols, h):
        log_d = jnp.where(causal, cols[:, h:h + 1] + x_ib[h:h + 1, :], NEG)
        return log_d, jnp.max(log_d, axis=-1, keepdims=True)

    def stabilisers(cols, row_maxes):
        intra_max = jnp.zeros((n, GATE_LANES), F32)
        for h, row_max in enumerate(row_maxes):
            intra_max = jnp.where(lane_g == 2 * heads + h, row_max, intra_max)
        m_all = jnp.maximum(cols, intra_max)
        return m_all, jnp.exp(cols - m_all), jnp.exp(-m_all)

    def head(b, h, qk, v, xc, cols, decay, log_d, m_all, inter_w_all, floor_all):
        hs = slice(h * dh, (h + 1) * dh)
        lane_m = 2 * heads + h
        q = (qk[:, :dh] * (dh ** -0.5)).astype(BF16)
        k = qk[:, dh:].astype(BF16)
        v_aug = jnp.concatenate([v, ones], axis=1)

        s = lax.dot_general(q, k, (((1,), (1,)), ((), ())), preferred_element_type=F32)
        sw = (s * jnp.exp(log_d - m_all[:, lane_m:lane_m + 1])).astype(BF16)
        c_prev = cstate_ref[b, h]
        out = inter_w_all[:, lane_m:lane_m + 1] * jnp.dot(q, c_prev.astype(BF16), preferred_element_type=F32) \
            + jnp.dot(sw, v_aug.astype(BF16), preferred_element_type=F32)
        hh = out[:, :dh] / jnp.maximum(jnp.abs(out[:, dh:]), floor_all[:, lane_m:lane_m + 1])

        wv_state = (cols[:, heads + h:heads + h + 1] * v_aug).astype(BF16)
        d_c = lax.dot_general(k, wv_state, (((0,), (0,)), ((), ())), preferred_element_type=F32)
        cstate_ref[b, h] = decay[h:h + 1, :] * c_prev + d_c

        og = rest_ref[b, :, POOL_WIDTH + MLSTM_WIDTH + h * dh:POOL_WIDTH + MLSTM_WIDTH + (h + 1) * dh].astype(F32)
        hh = hh * jax.nn.sigmoid(og)
        hh = _rms(hh) * normw_ref[:, hs]
        hm_ref[b, :, hs] = (hh + skip_ref[:, hs] * xc[:, hs]).astype(hm_ref.dtype)

    def rows_of(stacked, b, rows):
        return stacked[b * rows:(b + 1) * rows]

    raw = [gate_rows(b) for b in group]
    parts = [part for _, logf in raw for part in _split3(logf)]
    cums = jnp.dot(jnp.concatenate(parts, axis=0), upper_ref[...], preferred_element_type=F32)
    gated = []
    for b in group:
        piece = rows_of(cums, b, 3 * SUBLANES)
        bcum = piece[0:SUBLANES] + piece[SUBLANES:2 * SUBLANES] + piece[2 * SUBLANES:3 * SUBLANES]
        gated.append(gate_stats(b, raw[b][0], bcum))

    matmul_work = [lambda b=b: window_delta(b) for b in group] + [lambda b=b: front(b) for b in group]
    matmul_done = []
    decay = {}
    for b in group:
        x_ib, cols, _ = gated[b]
        for h in range(heads):
            decay[b, h] = decay_log(x_ib, cols, h)
            if matmul_work and h % 2 == 0:
                matmul_done.append(matmul_work.pop(0)())
    matmul_done += [work() for work in matmul_work]
    deltas, fronts = matmul_done[:len(group)], matmul_done[len(group):]
    stats = [stabilisers(gated[b][1], [decay[b, h][1] for h in range(heads)]) for b in group]

    pooled = jnp.dot(jnp.concatenate(deltas, axis=0), poolw_ref[...], preferred_element_type=F32) * pscale_ref[...]
    for b in group:
        pool_ref[b] = rows_of(pooled, b, n).astype(pool_ref.dtype)

    for h in range(heads):
        hs = slice(h * dh, (h + 1) * dh)
        qk_all = jnp.dot(jnp.concatenate([fronts[b][1][:, hs] for b in group], axis=0), wqk_ref[h],
                         preferred_element_type=F32)
        v_all = jnp.dot(jnp.concatenate([fronts[b][2][:, hs] for b in group], axis=0), wv_ref[h],
                        preferred_element_type=F32)
        for b in group:
            _, cols, state_decay = gated[b]
            m_all, inter_w_all, floor_all = stats[b]
            head(b, h, rows_of(qk_all, b, n), rows_of(v_all, b, n), fronts[b][0], cols, state_decay,
                 decay[b, h][0], m_all, inter_w_all, floor_all)


def _mix(rest, gates, pool_bd, pool_scale, conv_w, conv_b, w_qk, w_v, gate_bias, norm_w, skip, batch, seq):
    n = SEQ_CHUNK
    sg = SEQ_GROUP
    n_rest = rest.shape[-1]
    rest = rest.reshape(batch, seq, n_rest)
    gates = gates.reshape(batch, seq, GATE_LANES)
    c2 = lambda b, s: (0, 0)
    c3 = lambda b, s: (0, 0, 0)
    chunk = lambda b, s: (b, s, 0)
    return pl.pallas_call(
        _mix_kernel,
        out_shape=(
            jax.ShapeDtypeStruct((batch, seq, POOL_WIDTH), BF16),
            jax.ShapeDtypeStruct((batch, seq, MLSTM_WIDTH), BF16),
        ),
        grid=(batch // sg, seq // n),
        in_specs=[
            pl.BlockSpec((sg, n, n_rest), chunk),
            pl.BlockSpec((sg, n, GATE_LANES), chunk),
            pl.BlockSpec(pool_bd.shape, c2),
            pl.BlockSpec(pool_scale.shape, c2),
            pl.BlockSpec(conv_w.shape, c2),
            pl.BlockSpec(conv_b.shape, c2),
            pl.BlockSpec(w_qk.shape, c3),
            pl.BlockSpec(w_v.shape, c3),
            pl.BlockSpec(gate_bias.shape, c2),
            pl.BlockSpec(norm_w.shape, c2),
            pl.BlockSpec(skip.shape, c2),
        ],
        out_specs=(
            pl.BlockSpec((sg, n, POOL_WIDTH), chunk),
            pl.BlockSpec((sg, n, MLSTM_WIDTH), chunk),
        ),
        scratch_shapes=[
            pltpu.VMEM((sg, HIST, POOL_WIDTH), F32),
            pltpu.VMEM((sg, HIST, MLSTM_WIDTH), F32),
            pltpu.VMEM((sg, MLSTM_HEADS, MLSTM_HEAD_DIM, 2 * MLSTM_HEAD_DIM), F32),
            pltpu.VMEM((sg, SUBLANES, GATE_LANES), F32),
            pltpu.VMEM((len(POOL_WINDOWS) * n, HIST + n), BF16),
            pltpu.VMEM(((MLSTM_CONV - 1) * n, HIST + n), BF16),
            pltpu.VMEM((n, n), BF16),
        ],
        compiler_params=_params(("parallel", "arbitrary")),
        name="pool_mlstm",
    )(rest, gates, pool_bd, pool_scale, conv_w, conv_b, w_qk, w_v, gate_bias, norm_w, skip)


def kernel(x, c, ada_w, ada_b, pre_norm_w, post_norm_w, ffn_up, ffn_down, mix_in_w, mix_out_w, pool_w, pool_scale,
           mlstm_conv_w, mlstm_conv_b, mlstm_qkv_w, mlstm_gate_b, mlstm_norm_w, mlstm_skip):
    batch, seq, d = x.shape
    depth = ada_w.shape[0]
    assert d == ATT_WIDTH + POOL_WIDTH + MLSTM_WIDTH
    assert seq % ATT_TILE == 0 and seq % ROW_TILE == 0 and seq % SEQ_CHUNK == 0 and batch % SEQ_GROUP == 0

    up_b = ffn_up.astype(BF16)
    down_b = ffn_down.astype(BF16)
    out_b = mix_out_w.astype(BF16)
    n_main = QKV_W + POOL_WIDTH + 2 * MLSTM_WIDTH
    w_qkv = mix_in_w[:, :, :QKV_W].astype(BF16)
    w_rest = mix_in_w[:, :, QKV_W:n_main].astype(BF16)
    w_gate = jnp.pad(mix_in_w[:, :, n_main:], ((0, 0), (0, 0), (0, GATE_LANES - 2 * MLSTM_HEADS))).astype(BF16)
    gate_bias = jnp.pad(mlstm_gate_b.reshape(depth, 1, 2 * MLSTM_HEADS),
                        ((0, 0), (0, 0), (0, GATE_LANES - 2 * MLSTM_HEADS)))
    groups = len(POOL_WINDOWS)
    eye = jnp.eye(groups, dtype=pool_w.dtype)
    pool_bd = (pool_w[:, :, :, None, :] * eye[None, :, None, :, None]).reshape(depth, POOL_WIDTH, POOL_WIDTH).astype(BF16)
    w_qk = jnp.concatenate([mlstm_qkv_w[:, 0], mlstm_qkv_w[:, 1]], axis=-1).astype(BF16)
    w_v = mlstm_qkv_w[:, 2].astype(BF16)

    mod = _adaln(c, ada_w, ada_b).reshape(depth, batch, 9, d)

    h = x.reshape(batch * seq, d)
    for l in range(depth):
        h = _ffn(h, mod[l], pre_norm_w[l], post_norm_w[l], up_b, down_b, l, 0, seq)
        *qkv, rest, gates = _inproj(h, mod[l], pre_norm_w[l], w_qkv, w_rest, w_gate, l, batch, seq)
        att = _attention(qkv, batch, seq)
        pool, hm = _mix(rest, gates, pool_bd[l], pool_scale[l].reshape(1, -1), mlstm_conv_w[l],
                        mlstm_conv_b[l].reshape(1, -1), w_qk[l], w_v[l], gate_bias[l],
                        mlstm_norm_w[l].reshape(1, -1), mlstm_skip[l].reshape(1, -1), batch, seq)
        h = _outproj(h, att, pool, hm, mod[l], post_norm_w[l], out_b, l, seq)
        h = _ffn(h, mod[l], pre_norm_w[l], post_norm_w[l], up_b, down_b, l, 1, seq)
    return h.reshape(batch, seq, d)
```

```python
import functools

import jax
import jax.numpy as jnp
from jax import lax
from jax.experimental import pallas as pl
from jax.experimental.pallas import tpu as pltpu

F32 = jnp.float32
BF16 = jnp.bfloat16

ATT_HEADS = 4
ATT_HEAD_DIM = 64
ATT_WIDTH = ATT_HEADS * ATT_HEAD_DIM
DILATIONS = (1, 4, 16)
ATT_SPAN = 128
POOL_WINDOWS = (2, 4, 8, 16)
POOL_GROUP_DIM = 64
POOL_WIDTH = len(POOL_WINDOWS) * POOL_GROUP_DIM
MLSTM_HEADS = 4
MLSTM_HEAD_DIM = 128
MLSTM_WIDTH = MLSTM_HEADS * MLSTM_HEAD_DIM
MLSTM_CONV = 4
MACARON_WEIGHT = 0.5
EPS = 1e-6
NEG = -1e30

LANES = 128
SUBLANES = 8
VMEM_LIMIT_BYTES = 56 * 1024 * 1024

ROW_TILE = 1024
ROW_PARTS = 2
ATT_TILE = 2048
ATT_BLOCK = 128
SEQ_CHUNK = 256
SEQ_GROUP = 4
HIST = 16
GATE_LANES = LANES


def _params(sem, **flags):
    return pltpu.CompilerParams(dimension_semantics=sem, vmem_limit_bytes=VMEM_LIMIT_BYTES, flags=flags or None)


def _rms(x):
    return x * lax.rsqrt(jnp.mean(x * x, axis=-1, keepdims=True) + EPS)


def _adaln_kernel(c_ref, w_ref, b_ref, o_ref):
    c = c_ref[...]
    c_act = (c * jax.nn.sigmoid(c)).astype(BF16)
    o_ref[...] = jnp.dot(c_act, w_ref[...].astype(BF16), preferred_element_type=F32) + b_ref[...]


def _adaln(c, ada_w, ada_b):
    depth, d, n = ada_w.shape
    b = c.shape[0]
    tn = n // 8
    return pl.pallas_call(
        _adaln_kernel,
        out_shape=jax.ShapeDtypeStruct((depth, b, n), F32),
        grid=(depth, n // tn),
        in_specs=[
            pl.BlockSpec((b, d), lambda l, j: (0, 0)),
            pl.BlockSpec((None, d, tn), lambda l, j: (l, 0, j)),
            pl.BlockSpec((None, 1, tn), lambda l, j: (l, 0, j)),
        ],
        out_specs=pl.BlockSpec((None, b, tn), lambda l, j: (l, 0, j)),
        compiler_params=_params(("parallel", "parallel")),
        name="adaln",
    )(c, ada_w, ada_b.reshape(depth, 1, n))


def _modulated(x, mod_ref, prew_ref, sub):
    scale = prew_ref[sub:sub + 1, :] * (1.0 + mod_ref[3 * sub + 1:3 * sub + 2, :])
    return (_rms(x) * scale + mod_ref[3 * sub:3 * sub + 1, :]).astype(BF16)


def _part(ref, part):
    rows = ref.shape[0] // ROW_PARTS
    return slice(part * rows, (part + 1) * rows)


def _ffn_kernel(x_ref, mod_ref, prew_ref, postw_ref, wup_ref, wdown_ref, o_ref, *, sub):
    gate = postw_ref[sub:sub + 1, :] * (MACARON_WEIGHT * mod_ref[3 * sub + 2:3 * sub + 3, :])
    u = _modulated(x_ref[_part(x_ref, 0), :], mod_ref, prew_ref, sub)
    for part in range(ROW_PARTS):
        rows = _part(x_ref, part)
        gv = jnp.dot(u, wup_ref[...], preferred_element_type=F32)
        if part + 1 < ROW_PARTS:
            u = _modulated(x_ref[_part(x_ref, part + 1), :], mod_ref, prew_ref, sub)
        ff = gv.shape[1] // 2
        g, v = gv[:, :ff], gv[:, ff:]
        a = (g * jax.nn.sigmoid(g) * v).astype(BF16)
        y = jnp.dot(a, wdown_ref[...], preferred_element_type=F32)
        o_ref[rows, :] = x_ref[rows, :] + _rms(y) * gate


def _inproj_kernel(x_ref, mod_ref, prew_ref, wqkv_ref, wrest_ref, wgate_ref,
                   qkv1_ref, qkv4_ref, qkv16_ref, rest_ref, gate_ref, z_ref):
    slabs = z_ref.shape[0]

    def by_residue(part):
        rows = _part(x_ref, part)
        n = rows.stop - rows.start
        for dil, out_ref in ((DILATIONS[1], qkv4_ref), (DILATIONS[2], qkv16_ref)):
            per = n // dil
            for r in range(dil):
                for c in range(slabs):
                    out_ref[r, part * per:(part + 1) * per, c * LANES:(c + 1) * LANES] = (
                        z_ref[c, pl.ds(rows.start + r, per, stride=dil), :].astype(BF16))

    u = _modulated(x_ref[_part(x_ref, 0), :], mod_ref, prew_ref, 1)
    for part in range(ROW_PARTS):
        rows = _part(x_ref, part)
        z = jnp.dot(u, wqkv_ref[...], preferred_element_type=F32)
        qkv1_ref[0, rows, :] = z.astype(BF16)
        for c in range(slabs):
            z_ref[c, rows, :] = z[:, c * LANES:(c + 1) * LANES]
        rest_ref[rows, :] = jnp.dot(u, wrest_ref[...], preferred_element_type=F32).astype(BF16)
        gate_ref[rows, :] = jnp.dot(u, wgate_ref[...], preferred_element_type=F32)
        if part + 1 < ROW_PARTS:
            u = _modulated(x_ref[_part(x_ref, part + 1), :], mod_ref, prew_ref, 1)
        by_residue(part)


def _outproj_kernel(x_ref, att_ref, pool_ref, hm_ref, mod_ref, postw_ref, wa_ref, wp_ref, wh_ref, o_ref):
    gate = postw_ref[1:2, :] * mod_ref[5:6, :]
    ys = []
    for part in range(ROW_PARTS):
        rows = _part(x_ref, part)
        y = jnp.dot(att_ref[rows, :], wa_ref[...], preferred_element_type=F32)
        y = y + jnp.dot(pool_ref[rows, :], wp_ref[...], preferred_element_type=F32)
        ys.append(y + jnp.dot(hm_ref[rows, :], wh_ref[...], preferred_element_type=F32))
    for part in range(ROW_PARTS):
        rows = _part(x_ref, part)
        o_ref[rows, :] = x_ref[rows, :] + _rms(ys[part]) * gate


def _resident(shape, index):
    return pl.BlockSpec(shape, index, pipeline_mode=pl.Buffered(1))


def _row_specs(d, seq):
    tm = ROW_TILE
    per_batch = seq // tm
    return (pl.BlockSpec((tm, d), lambda i: (i, 0)),
            pl.BlockSpec((None, 9, d), lambda i: (i // per_batch, 0, 0)),
            pl.BlockSpec((3, d), lambda i: (0, 0)))


def _ffn(h, mod, pre_w, post_w, w_up, w_down, layer, slot, seq):
    t, d = h.shape
    ff = w_down.shape[2]
    rows, mods, gains = _row_specs(d, seq)
    return pl.pallas_call(
        functools.partial(_ffn_kernel, sub=2 * slot),
        out_shape=jax.ShapeDtypeStruct((t, d), F32),
        grid=(t // ROW_TILE,),
        in_specs=[
            rows, mods, gains, gains,
            _resident((None, None, d, 2 * ff), lambda i: (layer, slot, 0, 0)),
            _resident((None, None, ff, d), lambda i: (layer, slot, 0, 0)),
        ],
        out_specs=rows,
        compiler_params=_params(("parallel",)),
        name="ffn",
    )(h, mod, pre_w, post_w, w_up, w_down)


def _inproj(h, mod, pre_w, w_qkv, w_rest, w_gate, layer, batch, seq):
    t, d = h.shape
    tm = ROW_TILE
    per_batch = seq // tm
    n_qkv, n_rest, n_gate = w_qkv.shape[2], w_rest.shape[2], w_gate.shape[2]
    rows, mods, gains = _row_specs(d, seq)
    by_residue = lambda i: (i // per_batch, 0, i % per_batch, 0)
    qkv_shapes = [jax.ShapeDtypeStruct((batch, dil, seq // dil, n_qkv), BF16) for dil in DILATIONS]
    qkv_specs = [pl.BlockSpec((None, dil, tm // dil, n_qkv), by_residue) for dil in DILATIONS]
    return pl.pallas_call(
        _inproj_kernel,
        out_shape=(
            *qkv_shapes,
            jax.ShapeDtypeStruct((t, n_rest), BF16),
            jax.ShapeDtypeStruct((t, n_gate), F32),
        ),
        grid=(t // tm,),
        in_specs=[
            rows, mods, gains,
            _resident((None, d, n_qkv), lambda i: (layer, 0, 0)),
            _resident((None, d, n_rest), lambda i: (layer, 0, 0)),
            _resident((None, d, n_gate), lambda i: (layer, 0, 0)),
        ],
        out_specs=(
            *qkv_specs,
            pl.BlockSpec((tm, n_rest), lambda i: (i, 0)),
            pl.BlockSpec((tm, n_gate), lambda i: (i, 0)),
        ),
        scratch_shapes=[pltpu.VMEM((n_qkv // LANES, tm, LANES), F32)],
        compiler_params=_params(("parallel",)),
        name="inproj",
    )(h, mod, pre_w, w_qkv, w_rest, w_gate)


def _outproj(h, att, pool, hm, mod, post_w, w_out, layer, seq):
    t, d = h.shape
    tm = ROW_TILE
    rows, mods, gains = _row_specs(d, seq)
    widths = (att.shape[-1], pool.shape[-1], hm.shape[-1])
    w_specs = []
    off = 0
    for wdt in widths:
        w_specs.append(_resident((None, wdt, d), functools.partial(lambda i, blk: (layer, blk, 0), blk=off // wdt)))
        off += wdt
    return pl.pallas_call(
        _outproj_kernel,
        out_shape=jax.ShapeDtypeStruct((t, d), F32),
        grid=(t // tm,),
        in_specs=[
            rows,
            pl.BlockSpec((tm, widths[0]), lambda i: (i, 0)),
            pl.BlockSpec((tm, widths[1]), lambda i: (i, 0)),
            pl.BlockSpec((tm, widths[2]), lambda i: (i, 0)),
            mods, gains,
            *w_specs,
        ],
        out_specs=rows,
        compiler_params=_params(("parallel",)),
        name="outproj",
    )(h, att.reshape(t, widths[0]), pool.reshape(t, widths[1]), hm.reshape(t, widths[2]), mod, post_w,
      w_out, w_out, w_out)


QKV_W = 3 * ATT_WIDTH


def _attn_scores(q, kwin, bias, head_of_lane):
    q32 = q.astype(F32) * (ATT_HEAD_DIM ** -0.5)
    lhs = jnp.concatenate(
        [jnp.where(head_of_lane == h, q32, 0.0) for h in range(ATT_HEADS)], axis=0).astype(BF16)
    return lax.dot_general(lhs, kwin, (((1,), (1,)), ((), ())), preferred_element_type=F32) + bias


def _by_lane_half(x, low):
    nq = x.shape[0] // ATT_HEADS
    out = []
    for half in range(2):
        first = x[(2 * half) * nq:(2 * half + 1) * nq]
        second = x[(2 * half + 1) * nq:(2 * half + 2) * nq]
        if x.shape[1] != 1:
            first = first[:, half * LANES:(half + 1) * LANES]
            second = second[:, half * LANES:(half + 1) * LANES]
        out.append(jnp.where(low, first, second))
    return out


def _attn_kernel(c1_ref, h1_ref, c4_ref, h4_ref, c16_ref, h16_ref, o_ref, bias_ref, dst_ref, s_ref, p_ref):
    t = pl.program_id(1)
    nq = ATT_BLOCK
    items = ATT_TILE // nq
    head_of_lane = lax.broadcasted_iota(jnp.int32, (nq, ATT_WIDTH), 1) >> (ATT_HEAD_DIM.bit_length() - 1)
    low = lax.broadcasted_iota(jnp.int32, (nq, LANES), 1) < ATT_HEAD_DIM

    qi = lax.broadcasted_iota(jnp.int32, (ATT_HEADS * nq, 2 * nq), 0) & (nq - 1)
    ki = lax.broadcasted_iota(jnp.int32, (ATT_HEADS * nq, 2 * nq), 1)
    ok = (ki >= qi) & (ki <= qi + ATT_SPAN)
    bias_ref[0] = jnp.where(ok, 0.0, NEG)
    bias_ref[1] = jnp.where(ok & (ki >= nq), 0.0, NEG)
    qs = slice(0, ATT_WIDTH)
    ks = slice(ATT_WIDTH, 2 * ATT_WIDTH)
    vs = slice(2 * ATT_WIDTH, 3 * ATT_WIDTH)

    branches = tuple(zip((c1_ref, c4_ref, c16_ref), (h1_ref, h4_ref, h16_ref), DILATIONS))

    def split(dil, idx):
        blocks = items // dil
        return idx >> (blocks.bit_length() - 1), idx & (blocks - 1)

    def window(cur_ref, halo_ref, dil, idx, cols):
        r, i = split(dil, idx)
        own = cur_ref[r, pl.ds(pl.multiple_of(i * nq, nq), nq), cols]
        if items == dil:
            prev = halo_ref[r, :, cols]
        else:
            prev0 = pl.multiple_of(jnp.maximum(i - 1, 0) * nq, nq)
            prev = jnp.where(i == 0, halo_ref[r, :, cols], cur_ref[r, pl.ds(prev0, nq), cols])
        return jnp.concatenate([prev, own], axis=0)

    def put(branch, quantity, halves, dil, idx):
        r, i = split(dil, idx)
        start = i * (nq * dil) + r
        rows = pl.ds(start, nq) if dil == 1 else pl.ds(start, nq, stride=dil)
        for half, piece in enumerate(halves):
            dst_ref[branch, quantity, half, rows, :] = piece

    for branch in range(len(branches)):
        s_ref[branch, 1] = jnp.zeros(s_ref.shape[2:], s_ref.dtype)
        p_ref[branch, 1] = jnp.zeros(p_ref.shape[2:], p_ref.dtype)

    def trip(j, slot):
        other = 1 - slot
        item_a = jnp.minimum(j, items - 1)
        item_b = jnp.clip(j - 1, 0, items - 1)
        item_c = jnp.clip(j - 2, 0, items - 1)

        for branch, (cur_ref, halo_ref, dil) in enumerate(branches):
            pv = jnp.dot(p_ref[branch, other], window(cur_ref, halo_ref, dil, item_c, vs),
                         preferred_element_type=F32)
            put(branch, 0, _by_lane_half(pv, low), dil, item_c)

        for branch, (cur_ref, halo_ref, dil) in enumerate(branches):
            s = s_ref[branch, other]
            m = jnp.max(s, axis=-1, keepdims=True)
            p = jnp.exp(s - m)
            l = jnp.sum(p, axis=-1, keepdims=True)
            p_ref[branch, slot] = p.astype(BF16)
            put(branch, 1, _by_lane_half(m, low), dil, item_b)
            put(branch, 2, _by_lane_half(l, low), dil, item_b)

        for branch, (cur_ref, halo_ref, dil) in enumerate(branches):
            r, i = split(dil, item_a)
            no_past = jnp.where((i == 0) & (t == 0), 1, 0)
            q = cur_ref[r, pl.ds(pl.multiple_of(i * nq, nq), nq), qs]
            s_ref[branch, slot] = _attn_scores(q, window(cur_ref, halo_ref, dil, item_a, ks),
                                               bias_ref[no_past], head_of_lane)

    def body(pair, carry):
        trip(2 * pair, 0)
        trip(2 * pair + 1, 1)
        return carry
    lax.fori_loop(0, (items + 2) // 2, body, 0)

    rows = ATT_BLOCK

    def merge(c, carry):
        sl = pl.ds(pl.multiple_of(c * rows, rows), rows)
        for half in range(2):
            maxes = [dst_ref[branch, 1, half, sl, :] for branch in range(len(branches))]
            top = functools.reduce(jnp.maximum, maxes)
            num = 0.0
            den = 0.0
            for branch, m_b in enumerate(maxes):
                e = jnp.exp(m_b - top)
                num = num + e * dst_ref[branch, 0, half, sl, :]
                den = den + e * dst_ref[branch, 2, half, sl, :]
            o_ref[sl, half * LANES:(half + 1) * LANES] = (num / den).astype(o_ref.dtype)
        return carry
    lax.fori_loop(0, ATT_TILE // rows, merge, 0)


def _attention(qkv_by_dilation, batch, seq):
    tiles = seq // ATT_TILE
    in_specs = []
    operands = []
    for dil, view in zip(DILATIONS, qkv_by_dilation):
        cur_rows = ATT_TILE // dil
        per_tile = cur_rows // ATT_BLOCK
        in_specs.append(pl.BlockSpec((None, dil, cur_rows, QKV_W), lambda b, t: (b, 0, t, 0)))
        in_specs.append(pl.BlockSpec(
            (None, dil, ATT_BLOCK, QKV_W),
            functools.partial(lambda b, t, per_tile: (b, 0, jnp.maximum(t * per_tile - 1, 0), 0), per_tile=per_tile)))
        operands += [view, view]
    stacked = (ATT_HEADS * ATT_BLOCK, 2 * ATT_BLOCK)
    return pl.pallas_call(
        _attn_kernel,
        out_shape=jax.ShapeDtypeStruct((batch, seq, ATT_WIDTH), BF16),
        grid=(batch, tiles),
        in_specs=in_specs,
        out_specs=pl.BlockSpec((None, ATT_TILE, ATT_WIDTH), lambda b, t: (b, t, 0)),
        scratch_shapes=[
            pltpu.VMEM((2,) + stacked, F32),
            pltpu.VMEM((len(DILATIONS), 3, 2, ATT_TILE, LANES), F32),
            pltpu.VMEM((len(DILATIONS), 2) + stacked, F32),
            pltpu.VMEM((len(DILATIONS), 2) + stacked, BF16),
        ],
        compiler_params=_params(("parallel", "parallel")),
        name="dilated_attention",
    )(*operands)


def _split3(x):
    hi = x.astype(BF16)
    r1 = x - hi.astype(F32)
    mid = r1.astype(BF16)
    lo = (r1 - mid.astype(F32)).astype(BF16)
    return hi, mid, lo


def _log_sigmoid(x):
    return -(jnp.maximum(-x, 0.0) + jnp.log1p(jnp.exp(-jnp.abs(x))))


def _mix_kernel(rest_ref, gate_ref, poolw_ref, pscale_ref, convw_ref, convb_ref, wqk_ref, wv_ref,
                gbias_ref, normw_ref, skip_ref, pool_ref, hm_ref,
                phist_ref, chist_ref, cstate_ref, mstate_ref, band_ref, shift_ref, upper_ref):
    s_idx = pl.program_id(1)
    n = SEQ_CHUNK
    dh = MLSTM_HEAD_DIM
    heads = MLSTM_HEADS
    group = range(SEQ_GROUP)

    @pl.when(s_idx == 0)
    def _():
        phist_ref[...] = jnp.zeros_like(phist_ref)
        chist_ref[...] = jnp.zeros_like(chist_ref)
        cstate_ref[...] = jnp.zeros_like(cstate_ref)
        mstate_ref[...] = jnp.zeros_like(mstate_ref)
        t_row = lax.broadcasted_iota(jnp.int32, (n, HIST + n), 0) + HIST
        col = lax.broadcasted_iota(jnp.int32, (n, HIST + n), 1)
        for g, wlen in enumerate(POOL_WINDOWS):
            band_ref[g * n:(g + 1) * n, :] = jnp.where((col <= t_row) & (col > t_row - wlen), 1.0, 0.0).astype(BF16)
        for back in range(1, MLSTM_CONV):
            shift_ref[(back - 1) * n:back * n, :] = jnp.where(col == t_row - back, 1.0, 0.0).astype(BF16)
        src = lax.broadcasted_iota(jnp.int32, (n, n), 0)
        dst = lax.broadcasted_iota(jnp.int32, (n, n), 1)
        upper_ref[...] = jnp.where(src <= dst, 1.0, 0.0).astype(BF16)

    lane = lax.broadcasted_iota(jnp.int32, (n, POOL_WIDTH), 1)
    pool_group = lane >> (POOL_GROUP_DIM.bit_length() - 1)
    pos = lax.broadcasted_iota(jnp.int32, (n, POOL_WIDTH), 0) + s_idx * n
    win_len = jnp.full((n, POOL_WIDTH), POOL_WINDOWS[0], jnp.int32)
    for g, wlen in enumerate(POOL_WINDOWS):
        win_len = jnp.where(pool_group == g, wlen, win_len)
    count = jnp.minimum(pos + 1, win_len).astype(F32)
    row = lax.broadcasted_iota(jnp.int32, (n, n), 0)
    col = lax.broadcasted_iota(jnp.int32, (n, n), 1)
    causal = col <= row
    lane_g = lax.broadcasted_iota(jnp.int32, (n, GATE_LANES), 1)
    row8 = lax.broadcasted_iota(jnp.int32, (SUBLANES, n), 0)
    ones = jnp.ones((n, dh), F32)

    def window_delta(b):
        p_b = rest_ref[b, :, 0:POOL_WIDTH]
        ext = jnp.concatenate([phist_ref[b].astype(BF16), p_b], axis=0)
        phist_ref[b] = p_b[n - HIST:, :].astype(F32)
        sums = jnp.dot(band_ref[...], ext, preferred_element_type=F32)
        win_sum = sums[0:n]
        for g in range(1, len(POOL_WINDOWS)):
            win_sum = jnp.where(pool_group == g, sums[g * n:(g + 1) * n], win_sum)
        return (win_sum / count - p_b.astype(F32)).astype(BF16)

    def front(b):
        xm_b = rest_ref[b, :, POOL_WIDTH:POOL_WIDTH + MLSTM_WIDTH]
        cext = jnp.concatenate([chist_ref[b].astype(BF16), xm_b], axis=0)
        chist_ref[b] = xm_b[n - HIST:, :].astype(F32)
        shifted = jnp.dot(shift_ref[...], cext, preferred_element_type=F32)
        conv = xm_b.astype(F32) * convw_ref[MLSTM_CONV - 1:MLSTM_CONV, :] + convb_ref[...]
        for back in range(1, MLSTM_CONV):
            tap = MLSTM_CONV - 1 - back
            conv = conv + shifted[(back - 1) * n:back * n] * convw_ref[tap:tap + 1, :]
        xc = conv * jax.nn.sigmoid(conv)
        return xc, xc.astype(BF16), xm_b

    def gate_rows(b):
        gates_t = (gate_ref[b] + gbias_ref[...]).T
        i_rows = gates_t[0:SUBLANES, :]
        return i_rows, _log_sigmoid(pltpu.roll(i_rows, SUBLANES - heads, 0))

    def gate_stats(b, i_rows, bcum):
        x_ib = i_rows - bcum
        b_last = bcum[:, n - 1:n]
        m_prev = mstate_ref[b, :, 0:1]
        a = b_last + x_ib
        m_new = jnp.maximum(b_last + m_prev, jnp.max(a, axis=1, keepdims=True))
        decay = jnp.exp(b_last + m_prev - m_new)
        w_state = jnp.exp(a - m_new)
        mstate_ref[b] = jnp.broadcast_to(m_new, mstate_ref.shape[1:])
        packed = jnp.concatenate(
            [jnp.where(row8 < heads, bcum, pltpu.roll(w_state, heads, 0)), bcum + m_prev,
             jnp.zeros((LANES - 2 * SUBLANES, n), F32)], axis=0)
        return x_ib, packed.T, decay

    def decay_log(x_ib, cols, h):
        log_d = jnp.where(causal, cols[:, h:h + 1] + x_ib[h:h + 1, :], NEG)
        return log_d, jnp.max(log_d, axis=-1, keepdims=True)

    def stabilisers(cols, row_maxes):
        intra_max = jnp.zeros((n, GATE_LANES), F32)
        for h, row_max in enumerate(row_maxes):
            intra_max = jnp.where(lane_g == 2 * heads + h, row_max, intra_max)
        m_all = jnp.maximum(cols, intra_max)
        return m_all, jnp.exp(cols - m_all), jnp.exp(-m_all)

    def head(b, h, qk, v, xc, cols, decay, log_d, m_all, inter_w_all, floor_all):
        hs = slice(h * dh, (h + 1) * dh)
        lane_m = 2 * heads + h
        q = (qk[:, :dh] * (dh ** -0.5)).astype(BF16)
        k = qk[:, dh:].astype(BF16)
        v_aug = jnp.concatenate([v, ones], axis=1)

        s = lax.dot_general(q, k, (((1,), (1,)), ((), ())), preferred_element_type=F32)
        sw = (s * jnp.exp(log_d - m_all[:, lane_m:lane_m + 1])).astype(BF16)
        c_prev = cstate_ref[b, h]
        out = inter_w_all[:, lane_m:lane_m + 1] * jnp.dot(q, c_prev.astype(BF16), preferred_element_type=F32) \
            + jnp.dot(sw, v_aug.astype(BF16), preferred_element_type=F32)
        hh = out[:, :dh] / jnp.maximum(jnp.abs(out[:, dh:]), floor_all[:, lane_m:lane_m + 1])

        wv_state = (cols[:, heads + h:heads + h + 1] * v_aug).astype(BF16)
        d_c = lax.dot_general(k, wv_state, (((0,), (0,)), ((), ())), preferred_element_type=F32)
        cstate_ref[b, h] = decay[h:h + 1, :] * c_prev + d_c

        og = rest_ref[b, :, POOL_WIDTH + MLSTM_WIDTH + h * dh:POOL_WIDTH + MLSTM_WIDTH + (h + 1) * dh].astype(F32)
        hh = hh * jax.nn.sigmoid(og)
        hh = _rms(hh) * normw_ref[:, hs]
        hm_ref[b, :, hs] = (hh + skip_ref[:, hs] * xc[:, hs]).astype(hm_ref.dtype)

    def rows_of(stacked, b, rows):
        return stacked[b * rows:(b + 1) * rows]

    raw = [gate_rows(b) for b in group]
    parts = [part for _, logf in raw for part in _split3(logf)]
    cums = jnp.dot(jnp.concatenate(parts, axis=0), upper_ref[...], preferred_element_type=F32)
    gated = []
    for b in group:
        piece = rows_of(cums, b, 3 * SUBLANES)
        bcum = piece[0:SUBLANES] + piece[SUBLANES:2 * SUBLANES] + piece[2 * SUBLANES:3 * SUBLANES]
        gated.append(gate_stats(b, raw[b][0], bcum))

    matmul_work = [lambda b=b: window_delta(b) for b in group] + [lambda b=b: front(b) for b in group]
    matmul_done = []
    decay = {}
    for b in group:
        x_ib, cols, _ = gated[b]
        for h in range(heads):
            decay[b, h] = decay_log(x_ib, cols, h)
            if matmul_work and h % 2 == 0:
                matmul_done.append(matmul_work.pop(0)())
    matmul_done += [work() for work in matmul_work]
    deltas, fronts = matmul_done[:len(group)], matmul_done[len(group):]
    stats = [stabilisers(gated[b][1], [decay[b, h][1] for h in range(heads)]) for b in group]

    pooled = jnp.dot(jnp.concatenate(deltas, axis=0), poolw_ref[...], preferred_element_type=F32) * pscale_ref[...]
    for b in group:
        pool_ref[b] = rows_of(pooled, b, n).astype(pool_ref.dtype)

    for h in range(heads):
        hs = slice(h * dh, (h + 1) * dh)
        qk_all = jnp.dot(jnp.concatenate([fronts[b][1][:, hs] for b in group], axis=0), wqk_ref[h],
                         preferred_element_type=F32)
        v_all = jnp.dot(jnp.concatenate([fronts[b][2][:, hs] for b in group], axis=0), wv_ref[h],
                        preferred_element_type=F32)
        for b in group:
            _, cols, state_decay = gated[b]
            m_all, inter_w_all, floor_all = stats[b]
            head(b, h, rows_of(qk_all, b, n), rows_of(v_all, b, n), fronts[b][0], cols, state_decay,
                 decay[b, h][0], m_all, inter_w_all, floor_all)


def _mix(rest, gates, pool_bd, pool_scale, conv_w, conv_b, w_qk, w_v, gate_bias, norm_w, skip, batch, seq):
    n = SEQ_CHUNK
    sg = SEQ_GROUP
    n_rest = rest.shape[-1]
    rest = rest.reshape(batch, seq, n_rest)
    gates = gates.reshape(batch, seq, GATE_LANES)
    c2 = lambda b, s: (0, 0)
    c3 = lambda b, s: (0, 0, 0)
    chunk = lambda b, s: (b, s, 0)
    return pl.pallas_call(
        _mix_kernel,
        out_shape=(
            jax.ShapeDtypeStruct((batch, seq, POOL_WIDTH), BF16),
            jax.ShapeDtypeStruct((batch, seq, MLSTM_WIDTH), BF16),
        ),
        grid=(batch // sg, seq // n),
        in_specs=[
            pl.BlockSpec((sg, n, n_rest), chunk),
            pl.BlockSpec((sg, n, GATE_LANES), chunk),
            pl.BlockSpec(pool_bd.shape, c2),
            pl.BlockSpec(pool_scale.shape, c2),
            pl.BlockSpec(conv_w.shape, c2),
            pl.BlockSpec(conv_b.shape, c2),
            pl.BlockSpec(w_qk.shape, c3),
            pl.BlockSpec(w_v.shape, c3),
            pl.BlockSpec(gate_bias.shape, c2),
            pl.BlockSpec(norm_w.shape, c2),
            pl.BlockSpec(skip.shape, c2),
        ],
        out_specs=(
            pl.BlockSpec((sg, n, POOL_WIDTH), chunk),
            pl.BlockSpec((sg, n, MLSTM_WIDTH), chunk),
        ),
        scratch_shapes=[
            pltpu.VMEM((sg, HIST, POOL_WIDTH), F32),
            pltpu.VMEM((sg, HIST, MLSTM_WIDTH), F32),
            pltpu.VMEM((sg, MLSTM_HEADS, MLSTM_HEAD_DIM, 2 * MLSTM_HEAD_DIM), F32),
            pltpu.VMEM((sg, SUBLANES, GATE_LANES), F32),
            pltpu.VMEM((len(POOL_WINDOWS) * n, HIST + n), BF16),
            pltpu.VMEM(((MLSTM_CONV - 1) * n, HIST + n), BF16),
            pltpu.VMEM((n, n), BF16),
        ],
        compiler_params=_params(("parallel", "arbitrary")),
        name="pool_mlstm",
    )(rest, gates, pool_bd, pool_scale, conv_w, conv_b, w_qk, w_v, gate_bias, norm_w, skip)


def kernel(x, c, ada_w, ada_b, pre_norm_w, post_norm_w, ffn_up, ffn_down, mix_in_w, mix_out_w, pool_w, pool_scale,
           mlstm_conv_w, mlstm_conv_b, mlstm_qkv_w, mlstm_gate_b, mlstm_norm_w, mlstm_skip):
    batch, seq, d = x.shape
    depth = ada_w.shape[0]
    assert d == ATT_WIDTH + POOL_WIDTH + MLSTM_WIDTH
    assert seq % ATT_TILE == 0 and seq % ROW_TILE == 0 and seq % SEQ_CHUNK == 0 and batch % SEQ_GROUP == 0

    up_b = ffn_up.astype(BF16)
    down_b = ffn_down.astype(BF16)
    out_b = mix_out_w.astype(BF16)
    n_main = QKV_W + POOL_WIDTH + 2 * MLSTM_WIDTH
    w_qkv = mix_in_w[:, :, :QKV_W].astype(BF16)
    w_rest = mix_in_w[:, :, QKV_W:n_main].astype(BF16)
    w_gate = jnp.pad(mix_in_w[:, :, n_main:], ((0, 0), (0, 0), (0, GATE_LANES - 2 * MLSTM_HEADS))).astype(BF16)
    gate_bias = jnp.pad(mlstm_gate_b.reshape(depth, 1, 2 * MLSTM_HEADS),
                        ((0, 0), (0, 0), (0, GATE_LANES - 2 * MLSTM_HEADS)))
    groups = len(POOL_WINDOWS)
    eye = jnp.eye(groups, dtype=pool_w.dtype)
    pool_bd = (pool_w[:, :, :, None, :] * eye[None, :, None, :, None]).reshape(depth, POOL_WIDTH, POOL_WIDTH).astype(BF16)
    w_qk = jnp.concatenate([mlstm_qkv_w[:, 0], mlstm_qkv_w[:, 1]], axis=-1).astype(BF16)
    w_v = mlstm_qkv_w[:, 2].astype(BF16)

    mod = _adaln(c, ada_w, ada_b).reshape(depth, batch, 9, d)

    h = x.reshape(batch * seq, d)
    for l in range(depth):
        h = _ffn(h, mod[l], pre_norm_w[l], post_norm_w[l], up_b, down_b, l, 0, seq)
        *qkv, rest, gates = _inproj(h, mod[l], pre_norm_w[l], w_qkv, w_rest, w_gate, l, batch, seq)
        att = _attention(qkv, batch, seq)
        pool, hm = _mix(rest, gates, pool_bd[l], pool_scale[l].reshape(1, -1), mlstm_conv_w[l],
                        mlstm_conv_b[l].reshape(1, -1), w_qk[l], w_v[l], gate_bias[l],
                        mlstm_norm_w[l].reshape(1, -1), mlstm_skip[l].reshape(1, -1), batch, seq)
        h = _outproj(h, att, pool, hm, mod[l], post_norm_w[l], out_b, l, seq)
        h = _ffn(h, mod[l], pre_norm_w[l], post_norm_w[l], up_b, down_b, l, 1, seq)
    return h.reshape(batch, seq, d)
```

```python
import functools

import jax
import jax.numpy as jnp
from jax import lax
from jax.experimental import pallas as pl
from jax.experimental.pallas import tpu as pltpu

F32 = jnp.float32
BF16 = jnp.bfloat16

ATT_HEADS = 4
ATT_HEAD_DIM = 64
ATT_WIDTH = ATT_HEADS * ATT_HEAD_DIM
DILATIONS = (1, 4, 16)
ATT_SPAN = 128
POOL_WINDOWS = (2, 4, 8, 16)
POOL_GROUP_DIM = 64
POOL_WIDTH = len(POOL_WINDOWS) * POOL_GROUP_DIM
MLSTM_HEADS = 4
MLSTM_HEAD_DIM = 128
MLSTM_WIDTH = MLSTM_HEADS * MLSTM_HEAD_DIM
MLSTM_CONV = 4
MACARON_WEIGHT = 0.5
EPS = 1e-6
NEG = -1e30

LANES = 128
SUBLANES = 8
VMEM_LIMIT_BYTES = 56 * 1024 * 1024

ROW_TILE = 1024
ROW_PARTS = 2
UP_TILE = 256
ATT_TILE = 2048
ATT_BLOCK = 128
SEQ_CHUNK = 256
SEQ_GROUP = 4
HIST = 16
GATE_LANES = LANES


def _params(sem, **flags):
    return pltpu.CompilerParams(dimension_semantics=sem, vmem_limit_bytes=VMEM_LIMIT_BYTES, flags=flags or None)


def _rms(x):
    return x * lax.rsqrt(jnp.mean(x * x, axis=-1, keepdims=True) + EPS)


def _adaln_kernel(c_ref, w_ref, b_ref, o_ref):
    c = c_ref[...]
    c_act = (c * jax.nn.sigmoid(c)).astype(BF16)
    o_ref[...] = jnp.dot(c_act, w_ref[...].astype(BF16), preferred_element_type=F32) + b_ref[...]


def _adaln(c, ada_w, ada_b):
    depth, d, n = ada_w.shape
    b = c.shape[0]
    tn = n // 8
    return pl.pallas_call(
        _adaln_kernel,
        out_shape=jax.ShapeDtypeStruct((depth, b, n), F32),
        grid=(depth, n // tn),
        in_specs=[
            pl.BlockSpec((b, d), lambda l, j: (0, 0)),
            pl.BlockSpec((None, d, tn), lambda l, j: (l, 0, j)),
            pl.BlockSpec((None, 1, tn), lambda l, j: (l, 0, j)),
        ],
        out_specs=pl.BlockSpec((None, b, tn), lambda l, j: (l, 0, j)),
        compiler_params=_params(("parallel", "parallel")),
        name="adaln",
    )(c, ada_w, ada_b.reshape(depth, 1, n))


def _modulated(x, mod_ref, prew_ref, sub):
    scale = prew_ref[sub:sub + 1, :] * (1.0 + mod_ref[3 * sub + 1:3 * sub + 2, :])
    return (_rms(x) * scale + mod_ref[3 * sub:3 * sub + 1, :]).astype(BF16)


def _part(ref, part):
    rows = ref.shape[0] // ROW_PARTS
    return slice(part * rows, (part + 1) * rows)


def _ffn_kernel(x_ref, mod_ref, prew_ref, postw_ref, wup_ref, wdown_ref, o_ref, *, sub):
    gate = postw_ref[sub:sub + 1, :] * (MACARON_WEIGHT * mod_ref[3 * sub + 2:3 * sub + 3, :])
    u = _modulated(x_ref[_part(x_ref, 0), :], mod_ref, prew_ref, sub)
    for part in range(ROW_PARTS):
        rows = _part(x_ref, part)
        gv = jnp.dot(u, wup_ref[...], preferred_element_type=F32)
        if part + 1 < ROW_PARTS:
            u = _modulated(x_ref[_part(x_ref, part + 1), :], mod_ref, prew_ref, sub)
        acts = []
        for c in range(gv.shape[1] // (2 * UP_TILE)):
            g = gv[:, 2 * c * UP_TILE:(2 * c + 1) * UP_TILE]
            v = gv[:, (2 * c + 1) * UP_TILE:(2 * c + 2) * UP_TILE]
            acts.append((g * jax.nn.sigmoid(g) * v).astype(BF16))
        a = jnp.concatenate(acts, axis=1)
        y = jnp.dot(a, wdown_ref[...], preferred_element_type=F32)
        o_ref[rows, :] = x_ref[rows, :] + _rms(y) * gate


def _inproj_kernel(x_ref, mod_ref, prew_ref, wqkv_ref, wrest_ref, wgate_ref,
                   qkv1_ref, qkv4_ref, qkv16_ref, rest_ref, gate_ref, z_ref):
    slabs = z_ref.shape[0]

    def by_residue(part):
        rows = _part(x_ref, part)
        n = rows.stop - rows.start
        for dil, out_ref in ((DILATIONS[1], qkv4_ref), (DILATIONS[2], qkv16_ref)):
            per = n // dil
            for r in range(dil):
                for c in range(slabs):
                    out_ref[r, part * per:(part + 1) * per, c * LANES:(c + 1) * LANES] = (
                        z_ref[c, pl.ds(rows.start + r, per, stride=dil), :].astype(BF16))

    u = _modulated(x_ref[_part(x_ref, 0), :], mod_ref, prew_ref, 1)
    for part in range(ROW_PARTS):
        rows = _part(x_ref, part)
        z = jnp.dot(u, wqkv_ref[...], preferred_element_type=F32)
        qkv1_ref[0, rows, :] = z.astype(BF16)
        for c in range(slabs):
            z_ref[c, rows, :] = z[:, c * LANES:(c + 1) * LANES]
        rest_ref[rows, :] = jnp.dot(u, wrest_ref[...], preferred_element_type=F32).astype(BF16)
        gate_ref[rows, :] = jnp.dot(u, wgate_ref[...], preferred_element_type=F32)
        if part + 1 < ROW_PARTS:
            u = _modulated(x_ref[_part(x_ref, part + 1), :], mod_ref, prew_ref, 1)
        by_residue(part)


def _outproj_kernel(x_ref, att_ref, pool_ref, hm_ref, mod_ref, postw_ref, wa_ref, wp_ref, wh_ref, o_ref):
    gate = postw_ref[1:2, :] * mod_ref[5:6, :]
    ys = []
    for part in range(ROW_PARTS):
        rows = _part(x_ref, part)
        y = jnp.dot(att_ref[rows, :], wa_ref[...], preferred_element_type=F32)
        y = y + jnp.dot(pool_ref[rows, :], wp_ref[...], preferred_element_type=F32)
        ys.append(y + jnp.dot(hm_ref[rows, :], wh_ref[...], preferred_element_type=F32))
    for part in range(ROW_PARTS):
        rows = _part(x_ref, part)
        o_ref[rows, :] = x_ref[rows, :] + _rms(ys[part]) * gate


def _resident(shape, index):
    return pl.BlockSpec(shape, index, pipeline_mode=pl.Buffered(1))


def _row_specs(d, seq):
    tm = ROW_TILE
    per_batch = seq // tm
    return (pl.BlockSpec((tm, d), lambda i: (i, 0)),
            pl.BlockSpec((None, 9, d), lambda i: (i // per_batch, 0, 0)),
            pl.BlockSpec((3, d), lambda i: (0, 0)))


def _ffn(h, mod, pre_w, post_w, w_up, w_down, layer, slot, seq):
    t, d = h.shape
    ff = w_down.shape[2]
    rows, mods, gains = _row_specs(d, seq)
    return pl.pallas_call(
        functools.partial(_ffn_kernel, sub=2 * slot),
        out_shape=jax.ShapeDtypeStruct((t, d), F32),
        grid=(t // ROW_TILE,),
        in_specs=[
            rows, mods, gains, gains,
            _resident((None, None, d, 2 * ff), lambda i: (layer, slot, 0, 0)),
            _resident((None, None, ff, d), lambda i: (layer, slot, 0, 0)),
        ],
        out_specs=rows,
        compiler_params=_params(("parallel",)),
        name="ffn",
    )(h, mod, pre_w, post_w, w_up, w_down)


def _inproj(h, mod, pre_w, w_qkv, w_rest, w_gate, layer, batch, seq):
    t, d = h.shape
    tm = ROW_TILE
    per_batch = seq // tm
    n_qkv, n_rest, n_gate = w_qkv.shape[2], w_rest.shape[2], w_gate.shape[2]
    rows, mods, gains = _row_specs(d, seq)
    by_residue = lambda i: (i // per_batch, 0, i % per_batch, 0)
    qkv_shapes = [jax.ShapeDtypeStruct((batch, dil, seq // dil, n_qkv), BF16) for dil in DILATIONS]
    qkv_specs = [pl.BlockSpec((None, dil, tm // dil, n_qkv), by_residue) for dil in DILATIONS]
    return pl.pallas_call(
        _inproj_kernel,
        out_shape=(
            *qkv_shapes,
            jax.ShapeDtypeStruct((t, n_rest), BF16),
            jax.ShapeDtypeStruct((t, n_gate), F32),
        ),
        grid=(t // tm,),
        in_specs=[
            rows, mods, gains,
            _resident((None, d, n_qkv), lambda i: (layer, 0, 0)),
            _resident((None, d, n_rest), lambda i: (layer, 0, 0)),
            _resident((None, d, n_gate), lambda i: (layer, 0, 0)),
        ],
        out_specs=(
            *qkv_specs,
            pl.BlockSpec((tm, n_rest), lambda i: (i, 0)),
            pl.BlockSpec((tm, n_gate), lambda i: (i, 0)),
        ),
        scratch_shapes=[pltpu.VMEM((n_qkv // LANES, tm, LANES), F32)],
        compiler_params=_params(("parallel",)),
        name="inproj",
    )(h, mod, pre_w, w_qkv, w_rest, w_gate)


def _outproj(h, att, pool, hm, mod, post_w, w_out, layer, seq):
    t, d = h.shape
    tm = ROW_TILE
    rows, mods, gains = _row_specs(d, seq)
    widths = (att.shape[-1], pool.shape[-1], hm.shape[-1])
    w_specs = []
    off = 0
    for wdt in widths:
        w_specs.append(_resident((None, wdt, d), functools.partial(lambda i, blk: (layer, blk, 0), blk=off // wdt)))
        off += wdt
    return pl.pallas_call(
        _outproj_kernel,
        out_shape=jax.ShapeDtypeStruct((t, d), F32),
        grid=(t // tm,),
        in_specs=[
            rows,
            pl.BlockSpec((tm, widths[0]), lambda i: (i, 0)),
            pl.BlockSpec((tm, widths[1]), lambda i: (i, 0)),
            pl.BlockSpec((tm, widths[2]), lambda i: (i, 0)),
            mods, gains,
            *w_specs,
        ],
        out_specs=rows,
        compiler_params=_params(("parallel",)),
        name="outproj",
    )(h, att.reshape(t, widths[0]), pool.reshape(t, widths[1]), hm.reshape(t, widths[2]), mod, post_w,
      w_out, w_out, w_out)


QKV_W = 3 * ATT_WIDTH


def _attn_scores(q, kwin, bias, head_of_lane):
    q32 = q.astype(F32) * (ATT_HEAD_DIM ** -0.5)
    lhs = jnp.concatenate(
        [jnp.where(head_of_lane == h, q32, 0.0) for h in range(ATT_HEADS)], axis=0).astype(BF16)
    return lax.dot_general(lhs, kwin, (((1,), (1,)), ((), ())), preferred_element_type=F32) + bias


def _by_lane_half(x, low):
    nq = x.shape[0] // ATT_HEADS
    out = []
    for half in range(2):
        first = x[(2 * half) * nq:(2 * half + 1) * nq]
        second = x[(2 * half + 1) * nq:(2 * half + 2) * nq]
        if x.shape[1] != 1:
            first = first[:, half * LANES:(half + 1) * LANES]
            second = second[:, half * LANES:(half + 1) * LANES]
        out.append(jnp.where(low, first, second))
    return out


def _attn_kernel(c1_ref, h1_ref, c4_ref, h4_ref, c16_ref, h16_ref, o_ref, bias_ref, dst_ref, s_ref, p_ref):
    t = pl.program_id(1)
    nq = ATT_BLOCK
    items = ATT_TILE // nq
    head_of_lane = lax.broadcasted_iota(jnp.int32, (nq, ATT_WIDTH), 1) >> (ATT_HEAD_DIM.bit_length() - 1)
    low = lax.broadcasted_iota(jnp.int32, (nq, LANES), 1) < ATT_HEAD_DIM

    qi = lax.broadcasted_iota(jnp.int32, (ATT_HEADS * nq, 2 * nq), 0) & (nq - 1)
    ki = lax.broadcasted_iota(jnp.int32, (ATT_HEADS * nq, 2 * nq), 1)
    ok = (ki >= qi) & (ki <= qi + ATT_SPAN)
    bias_ref[0] = jnp.where(ok, 0.0, NEG)
    bias_ref[1] = jnp.where(ok & (ki >= nq), 0.0, NEG)
    qs = slice(0, ATT_WIDTH)
    ks = slice(ATT_WIDTH, 2 * ATT_WIDTH)
    vs = slice(2 * ATT_WIDTH, 3 * ATT_WIDTH)

    branches = tuple(zip((c1_ref, c4_ref, c16_ref), (h1_ref, h4_ref, h16_ref), DILATIONS))

    def split(dil, idx):
        blocks = items // dil
        return idx >> (blocks.bit_length() - 1), idx & (blocks - 1)

    def window(cur_ref, halo_ref, dil, idx, cols):
        r, i = split(dil, idx)
        own = cur_ref[r, pl.ds(pl.multiple_of(i * nq, nq), nq), cols]
        if items == dil:
            prev = halo_ref[r, :, cols]
        else:
            prev0 = pl.multiple_of(jnp.maximum(i - 1, 0) * nq, nq)
            prev = jnp.where(i == 0, halo_ref[r, :, cols], cur_ref[r, pl.ds(prev0, nq), cols])
        return jnp.concatenate([prev, own], axis=0)

    def put(branch, quantity, halves, dil, idx):
        r, i = split(dil, idx)
        start = i * (nq * dil) + r
        rows = pl.ds(start, nq) if dil == 1 else pl.ds(start, nq, stride=dil)
        for half, piece in enumerate(halves):
            dst_ref[branch, quantity, half, rows, :] = piece

    for branch in range(len(branches)):
        s_ref[branch, 1] = jnp.zeros(s_ref.shape[2:], s_ref.dtype)
        p_ref[branch, 1] = jnp.zeros(p_ref.shape[2:], p_ref.dtype)

    def trip(j, slot):
        other = 1 - slot
        item_a = jnp.minimum(j, items - 1)
        item_b = jnp.clip(j - 1, 0, items - 1)
        item_c = jnp.clip(j - 2, 0, items - 1)

        for branch, (cur_ref, halo_ref, dil) in enumerate(branches):
            pv = jnp.dot(p_ref[branch, other], window(cur_ref, halo_ref, dil, item_c, vs),
                         preferred_element_type=F32)
            put(branch, 0, _by_lane_half(pv, low), dil, item_c)

        for branch, (cur_ref, halo_ref, dil) in enumerate(branches):
            s = s_ref[branch, other]
            m = jnp.max(s, axis=-1, keepdims=True)
            p = jnp.exp(s - m)
            l = jnp.sum(p, axis=-1, keepdims=True)
            p_ref[branch, slot] = p.astype(BF16)
            put(branch, 1, _by_lane_half(m, low), dil, item_b)
            put(branch, 2, _by_lane_half(l, low), dil, item_b)

        for branch, (cur_ref, halo_ref, dil) in enumerate(branches):
            r, i = split(dil, item_a)
            no_past = jnp.where((i == 0) & (t == 0), 1, 0)
            q = cur_ref[r, pl.ds(pl.multiple_of(i * nq, nq), nq), qs]
            s_ref[branch, slot] = _attn_scores(q, window(cur_ref, halo_ref, dil, item_a, ks),
                                               bias_ref[no_past], head_of_lane)

    def body(pair, carry):
        trip(2 * pair, 0)
        trip(2 * pair + 1, 1)
        return carry
    lax.fori_loop(0, (items + 2) // 2, body, 0)

    rows = ATT_BLOCK

    def merge(c, carry):
        sl = pl.ds(pl.multiple_of(c * rows, rows), rows)
        for half in range(2):
            maxes = [dst_ref[branch, 1, half, sl, :] for branch in range(len(branches))]
            top = functools.reduce(jnp.maximum, maxes)
            num = 0.0
            den = 0.0
            for branch, m_b in enumerate(maxes):
                e = jnp.exp(m_b - top)
                num = num + e * dst_ref[branch, 0, half, sl, :]
                den = den + e * dst_ref[branch, 2, half, sl, :]
            o_ref[sl, half * LANES:(half + 1) * LANES] = (num / den).astype(o_ref.dtype)
        return carry
    lax.fori_loop(0, ATT_TILE // rows, merge, 0)


def _attention(qkv_by_dilation, batch, seq):
    tiles = seq // ATT_TILE
    in_specs = []
    operands = []
    for dil, view in zip(DILATIONS, qkv_by_dilation):
        cur_rows = ATT_TILE // dil
        per_tile = cur_rows // ATT_BLOCK
        in_specs.append(pl.BlockSpec((None, dil, cur_rows, QKV_W), lambda b, t: (b, 0, t, 0)))
        in_specs.append(pl.BlockSpec(
            (None, dil, ATT_BLOCK, QKV_W),
            functools.partial(lambda b, t, per_tile: (b, 0, jnp.maximum(t * per_tile - 1, 0), 0), per_tile=per_tile)))
        operands += [view, view]
    stacked = (ATT_HEADS * ATT_BLOCK, 2 * ATT_BLOCK)
    return pl.pallas_call(
        _attn_kernel,
        out_shape=jax.ShapeDtypeStruct((batch, seq, ATT_WIDTH), BF16),
        grid=(batch, tiles),
        in_specs=in_specs,
        out_specs=pl.BlockSpec((None, ATT_TILE, ATT_WIDTH), lambda b, t: (b, t, 0)),
        scratch_shapes=[
            pltpu.VMEM((2,) + stacked, F32),
            pltpu.VMEM((len(DILATIONS), 3, 2, ATT_TILE, LANES), F32),
            pltpu.VMEM((len(DILATIONS), 2) + stacked, F32),
            pltpu.VMEM((len(DILATIONS), 2) + stacked, BF16),
        ],
        compiler_params=_params(("parallel", "parallel")),
        name="dilated_attention",
    )(*operands)


def _split3(x):
    hi = x.astype(BF16)
    r1 = x - hi.astype(F32)
    mid = r1.astype(BF16)
    lo = (r1 - mid.astype(F32)).astype(BF16)
    return hi, mid, lo


def _log_sigmoid(x):
    return -(jnp.maximum(-x, 0.0) + jnp.log1p(jnp.exp(-jnp.abs(x))))


def _mix_kernel(rest_ref, gate_ref, poolw_ref, pscale_ref, convw_ref, convb_ref, wqk_ref, wv_ref,
                gbias_ref, normw_ref, skip_ref, pool_ref, hm_ref,
                phist_ref, chist_ref, cstate_ref, mstate_ref, band_ref, shift_ref, upper_ref):
    s_idx = pl.program_id(1)
    n = SEQ_CHUNK
    dh = MLSTM_HEAD_DIM
    heads = MLSTM_HEADS
    group = range(SEQ_GROUP)

    @pl.when(s_idx == 0)
    def _():
        phist_ref[...] = jnp.zeros_like(phist_ref)
        chist_ref[...] = jnp.zeros_like(chist_ref)
        cstate_ref[...] = jnp.zeros_like(cstate_ref)
        mstate_ref[...] = jnp.zeros_like(mstate_ref)
        t_row = lax.broadcasted_iota(jnp.int32, (n, HIST + n), 0) + HIST
        col = lax.broadcasted_iota(jnp.int32, (n, HIST + n), 1)
        for g, wlen in enumerate(POOL_WINDOWS):
            band_ref[g * n:(g + 1) * n, :] = jnp.where((col <= t_row) & (col > t_row - wlen), 1.0, 0.0).astype(BF16)
        for back in range(1, MLSTM_CONV):
            shift_ref[(back - 1) * n:back * n, :] = jnp.where(col == t_row - back, 1.0, 0.0).astype(BF16)
        src = lax.broadcasted_iota(jnp.int32, (n, n), 0)
        dst = lax.broadcasted_iota(jnp.int32, (n, n), 1)
        upper_ref[...] = jnp.where(src <= dst, 1.0, 0.0).astype(BF16)

    lane = lax.broadcasted_iota(jnp.int32, (n, POOL_WIDTH), 1)
    pool_group = lane >> (POOL_GROUP_DIM.bit_length() - 1)
    pos = lax.broadcasted_iota(jnp.int32, (n, POOL_WIDTH), 0) + s_idx * n
    win_len = jnp.full((n, POOL_WIDTH), POOL_WINDOWS[0], jnp.int32)
    for g, wlen in enumerate(POOL_WINDOWS):
        win_len = jnp.where(pool_group == g, wlen, win_len)
    count = jnp.minimum(pos + 1, win_len).astype(F32)
    row = lax.broadcasted_iota(jnp.int32, (n, n), 0)
    col = lax.broadcasted_iota(jnp.int32, (n, n), 1)
    causal = col <= row
    lane_g = lax.broadcasted_iota(jnp.int32, (n, GATE_LANES), 1)
    row8 = lax.broadcasted_iota(jnp.int32, (SUBLANES, n), 0)
    ones = jnp.ones((n, dh), F32)

    def window_delta(b):
        p_b = rest_ref[b, :, 0:POOL_WIDTH]
        ext = jnp.concatenate([phist_ref[b].astype(BF16), p_b], axis=0)
        phist_ref[b] = p_b[n - HIST:, :].astype(F32)
        sums = jnp.dot(band_ref[...], ext, preferred_element_type=F32)
        win_sum = sums[0:n]
        for g in range(1, len(POOL_WINDOWS)):
            win_sum = jnp.where(pool_group == g, sums[g * n:(g + 1) * n], win_sum)
        return (win_sum / count - p_b.astype(F32)).astype(BF16)

    def front(b):
        xm_b = rest_ref[b, :, POOL_WIDTH:POOL_WIDTH + MLSTM_WIDTH]
        cext = jnp.concatenate([chist_ref[b].astype(BF16), xm_b], axis=0)
        chist_ref[b] = xm_b[n - HIST:, :].astype(F32)
        shifted = jnp.dot(shift_ref[...], cext, preferred_element_type=F32)
        conv = xm_b.astype(F32) * convw_ref[MLSTM_CONV - 1:MLSTM_CONV, :] + convb_ref[...]
        for back in range(1, MLSTM_CONV):
            tap = MLSTM_CONV - 1 - back
            conv = conv + shifted[(back - 1) * n:back * n] * convw_ref[tap:tap + 1, :]
        xc = conv * jax.nn.sigmoid(conv)
        return xc, xc.astype(BF16), xm_b

    def gate_rows(b):
        gates_t = (gate_ref[b] + gbias_ref[...]).T
        i_rows = gates_t[0:SUBLANES, :]
        return i_rows, _log_sigmoid(pltpu.roll(i_rows, SUBLANES - heads, 0))

    def gate_stats(b, i_rows, bcum):
        x_ib = i_rows - bcum
        b_last = bcum[:, n - 1:n]
        m_prev = mstate_ref[b, :, 0:1]
        a = b_last + x_ib
        m_new = jnp.maximum(b_last + m_prev, jnp.max(a, axis=1, keepdims=True))
        decay = jnp.exp(b_last + m_prev - m_new)
        w_state = jnp.exp(a - m_new)
        mstate_ref[b] = jnp.broadcast_to(m_new, mstate_ref.shape[1:])
        packed = jnp.concatenate(
            [jnp.where(row8 < heads, bcum, pltpu.roll(w_state, heads, 0)), bcum + m_prev,
             jnp.zeros((LANES - 2 * SUBLANES, n), F32)], axis=0)
        return x_ib, packed.T, decay

    def decay_log(x_ib, cols, h):
        log_d = jnp.where(causal, cols[:, h:h + 1] + x_ib[h:h + 1, :], NEG)
        return log_d, jnp.max(log_d, axis=-1, keepdims=True)

    def stabilisers(cols, row_maxes):
        intra_max = jnp.zeros((n, GATE_LANES), F32)
        for h, row_max in enumerate(row_maxes):
            intra_max = jnp.where(lane_g == 2 * heads + h, row_max, intra_max)
        m_all = jnp.maximum(cols, intra_max)
        return m_all, jnp.exp(cols - m_all), jnp.exp(-m_all)

    def head(b, h, qk, v, xc, cols, decay, log_d, m_all, inter_w_all, floor_all):
        hs = slice(h * dh, (h + 1) * dh)
        lane_m = 2 * heads + h
        q = (qk[:, :dh] * (dh ** -0.5)).astype(BF16)
        k = qk[:, dh:].astype(BF16)
        v_aug = jnp.concatenate([v, ones], axis=1)

        s = lax.dot_general(q, k, (((1,), (1,)), ((), ())), preferred_element_type=F32)
        sw = (s * jnp.exp(log_d - m_all[:, lane_m:lane_m + 1])).astype(BF16)
        c_prev = cstate_ref[b, h]
        out = inter_w_all[:, lane_m:lane_m + 1] * jnp.dot(q, c_prev.astype(BF16), preferred_element_type=F32) \
            + jnp.dot(sw, v_aug.astype(BF16), preferred_element_type=F32)
        hh = out[:, :dh] / jnp.maximum(jnp.abs(out[:, dh:]), floor_all[:, lane_m:lane_m + 1])

        wv_state = (cols[:, heads + h:heads + h + 1] * v_aug).astype(BF16)
        d_c = lax.dot_general(k, wv_state, (((0,), (0,)), ((), ())), preferred_element_type=F32)
        cstate_ref[b, h] = decay[h:h + 1, :] * c_prev + d_c

        og = rest_ref[b, :, POOL_WIDTH + MLSTM_WIDTH + h * dh:POOL_WIDTH + MLSTM_WIDTH + (h + 1) * dh].astype(F32)
        hh = hh * jax.nn.sigmoid(og)
        hh = _rms(hh) * normw_ref[:, hs]
        hm_ref[b, :, hs] = (hh + skip_ref[:, hs] * xc[:, hs]).astype(hm_ref.dtype)

    def rows_of(stacked, b, rows):
        return stacked[b * rows:(b + 1) * rows]

    raw = [gate_rows(b) for b in group]
    parts = [part for _, logf in raw for part in _split3(logf)]
    cums = jnp.dot(jnp.concatenate(parts, axis=0), upper_ref[...], preferred_element_type=F32)
    gated = []
    for b in group:
        piece = rows_of(cums, b, 3 * SUBLANES)
        bcum = piece[0:SUBLANES] + piece[SUBLANES:2 * SUBLANES] + piece[2 * SUBLANES:3 * SUBLANES]
        gated.append(gate_stats(b, raw[b][0], bcum))

    matmul_work = [lambda b=b: window_delta(b) for b in group] + [lambda b=b: front(b) for b in group]
    matmul_done = []
    decay = {}
    for b in group:
        x_ib, cols, _ = gated[b]
        for h in range(heads):
            decay[b, h] = decay_log(x_ib, cols, h)
            if matmul_work and h % 2 == 0:
                matmul_done.append(matmul_work.pop(0)())
    matmul_done += [work() for work in matmul_work]
    deltas, fronts = matmul_done[:len(group)], matmul_done[len(group):]
    stats = [stabilisers(gated[b][1], [decay[b, h][1] for h in range(heads)]) for b in group]

    pooled = jnp.dot(jnp.concatenate(deltas, axis=0), poolw_ref[...], preferred_element_type=F32) * pscale_ref[...]
    for b in group:
        pool_ref[b] = rows_of(pooled, b, n).astype(pool_ref.dtype)

    for h in range(heads):
        hs = slice(h * dh, (h + 1) * dh)
        qk_all = jnp.dot(jnp.concatenate([fronts[b][1][:, hs] for b in group], axis=0), wqk_ref[h],
                         preferred_element_type=F32)
        v_all = jnp.dot(jnp.concatenate([fronts[b][2][:, hs] for b in group], axis=0), wv_ref[h],
                        preferred_element_type=F32)
        for b in group:
            _, cols, state_decay = gated[b]
            m_all, inter_w_all, floor_all = stats[b]
            head(b, h, rows_of(qk_all, b, n), rows_of(v_all, b, n), fronts[b][0], cols, state_decay,
                 decay[b, h][0], m_all, inter_w_all, floor_all)


def _mix(rest, gates, pool_bd, pool_scale, conv_w, conv_b, w_qk, w_v, gate_bias, norm_w, skip, batch, seq):
    n = SEQ_CHUNK
    sg = SEQ_GROUP
    n_rest = rest.shape[-1]
    rest = rest.reshape(batch, seq, n_rest)
    gates = gates.reshape(batch, seq, GATE_LANES)
    c2 = lambda b, s: (0, 0)
    c3 = lambda b, s: (0, 0, 0)
    chunk = lambda b, s: (b, s, 0)
    return pl.pallas_call(
        _mix_kernel,
        out_shape=(
            jax.ShapeDtypeStruct((batch, seq, POOL_WIDTH), BF16),
            jax.ShapeDtypeStruct((batch, seq, MLSTM_WIDTH), BF16),
        ),
        grid=(batch // sg, seq // n),
        in_specs=[
            pl.BlockSpec((sg, n, n_rest), chunk),
            pl.BlockSpec((sg, n, GATE_LANES), chunk),
            pl.BlockSpec(pool_bd.shape, c2),
            pl.BlockSpec(pool_scale.shape, c2),
            pl.BlockSpec(conv_w.shape, c2),
            pl.BlockSpec(conv_b.shape, c2),
            pl.BlockSpec(w_qk.shape, c3),
            pl.BlockSpec(w_v.shape, c3),
            pl.BlockSpec(gate_bias.shape, c2),
            pl.BlockSpec(norm_w.shape, c2),
            pl.BlockSpec(skip.shape, c2),
        ],
        out_specs=(
            pl.BlockSpec((sg, n, POOL_WIDTH), chunk),
            pl.BlockSpec((sg, n, MLSTM_WIDTH), chunk),
        ),
        scratch_shapes=[
            pltpu.VMEM((sg, HIST, POOL_WIDTH), F32),
            pltpu.VMEM((sg, HIST, MLSTM_WIDTH), F32),
            pltpu.VMEM((sg, MLSTM_HEADS, MLSTM_HEAD_DIM, 2 * MLSTM_HEAD_DIM), F32),
            pltpu.VMEM((sg, SUBLANES, GATE_LANES), F32),
            pltpu.VMEM((len(POOL_WINDOWS) * n, HIST + n), BF16),
            pltpu.VMEM(((MLSTM_CONV - 1) * n, HIST + n), BF16),
            pltpu.VMEM((n, n), BF16),
        ],
        compiler_params=_params(("parallel", "arbitrary")),
        name="pool_mlstm",
    )(rest, gates, pool_bd, pool_scale, conv_w, conv_b, w_qk, w_v, gate_bias, norm_w, skip)


def kernel(x, c, ada_w, ada_b, pre_norm_w, post_norm_w, ffn_up, ffn_down, mix_in_w, mix_out_w, pool_w, pool_scale,
           mlstm_conv_w, mlstm_conv_b, mlstm_qkv_w, mlstm_gate_b, mlstm_norm_w, mlstm_skip):
    batch, seq, d = x.shape
    depth = ada_w.shape[0]
    assert d == ATT_WIDTH + POOL_WIDTH + MLSTM_WIDTH
    assert seq % ATT_TILE == 0 and seq % ROW_TILE == 0 and seq % SEQ_CHUNK == 0 and batch % SEQ_GROUP == 0

    ff = ffn_down.shape[2]
    up_b = ffn_up.astype(BF16).reshape(depth, 2, d, 2, ff // UP_TILE, UP_TILE).transpose(0, 1, 2, 4, 3, 5)
    up_b = up_b.reshape(depth, 2, d, 2 * ff)
    down_b = ffn_down.astype(BF16)
    out_b = mix_out_w.astype(BF16)
    n_main = QKV_W + POOL_WIDTH + 2 * MLSTM_WIDTH
    w_qkv = mix_in_w[:, :, :QKV_W].astype(BF16)
    w_rest = mix_in_w[:, :, QKV_W:n_main].astype(BF16)
    w_gate = jnp.pad(mix_in_w[:, :, n_main:], ((0, 0), (0, 0), (0, GATE_LANES - 2 * MLSTM_HEADS))).astype(BF16)
    gate_bias = jnp.pad(mlstm_gate_b.reshape(depth, 1, 2 * MLSTM_HEADS),
                        ((0, 0), (0, 0), (0, GATE_LANES - 2 * MLSTM_HEADS)))
    groups = len(POOL_WINDOWS)
    eye = jnp.eye(groups, dtype=pool_w.dtype)
    pool_bd = (pool_w[:, :, :, None, :] * eye[None, :, None, :, None]).reshape(depth, POOL_WIDTH, POOL_WIDTH).astype(BF16)
    w_qk = jnp.concatenate([mlstm_qkv_w[:, 0], mlstm_qkv_w[:, 1]], axis=-1).astype(BF16)
    w_v = mlstm_qkv_w[:, 2].astype(BF16)

    mod = _adaln(c, ada_w, ada_b).reshape(depth, batch, 9, d)

    h = x.reshape(batch * seq, d)
    for l in range(depth):
        h = _ffn(h, mod[l], pre_norm_w[l], post_norm_w[l], up_b, down_b, l, 0, seq)
        *qkv, rest, gates = _inproj(h, mod[l], pre_norm_w[l], w_qkv, w_rest, w_gate, l, batch, seq)
        att = _attention(qkv, batch, seq)
        pool, hm = _mix(rest, gates, pool_bd[l], pool_scale[l].reshape(1, -1), mlstm_conv_w[l],
                        mlstm_conv_b[l].reshape(1, -1), w_qk[l], w_v[l], gate_bias[l],
                        mlstm_norm_w[l].reshape(1, -1), mlstm_skip[l].reshape(1, -1), batch, seq)
        h = _outproj(h, att, pool, hm, mod[l], post_norm_w[l], out_b, l, seq)
        h = _ffn(h, mod[l], pre_norm_w[l], post_norm_w[l], up_b, down_b, l, 1, seq)
    return h.reshape(batch, seq, d)
```

```python
import functools

import jax
import jax.numpy as jnp
from jax import lax
from jax.experimental import pallas as pl
from jax.experimental.pallas import tpu as pltpu

F32 = jnp.float32
BF16 = jnp.bfloat16

ATT_HEADS = 4
ATT_HEAD_DIM = 64
ATT_WIDTH = ATT_HEADS * ATT_HEAD_DIM
DILATIONS = (1, 4, 16)
ATT_SPAN = 128
POOL_WINDOWS = (2, 4, 8, 16)
POOL_GROUP_DIM = 64
POOL_WIDTH = len(POOL_WINDOWS) * POOL_GROUP_DIM
MLSTM_HEADS = 4
MLSTM_HEAD_DIM = 128
MLSTM_WIDTH = MLSTM_HEADS * MLSTM_HEAD_DIM
MLSTM_CONV = 4
MACARON_WEIGHT = 0.5
EPS = 1e-6
NEG = -1e30

LANES = 128
SUBLANES = 8
VMEM_LIMIT_BYTES = 56 * 1024 * 1024

ROW_TILE = 1024
ROW_PARTS = 2
UP_TILE = 256
ATT_TILE = 2048
ATT_BLOCK = 128
SEQ_CHUNK = 256
SEQ_GROUP = 4
HIST = 16
GATE_LANES = LANES


def _params(sem, **flags):
    return pltpu.CompilerParams(dimension_semantics=sem, vmem_limit_bytes=VMEM_LIMIT_BYTES, flags=flags or None)


def _rms(x):
    return x * lax.rsqrt(jnp.mean(x * x, axis=-1, keepdims=True) + EPS)


def _adaln_kernel(c_ref, w_ref, b_ref, o_ref):
    c = c_ref[...]
    c_act = (c * jax.nn.sigmoid(c)).astype(BF16)
    o_ref[...] = jnp.dot(c_act, w_ref[...].astype(BF16), preferred_element_type=F32) + b_ref[...]


def _adaln(c, ada_w, ada_b):
    depth, d, n = ada_w.shape
    b = c.shape[0]
    tn = n // 8
    return pl.pallas_call(
        _adaln_kernel,
        out_shape=jax.ShapeDtypeStruct((depth, b, n), F32),
        grid=(depth, n // tn),
        in_specs=[
            pl.BlockSpec((b, d), lambda l, j: (0, 0)),
            pl.BlockSpec((None, d, tn), lambda l, j: (l, 0, j)),
            pl.BlockSpec((None, 1, tn), lambda l, j: (l, 0, j)),
        ],
        out_specs=pl.BlockSpec((None, b, tn), lambda l, j: (l, 0, j)),
        compiler_params=_params(("parallel", "parallel")),
        name="adaln",
    )(c, ada_w, ada_b.reshape(depth, 1, n))


def _modulated(x, mod_ref, prew_ref, sub):
    scale = prew_ref[sub:sub + 1, :] * (1.0 + mod_ref[3 * sub + 1:3 * sub + 2, :])
    return (_rms(x) * scale + mod_ref[3 * sub:3 * sub + 1, :]).astype(BF16)


def _part(ref, part):
    rows = ref.shape[0] // ROW_PARTS
    return slice(part * rows, (part + 1) * rows)


def _ffn_kernel(x_ref, mod_ref, prew_ref, postw_ref, wup_ref, wdown_ref, o_ref, *, sub):
    gate = postw_ref[sub:sub + 1, :] * (MACARON_WEIGHT * mod_ref[3 * sub + 2:3 * sub + 3, :])
    u = _modulated(x_ref[_part(x_ref, 0), :], mod_ref, prew_ref, sub)
    for part in range(ROW_PARTS):
        rows = _part(x_ref, part)
        ff = wdown_ref.shape[0]
        acts = []
        for c in range(ff // UP_TILE):
            g = jnp.dot(u, wup_ref[:, c * UP_TILE:(c + 1) * UP_TILE], preferred_element_type=F32)
            v = jnp.dot(u, wup_ref[:, ff + c * UP_TILE:ff + (c + 1) * UP_TILE], preferred_element_type=F32)
            acts.append((g * jax.nn.sigmoid(g) * v).astype(BF16))
        if part + 1 < ROW_PARTS:
            u = _modulated(x_ref[_part(x_ref, part + 1), :], mod_ref, prew_ref, sub)
        a = jnp.concatenate(acts, axis=1)
        y = jnp.dot(a, wdown_ref[...], preferred_element_type=F32)
        o_ref[rows, :] = x_ref[rows, :] + _rms(y) * gate


def _inproj_kernel(x_ref, mod_ref, prew_ref, wqkv_ref, wrest_ref, wgate_ref,
                   qkv1_ref, qkv4_ref, qkv16_ref, rest_ref, gate_ref, z_ref):
    slabs = z_ref.shape[0]

    def by_residue(part):
        rows = _part(x_ref, part)
        n = rows.stop - rows.start
        for dil, out_ref in ((DILATIONS[1], qkv4_ref), (DILATIONS[2], qkv16_ref)):
            per = n // dil
            for r in range(dil):
                for c in range(slabs):
                    out_ref[r, part * per:(part + 1) * per, c * LANES:(c + 1) * LANES] = (
                        z_ref[c, pl.ds(rows.start + r, per, stride=dil), :].astype(BF16))

    u = _modulated(x_ref[_part(x_ref, 0), :], mod_ref, prew_ref, 1)
    for part in range(ROW_PARTS):
        rows = _part(x_ref, part)
        z = jnp.dot(u, wqkv_ref[...], preferred_element_type=F32)
        qkv1_ref[0, rows, :] = z.astype(BF16)
        for c in range(slabs):
            z_ref[c, rows, :] = z[:, c * LANES:(c + 1) * LANES]
        rest_ref[rows, :] = jnp.dot(u, wrest_ref[...], preferred_element_type=F32).astype(BF16)
        gate_ref[rows, :] = jnp.dot(u, wgate_ref[...], preferred_element_type=F32)
        if part + 1 < ROW_PARTS:
            u = _modulated(x_ref[_part(x_ref, part + 1), :], mod_ref, prew_ref, 1)
        by_residue(part)


def _outproj_kernel(x_ref, att_ref, pool_ref, hm_ref, mod_ref, postw_ref, wa_ref, wp_ref, wh_ref, o_ref):
    gate = postw_ref[1:2, :] * mod_ref[5:6, :]
    ys = []
    for part in range(ROW_PARTS):
        rows = _part(x_ref, part)
        y = jnp.dot(att_ref[rows, :], wa_ref[...], preferred_element_type=F32)
        y = y + jnp.dot(pool_ref[rows, :], wp_ref[...], preferred_element_type=F32)
        ys.append(y + jnp.dot(hm_ref[rows, :], wh_ref[...], preferred_element_type=F32))
    for part in range(ROW_PARTS):
        rows = _part(x_ref, part)
        o_ref[rows, :] = x_ref[rows, :] + _rms(ys[part]) * gate


def _resident(shape, index):
    return pl.BlockSpec(shape, index, pipeline_mode=pl.Buffered(1))


def _row_specs(d, seq):
    tm = ROW_TILE
    per_batch = seq // tm
    return (pl.BlockSpec((tm, d), lambda i: (i, 0)),
            pl.BlockSpec((None, 9, d), lambda i: (i // per_batch, 0, 0)),
            pl.BlockSpec((3, d), lambda i: (0, 0)))


def _ffn(h, mod, pre_w, post_w, w_up, w_down, layer, slot, seq):
    t, d = h.shape
    ff = w_down.shape[2]
    rows, mods, gains = _row_specs(d, seq)
    return pl.pallas_call(
        functools.partial(_ffn_kernel, sub=2 * slot),
        out_shape=jax.ShapeDtypeStruct((t, d), F32),
        grid=(t // ROW_TILE,),
        in_specs=[
            rows, mods, gains, gains,
            _resident((None, None, d, 2 * ff), lambda i: (layer, slot, 0, 0)),
            _resident((None, None, ff, d), lambda i: (layer, slot, 0, 0)),
        ],
        out_specs=rows,
        compiler_params=_params(("parallel",)),
        name="ffn",
    )(h, mod, pre_w, post_w, w_up, w_down)


def _inproj(h, mod, pre_w, w_qkv, w_rest, w_gate, layer, batch, seq):
    t, d = h.shape
    tm = ROW_TILE
    per_batch = seq // tm
    n_qkv, n_rest, n_gate = w_qkv.shape[2], w_rest.shape[2], w_gate.shape[2]
    rows, mods, gains = _row_specs(d, seq)
    by_residue = lambda i: (i // per_batch, 0, i % per_batch, 0)
    qkv_shapes = [jax.ShapeDtypeStruct((batch, dil, seq // dil, n_qkv), BF16) for dil in DILATIONS]
    qkv_specs = [pl.BlockSpec((None, dil, tm // dil, n_qkv), by_residue) for dil in DILATIONS]
    return pl.pallas_call(
        _inproj_kernel,
        out_shape=(
            *qkv_shapes,
            jax.ShapeDtypeStruct((t, n_rest), BF16),
            jax.ShapeDtypeStruct((t, n_gate), F32),
        ),
        grid=(t // tm,),
        in_specs=[
            rows, mods, gains,
            _resident((None, d, n_qkv), lambda i: (layer, 0, 0)),
            _resident((None, d, n_rest), lambda i: (layer, 0, 0)),
            _resident((None, d, n_gate), lambda i: (layer, 0, 0)),
        ],
        out_specs=(
            *qkv_specs,
            pl.BlockSpec((tm, n_rest), lambda i: (i, 0)),
            pl.BlockSpec((tm, n_gate), lambda i: (i, 0)),
        ),
        scratch_shapes=[pltpu.VMEM((n_qkv // LANES, tm, LANES), F32)],
        compiler_params=_params(("parallel",)),
        name="inproj",
    )(h, mod, pre_w, w_qkv, w_rest, w_gate)


def _outproj(h, att, pool, hm, mod, post_w, w_out, layer, seq):
    t, d = h.shape
    tm = ROW_TILE
    rows, mods, gains = _row_specs(d, seq)
    widths = (att.shape[-1], pool.shape[-1], hm.shape[-1])
    w_specs = []
    off = 0
    for wdt in widths:
        w_specs.append(_resident((None, wdt, d), functools.partial(lambda i, blk: (layer, blk, 0), blk=off // wdt)))
        off += wdt
    return pl.pallas_call(
        _outproj_kernel,
        out_shape=jax.ShapeDtypeStruct((t, d), F32),
        grid=(t // tm,),
        in_specs=[
            rows,
            pl.BlockSpec((tm, widths[0]), lambda i: (i, 0)),
            pl.BlockSpec((tm, widths[1]), lambda i: (i, 0)),
            pl.BlockSpec((tm, widths[2]), lambda i: (i, 0)),
            mods, gains,
            *w_specs,
        ],
        out_specs=rows,
        compiler_params=_params(("parallel",)),
        name="outproj",
    )(h, att.reshape(t, widths[0]), pool.reshape(t, widths[1]), hm.reshape(t, widths[2]), mod, post_w,
      w_out, w_out, w_out)


QKV_W = 3 * ATT_WIDTH


def _attn_scores(q, kwin, bias, head_of_lane):
    q32 = q.astype(F32) * (ATT_HEAD_DIM ** -0.5)
    lhs = jnp.concatenate(
        [jnp.where(head_of_lane == h, q32, 0.0) for h in range(ATT_HEADS)], axis=0).astype(BF16)
    return lax.dot_general(lhs, kwin, (((1,), (1,)), ((), ())), preferred_element_type=F32) + bias


def _by_lane_half(x, low):
    nq = x.shape[0] // ATT_HEADS
    out = []
    for half in range(2):
        first = x[(2 * half) * nq:(2 * half + 1) * nq]
        second = x[(2 * half + 1) * nq:(2 * half + 2) * nq]
        if x.shape[1] != 1:
            first = first[:, half * LANES:(half + 1) * LANES]
            second = second[:, half * LANES:(half + 1) * LANES]
        out.append(jnp.where(low, first, second))
    return out


def _attn_kernel(c1_ref, h1_ref, c4_ref, h4_ref, c16_ref, h16_ref, o_ref, bias_ref, dst_ref, s_ref, p_ref):
    t = pl.program_id(1)
    nq = ATT_BLOCK
    items = ATT_TILE // nq
    head_of_lane = lax.broadcasted_iota(jnp.int32, (nq, ATT_WIDTH), 1) >> (ATT_HEAD_DIM.bit_length() - 1)
    low = lax.broadcasted_iota(jnp.int32, (nq, LANES), 1) < ATT_HEAD_DIM

    qi = lax.broadcasted_iota(jnp.int32, (ATT_HEADS * nq, 2 * nq), 0) & (nq - 1)
    ki = lax.broadcasted_iota(jnp.int32, (ATT_HEADS * nq, 2 * nq), 1)
    ok = (ki >= qi) & (ki <= qi + ATT_SPAN)
    bias_ref[0] = jnp.where(ok, 0.0, NEG)
    bias_ref[1] = jnp.where(ok & (ki >= nq), 0.0, NEG)
    qs = slice(0, ATT_WIDTH)
    ks = slice(ATT_WIDTH, 2 * ATT_WIDTH)
    vs = slice(2 * ATT_WIDTH, 3 * ATT_WIDTH)

    branches = tuple(zip((c1_ref, c4_ref, c16_ref), (h1_ref, h4_ref, h16_ref), DILATIONS))

    def split(dil, idx):
        blocks = items // dil
        return idx >> (blocks.bit_length() - 1), idx & (blocks - 1)

    def window(cur_ref, halo_ref, dil, idx, cols):
        r, i = split(dil, idx)
        own = cur_ref[r, pl.ds(pl.multiple_of(i * nq, nq), nq), cols]
        if items == dil:
            prev = halo_ref[r, :, cols]
        else:
            prev0 = pl.multiple_of(jnp.maximum(i - 1, 0) * nq, nq)
            prev = jnp.where(i == 0, halo_ref[r, :, cols], cur_ref[r, pl.ds(prev0, nq), cols])
        return jnp.concatenate([prev, own], axis=0)

    def put(branch, quantity, halves, dil, idx):
        r, i = split(dil, idx)
        start = i * (nq * dil) + r
        rows = pl.ds(start, nq) if dil == 1 else pl.ds(start, nq, stride=dil)
        for half, piece in enumerate(halves):
            dst_ref[branch, quantity, half, rows, :] = piece

    for branch in range(len(branches)):
        s_ref[branch, 1] = jnp.zeros(s_ref.shape[2:], s_ref.dtype)
        p_ref[branch, 1] = jnp.zeros(p_ref.shape[2:], p_ref.dtype)

    def trip(j, slot):
        other = 1 - slot
        item_a = jnp.minimum(j, items - 1)
        item_b = jnp.clip(j - 1, 0, items - 1)
        item_c = jnp.clip(j - 2, 0, items - 1)

        for branch, (cur_ref, halo_ref, dil) in enumerate(branches):
            pv = jnp.dot(p_ref[branch, other], window(cur_ref, halo_ref, dil, item_c, vs),
                         preferred_element_type=F32)
            put(branch, 0, _by_lane_half(pv, low), dil, item_c)

        for branch, (cur_ref, halo_ref, dil) in enumerate(branches):
            s = s_ref[branch, other]
            m = jnp.max(s, axis=-1, keepdims=True)
            p = jnp.exp(s - m)
            l = jnp.sum(p, axis=-1, keepdims=True)
            p_ref[branch, slot] = p.astype(BF16)
            put(branch, 1, _by_lane_half(m, low), dil, item_b)
            put(branch, 2, _by_lane_half(l, low), dil, item_b)

        for branch, (cur_ref, halo_ref, dil) in enumerate(branches):
            r, i = split(dil, item_a)
            no_past = jnp.where((i == 0) & (t == 0), 1, 0)
            q = cur_ref[r, pl.ds(pl.multiple_of(i * nq, nq), nq), qs]
            s_ref[branch, slot] = _attn_scores(q, window(cur_ref, halo_ref, dil, item_a, ks),
                                               bias_ref[no_past], head_of_lane)

    def body(pair, carry):
        trip(2 * pair, 0)
        trip(2 * pair + 1, 1)
        return carry
    lax.fori_loop(0, (items + 2) // 2, body, 0)

    rows = ATT_BLOCK

    def merge(c, carry):
        sl = pl.ds(pl.multiple_of(c * rows, rows), rows)
        for half in range(2):
            maxes = [dst_ref[branch, 1, half, sl, :] for branch in range(len(branches))]
            top = functools.reduce(jnp.maximum, maxes)
            num = 0.0
            den = 0.0
            for branch, m_b in enumerate(maxes):
                e = jnp.exp(m_b - top)
                num = num + e * dst_ref[branch, 0, half, sl, :]
                den = den + e * dst_ref[branch, 2, half, sl, :]
            o_ref[sl, half * LANES:(half + 1) * LANES] = (num / den).astype(o_ref.dtype)
        return carry
    lax.fori_loop(0, ATT_TILE // rows, merge, 0)


def _attention(qkv_by_dilation, batch, seq):
    tiles = seq // ATT_TILE
    in_specs = []
    operands = []
    for dil, view in zip(DILATIONS, qkv_by_dilation):
        cur_rows = ATT_TILE // dil
        per_tile = cur_rows // ATT_BLOCK
        in_specs.append(pl.BlockSpec((None, dil, cur_rows, QKV_W), lambda b, t: (b, 0, t, 0)))
        in_specs.append(pl.BlockSpec(
            (None, dil, ATT_BLOCK, QKV_W),
            functools.partial(lambda b, t, per_tile: (b, 0, jnp.maximum(t * per_tile - 1, 0), 0), per_tile=per_tile)))
        operands += [view, view]
    stacked = (ATT_HEADS * ATT_BLOCK, 2 * ATT_BLOCK)
    return pl.pallas_call(
        _attn_kernel,
        out_shape=jax.ShapeDtypeStruct((batch, seq, ATT_WIDTH), BF16),
        grid=(batch, tiles),
        in_specs=in_specs,
        out_specs=pl.BlockSpec((None, ATT_TILE, ATT_WIDTH), lambda b, t: (b, t, 0)),
        scratch_shapes=[
            pltpu.VMEM((2,) + stacked, F32),
            pltpu.VMEM((len(DILATIONS), 3, 2, ATT_TILE, LANES), F32),
            pltpu.VMEM((len(DILATIONS), 2) + stacked, F32),
            pltpu.VMEM((len(DILATIONS), 2) + stacked, BF16),
        ],
        compiler_params=_params(("parallel", "parallel")),
        name="dilated_attention",
    )(*operands)


def _split3(x):
    hi = x.astype(BF16)
    r1 = x - hi.astype(F32)
    mid = r1.astype(BF16)
    lo = (r1 - mid.astype(F32)).astype(BF16)
    return hi, mid, lo


def _log_sigmoid(x):
    return -(jnp.maximum(-x, 0.0) + jnp.log1p(jnp.exp(-jnp.abs(x))))


def _mix_kernel(rest_ref, gate_ref, poolw_ref, pscale_ref, convw_ref, convb_ref, wqk_ref, wv_ref,
                gbias_ref, normw_ref, skip_ref, pool_ref, hm_ref,
                phist_ref, chist_ref, cstate_ref, mstate_ref, band_ref, shift_ref, upper_ref):
    s_idx = pl.program_id(1)
    n = SEQ_CHUNK
    dh = MLSTM_HEAD_DIM
    heads = MLSTM_HEADS
    group = range(SEQ_GROUP)

    @pl.when(s_idx == 0)
    def _():
        phist_ref[...] = jnp.zeros_like(phist_ref)
        chist_ref[...] = jnp.zeros_like(chist_ref)
        cstate_ref[...] = jnp.zeros_like(cstate_ref)
        mstate_ref[...] = jnp.zeros_like(mstate_ref)
        t_row = lax.broadcasted_iota(jnp.int32, (n, HIST + n), 0) + HIST
        col = lax.broadcasted_iota(jnp.int32, (n, HIST + n), 1)
        for g, wlen in enumerate(POOL_WINDOWS):
            band_ref[g * n:(g + 1) * n, :] = jnp.where((col <= t_row) & (col > t_row - wlen), 1.0, 0.0).astype(BF16)
        for back in range(1, MLSTM_CONV):
            shift_ref[(back - 1) * n:back * n, :] = jnp.where(col == t_row - back, 1.0, 0.0).astype(BF16)
        src = lax.broadcasted_iota(jnp.int32, (n, n), 0)
        dst = lax.broadcasted_iota(jnp.int32, (n, n), 1)
        upper_ref[...] = jnp.where(src <= dst, 1.0, 0.0).astype(BF16)

    lane = lax.broadcasted_iota(jnp.int32, (n, POOL_WIDTH), 1)
    pool_group = lane >> (POOL_GROUP_DIM.bit_length() - 1)
    pos = lax.broadcasted_iota(jnp.int32, (n, POOL_WIDTH), 0) + s_idx * n
    win_len = jnp.full((n, POOL_WIDTH), POOL_WINDOWS[0], jnp.int32)
    for g, wlen in enumerate(POOL_WINDOWS):
        win_len = jnp.where(pool_group == g, wlen, win_len)
    count = jnp.minimum(pos + 1, win_len).astype(F32)
    row = lax.broadcasted_iota(jnp.int32, (n, n), 0)
    col = lax.broadcasted_iota(jnp.int32, (n, n), 1)
    causal = col <= row
    lane_g = lax.broadcasted_iota(jnp.int32, (n, GATE_LANES), 1)
    row8 = lax.broadcasted_iota(jnp.int32, (SUBLANES, n), 0)
    ones = jnp.ones((n, dh), F32)

    def window_delta(b):
        p_b = rest_ref[b, :, 0:POOL_WIDTH]
        ext = jnp.concatenate([phist_ref[b].astype(BF16), p_b], axis=0)
        phist_ref[b] = p_b[n - HIST:, :].astype(F32)
        sums = jnp.dot(band_ref[...], ext, preferred_element_type=F32)
        win_sum = sums[0:n]
        for g in range(1, len(POOL_WINDOWS)):
            win_sum = jnp.where(pool_group == g, sums[g * n:(g + 1) * n], win_sum)
        return (win_sum / count - p_b.astype(F32)).astype(BF16)

    def front(b):
        xm_b = rest_ref[b, :, POOL_WIDTH:POOL_WIDTH + MLSTM_WIDTH]
        cext = jnp.concatenate([chist_ref[b].astype(BF16), xm_b], axis=0)
        chist_ref[b] = xm_b[n - HIST:, :].astype(F32)
        shifted = jnp.dot(shift_ref[...], cext, preferred_element_type=F32)
        conv = xm_b.astype(F32) * convw_ref[MLSTM_CONV - 1:MLSTM_CONV, :] + convb_ref[...]
        for back in range(1, MLSTM_CONV):
            tap = MLSTM_CONV - 1 - back
            conv = conv + shifted[(back - 1) * n:back * n] * convw_ref[tap:tap + 1, :]
        xc = conv * jax.nn.sigmoid(conv)
        return xc, xc.astype(BF16), xm_b

    def gate_rows(b):
        gates_t = (gate_ref[b] + gbias_ref[...]).T
        i_rows = gates_t[0:SUBLANES, :]
        return i_rows, _log_sigmoid(pltpu.roll(i_rows, SUBLANES - heads, 0))

    def gate_stats(b, i_rows, bcum):
        x_ib = i_rows - bcum
        b_last = bcum[:, n - 1:n]
        m_prev = mstate_ref[b, :, 0:1]
        a = b_last + x_ib
        m_new = jnp.maximum(b_last + m_prev, jnp.max(a, axis=1, keepdims=True))
        decay = jnp.exp(b_last + m_prev - m_new)
        w_state = jnp.exp(a - m_new)
        mstate_ref[b] = jnp.broadcast_to(m_new, mstate_ref.shape[1:])
        packed = jnp.concatenate(
            [jnp.where(row8 < heads, bcum, pltpu.roll(w_state, heads, 0)), bcum + m_prev,
             jnp.zeros((LANES - 2 * SUBLANES, n), F32)], axis=0)
        return x_ib, packed.T, decay

    def decay_log(x_ib, cols, h):
        log_d = jnp.where(causal, cols[:, h:h + 1] + x_ib[h:h + 1, :], NEG)
        return log_d, jnp.max(log_d, axis=-1, keepdims=True)

    def stabilisers(cols, row_maxes):
        intra_max = jnp.zeros((n, GATE_LANES), F32)
        for h, row_max in enumerate(row_maxes):
            intra_max = jnp.where(lane_g == 2 * heads + h, row_max, intra_max)
        m_all = jnp.maximum(cols, intra_max)
        return m_all, jnp.exp(cols - m_all), jnp.exp(-m_all)

    def head(b, h, qk, v, xc, cols, decay, log_d, m_all, inter_w_all, floor_all):
        hs = slice(h * dh, (h + 1) * dh)
        lane_m = 2 * heads + h
        q = (qk[:, :dh] * (dh ** -0.5)).astype(BF16)
        k = qk[:, dh:].astype(BF16)
        v_aug = jnp.concatenate([v, ones], axis=1)

        s = lax.dot_general(q, k, (((1,), (1,)), ((), ())), preferred_element_type=F32)
        sw = (s * jnp.exp(log_d - m_all[:, lane_m:lane_m + 1])).astype(BF16)
        c_prev = cstate_ref[b, h]
        out = inter_w_all[:, lane_m:lane_m + 1] * jnp.dot(q, c_prev.astype(BF16), preferred_element_type=F32) \
            + jnp.dot(sw, v_aug.astype(BF16), preferred_element_type=F32)
        hh = out[:, :dh] / jnp.maximum(jnp.abs(out[:, dh:]), floor_all[:, lane_m:lane_m + 1])

        wv_state = (cols[:, heads + h:heads + h + 1] * v_aug).astype(BF16)
        d_c = lax.dot_general(k, wv_state, (((0,), (0,)), ((), ())), preferred_element_type=F32)
        cstate_ref[b, h] = decay[h:h + 1, :] * c_prev + d_c

        og = rest_ref[b, :, POOL_WIDTH + MLSTM_WIDTH + h * dh:POOL_WIDTH + MLSTM_WIDTH + (h + 1) * dh].astype(F32)
        hh = hh * jax.nn.sigmoid(og)
        hh = _rms(hh) * normw_ref[:, hs]
        hm_ref[b, :, hs] = (hh + skip_ref[:, hs] * xc[:, hs]).astype(hm_ref.dtype)

    def rows_of(stacked, b, rows):
        return stacked[b * rows:(b + 1) * rows]

    raw = [gate_rows(b) for b in group]
    parts = [part for _, logf in raw for part in _split3(logf)]
    cums = jnp.dot(jnp.concatenate(parts, axis=0), upper_ref[...], preferred_element_type=F32)
    gated = []
    for b in group:
        piece = rows_of(cums, b, 3 * SUBLANES)
        bcum = piece[0:SUBLANES] + piece[SUBLANES:2 * SUBLANES] + piece[2 * SUBLANES:3 * SUBLANES]
        gated.append(gate_stats(b, raw[b][0], bcum))

    matmul_work = [lambda b=b: window_delta(b) for b in group] + [lambda b=b: front(b) for b in group]
    matmul_done = []
    decay = {}
    for b in group:
        x_ib, cols, _ = gated[b]
        for h in range(heads):
            decay[b, h] = decay_log(x_ib, cols, h)
            if matmul_work and h % 2 == 0:
                matmul_done.append(matmul_work.pop(0)())
    matmul_done += [work() for work in matmul_work]
    deltas, fronts = matmul_done[:len(group)], matmul_done[len(group):]
    stats = [stabilisers(gated[b][1], [decay[b, h][1] for h in range(heads)]) for b in group]

    pooled = jnp.dot(jnp.concatenate(deltas, axis=0), poolw_ref[...], preferred_element_type=F32) * pscale_ref[...]
    for b in group:
        pool_ref[b] = rows_of(pooled, b, n).astype(pool_ref.dtype)

    for h in range(heads):
        hs = slice(h * dh, (h + 1) * dh)
        qk_all = jnp.dot(jnp.concatenate([fronts[b][1][:, hs] for b in group], axis=0), wqk_ref[h],
                         preferred_element_type=F32)
        v_all = jnp.dot(jnp.concatenate([fronts[b][2][:, hs] for b in group], axis=0), wv_ref[h],
                        preferred_element_type=F32)
        for b in group:
            _, cols, state_decay = gated[b]
            m_all, inter_w_all, floor_all = stats[b]
            head(b, h, rows_of(qk_all, b, n), rows_of(v_all, b, n), fronts[b][0], cols, state_decay,
                 decay[b, h][0], m_all, inter_w_all, floor_all)


def _mix(rest, gates, pool_bd, pool_scale, conv_w, conv_b, w_qk, w_v, gate_bias, norm_w, skip, batch, seq):
    n = SEQ_CHUNK
    sg = SEQ_GROUP
    n_rest = rest.shape[-1]
    rest = rest.reshape(batch, seq, n_rest)
    gates = gates.reshape(batch, seq, GATE_LANES)
    c2 = lambda b, s: (0, 0)
    c3 = lambda b, s: (0, 0, 0)
    chunk = lambda b, s: (b, s, 0)
    return pl.pallas_call(
        _mix_kernel,
        out_shape=(
            jax.ShapeDtypeStruct((batch, seq, POOL_WIDTH), BF16),
            jax.ShapeDtypeStruct((batch, seq, MLSTM_WIDTH), BF16),
        ),
        grid=(batch // sg, seq // n),
        in_specs=[
            pl.BlockSpec((sg, n, n_rest), chunk),
            pl.BlockSpec((sg, n, GATE_LANES), chunk),
            pl.BlockSpec(pool_bd.shape, c2),
            pl.BlockSpec(pool_scale.shape, c2),
            pl.BlockSpec(conv_w.shape, c2),
            pl.BlockSpec(conv_b.shape, c2),
            pl.BlockSpec(w_qk.shape, c3),
            pl.BlockSpec(w_v.shape, c3),
            pl.BlockSpec(gate_bias.shape, c2),
            pl.BlockSpec(norm_w.shape, c2),
            pl.BlockSpec(skip.shape, c2),
        ],
        out_specs=(
            pl.BlockSpec((sg, n, POOL_WIDTH), chunk),
            pl.BlockSpec((sg, n, MLSTM_WIDTH), chunk),
        ),
        scratch_shapes=[
            pltpu.VMEM((sg, HIST, POOL_WIDTH), F32),
            pltpu.VMEM((sg, HIST, MLSTM_WIDTH), F32),
            pltpu.VMEM((sg, MLSTM_HEADS, MLSTM_HEAD_DIM, 2 * MLSTM_HEAD_DIM), F32),
            pltpu.VMEM((sg, SUBLANES, GATE_LANES), F32),
            pltpu.VMEM((len(POOL_WINDOWS) * n, HIST + n), BF16),
            pltpu.VMEM(((MLSTM_CONV - 1) * n, HIST + n), BF16),
            pltpu.VMEM((n, n), BF16),
        ],
        compiler_params=_params(("parallel", "arbitrary")),
        name="pool_mlstm",
    )(rest, gates, pool_bd, pool_scale, conv_w, conv_b, w_qk, w_v, gate_bias, norm_w, skip)


def kernel(x, c, ada_w, ada_b, pre_norm_w, post_norm_w, ffn_up, ffn_down, mix_in_w, mix_out_w, pool_w, pool_scale,
           mlstm_conv_w, mlstm_conv_b, mlstm_qkv_w, mlstm_gate_b, mlstm_norm_w, mlstm_skip):
    batch, seq, d = x.shape
    depth = ada_w.shape[0]
    assert d == ATT_WIDTH + POOL_WIDTH + MLSTM_WIDTH
    assert seq % ATT_TILE == 0 and seq % ROW_TILE == 0 and seq % SEQ_CHUNK == 0 and batch % SEQ_GROUP == 0

    up_b = ffn_up.astype(BF16)
    down_b = ffn_down.astype(BF16)
    out_b = mix_out_w.astype(BF16)
    n_main = QKV_W + POOL_WIDTH + 2 * MLSTM_WIDTH
    w_qkv = mix_in_w[:, :, :QKV_W].astype(BF16)
    w_rest = mix_in_w[:, :, QKV_W:n_main].astype(BF16)
    w_gate = jnp.pad(mix_in_w[:, :, n_main:], ((0, 0), (0, 0), (0, GATE_LANES - 2 * MLSTM_HEADS))).astype(BF16)
    gate_bias = jnp.pad(mlstm_gate_b.reshape(depth, 1, 2 * MLSTM_HEADS),
                        ((0, 0), (0, 0), (0, GATE_LANES - 2 * MLSTM_HEADS)))
    groups = len(POOL_WINDOWS)
    eye = jnp.eye(groups, dtype=pool_w.dtype)
    pool_bd = (pool_w[:, :, :, None, :] * eye[None, :, None, :, None]).reshape(depth, POOL_WIDTH, POOL_WIDTH).astype(BF16)
    w_qk = jnp.concatenate([mlstm_qkv_w[:, 0], mlstm_qkv_w[:, 1]], axis=-1).astype(BF16)
    w_v = mlstm_qkv_w[:, 2].astype(BF16)

    mod = _adaln(c, ada_w, ada_b).reshape(depth, batch, 9, d)

    h = x.reshape(batch * seq, d)
    for l in range(depth):
        h = _ffn(h, mod[l], pre_norm_w[l], post_norm_w[l], up_b, down_b, l, 0, seq)
        *qkv, rest, gates = _inproj(h, mod[l], pre_norm_w[l], w_qkv, w_rest, w_gate, l, batch, seq)
        att = _attention(qkv, batch, seq)
        pool, hm = _mix(rest, gates, pool_bd[l], pool_scale[l].reshape(1, -1), mlstm_conv_w[l],
                        mlstm_conv_b[l].reshape(1, -1), w_qk[l], w_v[l], gate_bias[l],
                        mlstm_norm_w[l].reshape(1, -1), mlstm_skip[l].reshape(1, -1), batch, seq)
        h = _outproj(h, att, pool, hm, mod[l], post_norm_w[l], out_b, l, seq)
        h = _ffn(h, mod[l], pre_norm_w[l], post_norm_w[l], up_b, down_b, l, 1, seq)
    return h.reshape(batch, seq, d)
```

```python
import functools

import jax
import jax.numpy as jnp
from jax import lax
from jax.experimental import pallas as pl
from jax.experimental.pallas import tpu as pltpu

F32 = jnp.float32
BF16 = jnp.bfloat16

ATT_HEADS = 4
ATT_HEAD_DIM = 64
ATT_WIDTH = ATT_HEADS * ATT_HEAD_DIM
DILATIONS = (1, 4, 16)
ATT_SPAN = 128
POOL_WINDOWS = (2, 4, 8, 16)
POOL_GROUP_DIM = 64
POOL_WIDTH = len(POOL_WINDOWS) * POOL_GROUP_DIM
MLSTM_HEADS = 4
MLSTM_HEAD_DIM = 128
MLSTM_WIDTH = MLSTM_HEADS * MLSTM_HEAD_DIM
MLSTM_CONV = 4
MACARON_WEIGHT = 0.5
EPS = 1e-6
NEG = -1e30

LANES = 128
SUBLANES = 8
VMEM_LIMIT_BYTES = 56 * 1024 * 1024

ROW_TILE = 1024
ROW_PARTS = 2
FFN_PARTS = 2
UP_TILE = 256
ATT_TILE = 2048
ATT_BLOCK = 128
SEQ_CHUNK = 256
SEQ_GROUP = 4
HIST = 16
GATE_LANES = LANES


def _params(sem, **flags):
    return pltpu.CompilerParams(dimension_semantics=sem, vmem_limit_bytes=VMEM_LIMIT_BYTES, flags=flags or None)


def _rms(x):
    return x * lax.rsqrt(jnp.mean(x * x, axis=-1, keepdims=True) + EPS)


def _adaln_kernel(c_ref, w_ref, b_ref, o_ref):
    c = c_ref[...]
    c_act = (c * jax.nn.sigmoid(c)).astype(BF16)
    o_ref[...] = jnp.dot(c_act, w_ref[...].astype(BF16), preferred_element_type=F32) + b_ref[...]


def _adaln(c, ada_w, ada_b):
    depth, d, n = ada_w.shape
    b = c.shape[0]
    tn = n // 8
    return pl.pallas_call(
        _adaln_kernel,
        out_shape=jax.ShapeDtypeStruct((depth, b, n), F32),
        grid=(depth, n // tn),
        in_specs=[
            pl.BlockSpec((b, d), lambda l, j: (0, 0)),
            pl.BlockSpec((None, d, tn), lambda l, j: (l, 0, j)),
            pl.BlockSpec((None, 1, tn), lambda l, j: (l, 0, j)),
        ],
        out_specs=pl.BlockSpec((None, b, tn), lambda l, j: (l, 0, j)),
        compiler_params=_params(("parallel", "parallel")),
        name="adaln",
    )(c, ada_w, ada_b.reshape(depth, 1, n))


def _modulated(x, mod_ref, prew_ref, sub):
    scale = prew_ref[sub:sub + 1, :] * (1.0 + mod_ref[3 * sub + 1:3 * sub + 2, :])
    return (_rms(x) * scale + mod_ref[3 * sub:3 * sub + 1, :]).astype(BF16)


def _part(ref, part, parts=None):
    rows = ref.shape[0] // (parts or ROW_PARTS)
    return slice(part * rows, (part + 1) * rows)


def _ffn_kernel(x_ref, mod_ref, prew_ref, postw_ref, wup_ref, wdown_ref, o_ref, *, sub):
    gate = postw_ref[sub:sub + 1, :] * (MACARON_WEIGHT * mod_ref[3 * sub + 2:3 * sub + 3, :])
    u = _modulated(x_ref[_part(x_ref, 0, FFN_PARTS), :], mod_ref, prew_ref, sub)
    for part in range(FFN_PARTS):
        rows = _part(x_ref, part, FFN_PARTS)
        ff = wdown_ref.shape[0]
        acts = []
        for c in range(ff // UP_TILE):
            g = jnp.dot(u, wup_ref[:, c * UP_TILE:(c + 1) * UP_TILE], preferred_element_type=F32)
            v = jnp.dot(u, wup_ref[:, ff + c * UP_TILE:ff + (c + 1) * UP_TILE], preferred_element_type=F32)
            acts.append((g * jax.nn.sigmoid(g) * v).astype(BF16))
        if part + 1 < FFN_PARTS:
            u = _modulated(x_ref[_part(x_ref, part + 1, FFN_PARTS), :], mod_ref, prew_ref, sub)
        a = jnp.concatenate(acts, axis=1)
        y = jnp.dot(a, wdown_ref[...], preferred_element_type=F32)
        o_ref[rows, :] = x_ref[rows, :] + _rms(y) * gate


def _inproj_kernel(x_ref, mod_ref, prew_ref, wqkv_ref, wrest_ref, wgate_ref,
                   qkv1_ref, qkv4_ref, qkv16_ref, rest_ref, gate_ref, z_ref, y_ref):
    slabs = z_ref.shape[0]
    step = DILATIONS[1]
    assert DILATIONS[2] == step * step

    def by_residue(part, r):
        rows = _part(x_ref, part)
        n = rows.stop - rows.start
        per, per2 = n // step, n // (step * step)
        base = rows.start + r * per
        for c in range(slabs):
            lanes = slice(c * LANES, (c + 1) * LANES)
            picked = z_ref[c, pl.ds(rows.start + r, per, stride=step), :]
            y_ref[c, base:base + per, :] = picked
            qkv4_ref[r, part * per:(part + 1) * per, lanes] = picked.astype(BF16)
        for q in range(step):
            for c in range(slabs):
                lanes = slice(c * LANES, (c + 1) * LANES)
                qkv16_ref[r + step * q, part * per2:(part + 1) * per2, lanes] = (
                    y_ref[c, pl.ds(base + q, per2, stride=step), :].astype(BF16))

    n_rest = rest_ref.shape[1]
    groups = n_rest // UP_TILE
    residues = list(range(step))
    share = -(-len(residues) // groups)
    u = _modulated(x_ref[_part(x_ref, 0), :], mod_ref, prew_ref, 1)
    for part in range(ROW_PARTS):
        rows = _part(x_ref, part)
        z = jnp.dot(u, wqkv_ref[...], preferred_element_type=F32)
        qkv1_ref[0, rows, :] = z.astype(BF16)
        for c in range(slabs):
            z_ref[c, rows, :] = z[:, c * LANES:(c + 1) * LANES]
        gate_ref[rows, :] = jnp.dot(u, wgate_ref[...], preferred_element_type=F32)
        for k in range(groups):
            cols = slice(k * UP_TILE, (k + 1) * UP_TILE)
            rest_ref[rows, cols] = jnp.dot(u, wrest_ref[:, cols], preferred_element_type=F32).astype(BF16)
            for r in residues[k * share:(k + 1) * share]:
                by_residue(part, r)
        if part + 1 < ROW_PARTS:
            u = _modulated(x_ref[_part(x_ref, part + 1), :], mod_ref, prew_ref, 1)


def _outproj_kernel(x_ref, att_ref, pool_ref, hm_ref, mod_ref, postw_ref, wa_ref, wp_ref, wh_ref, o_ref):
    gate = postw_ref[1:2, :] * mod_ref[5:6, :]
    ys = []
    for part in range(ROW_PARTS):
        rows = _part(x_ref, part)
        y = jnp.dot(att_ref[rows, :], wa_ref[...], preferred_element_type=F32)
        y = y + jnp.dot(pool_ref[rows, :], wp_ref[...], preferred_element_type=F32)
        ys.append(y + jnp.dot(hm_ref[rows, :], wh_ref[...], preferred_element_type=F32))
    for part in range(ROW_PARTS):
        rows = _part(x_ref, part)
        o_ref[rows, :] = x_ref[rows, :] + _rms(ys[part]) * gate


def _resident(shape, index):
    return pl.BlockSpec(shape, index, pipeline_mode=pl.Buffered(1))


def _row_specs(d, seq):
    tm = ROW_TILE
    per_batch = seq // tm
    return (pl.BlockSpec((tm, d), lambda i: (i, 0)),
            pl.BlockSpec((None, 9, d), lambda i: (i // per_batch, 0, 0)),
            pl.BlockSpec((3, d), lambda i: (0, 0)))


def _ffn(h, mod, pre_w, post_w, w_up, w_down, layer, slot, seq):
    t, d = h.shape
    ff = w_down.shape[2]
    rows, mods, gains = _row_specs(d, seq)
    return pl.pallas_call(
        functools.partial(_ffn_kernel, sub=2 * slot),
        out_shape=jax.ShapeDtypeStruct((t, d), F32),
        grid=(t // ROW_TILE,),
        in_specs=[
            rows, mods, gains, gains,
            _resident((None, None, d, 2 * ff), lambda i: (layer, slot, 0, 0)),
            _resident((None, None, ff, d), lambda i: (layer, slot, 0, 0)),
        ],
        out_specs=rows,
        compiler_params=_params(("parallel",)),
        name="ffn",
    )(h, mod, pre_w, post_w, w_up, w_down)


def _inproj(h, mod, pre_w, w_qkv, w_rest, w_gate, layer, batch, seq):
    t, d = h.shape
    tm = ROW_TILE
    per_batch = seq // tm
    n_qkv, n_rest, n_gate = w_qkv.shape[2], w_rest.shape[2], w_gate.shape[2]
    rows, mods, gains = _row_specs(d, seq)
    by_residue = lambda i: (i // per_batch, 0, i % per_batch, 0)
    qkv_shapes = [jax.ShapeDtypeStruct((batch, dil, seq // dil, n_qkv), BF16) for dil in DILATIONS]
    qkv_specs = [pl.BlockSpec((None, dil, tm // dil, n_qkv), by_residue) for dil in DILATIONS]
    return pl.pallas_call(
        _inproj_kernel,
        out_shape=(
            *qkv_shapes,
            jax.ShapeDtypeStruct((t, n_rest), BF16),
            jax.ShapeDtypeStruct((t, n_gate), F32),
        ),
        grid=(t // tm,),
        in_specs=[
            rows, mods, gains,
            _resident((None, d, n_qkv), lambda i: (layer, 0, 0)),
            _resident((None, d, n_rest), lambda i: (layer, 0, 0)),
            _resident((None, d, n_gate), lambda i: (layer, 0, 0)),
        ],
        out_specs=(
            *qkv_specs,
            pl.BlockSpec((tm, n_rest), lambda i: (i, 0)),
            pl.BlockSpec((tm, n_gate), lambda i: (i, 0)),
        ),
        scratch_shapes=[pltpu.VMEM((n_qkv // LANES, tm, LANES), F32)] * 2,
        compiler_params=_params(("parallel",)),
        name="inproj",
    )(h, mod, pre_w, w_qkv, w_rest, w_gate)


def _outproj(h, att, pool, hm, mod, post_w, w_out, layer, seq):
    t, d = h.shape
    tm = ROW_TILE
    rows, mods, gains = _row_specs(d, seq)
    widths = (att.shape[-1], pool.shape[-1], hm.shape[-1])
    w_specs = []
    off = 0
    for wdt in widths:
        w_specs.append(_resident((None, wdt, d), functools.partial(lambda i, blk: (layer, blk, 0), blk=off // wdt)))
        off += wdt
    return pl.pallas_call(
        _outproj_kernel,
        out_shape=jax.ShapeDtypeStruct((t, d), F32),
        grid=(t // tm,),
        in_specs=[
            rows,
            pl.BlockSpec((tm, widths[0]), lambda i: (i, 0)),
            pl.BlockSpec((tm, widths[1]), lambda i: (i, 0)),
            pl.BlockSpec((tm, widths[2]), lambda i: (i, 0)),
            mods, gains,
            *w_specs,
        ],
        out_specs=rows,
        compiler_params=_params(("parallel",)),
        name="outproj",
    )(h, att.reshape(t, widths[0]), pool.reshape(t, widths[1]), hm.reshape(t, widths[2]), mod, post_w,
      w_out, w_out, w_out)


QKV_W = 3 * ATT_WIDTH


def _attn_scores(q, kwin, bias, head_of_lane):
    q32 = q.astype(F32) * (ATT_HEAD_DIM ** -0.5)
    lhs = jnp.concatenate(
        [jnp.where(head_of_lane == h, q32, 0.0) for h in range(ATT_HEADS)], axis=0).astype(BF16)
    return lax.dot_general(lhs, kwin, (((1,), (1,)), ((), ())), preferred_element_type=F32) + bias


def _by_lane_half(x, low):
    nq = x.shape[0] // ATT_HEADS
    out = []
    for half in range(2):
        first = x[(2 * half) * nq:(2 * half + 1) * nq]
        second = x[(2 * half + 1) * nq:(2 * half + 2) * nq]
        if x.shape[1] != 1:
            first = first[:, half * LANES:(half + 1) * LANES]
            second = second[:, half * LANES:(half + 1) * LANES]
        out.append(jnp.where(low, first, second))
    return out


def _attn_kernel(c1_ref, h1_ref, c4_ref, h4_ref, c16_ref, h16_ref, o_ref, bias_ref, dst_ref, s_ref, p_ref):
    t = pl.program_id(1)
    nq = ATT_BLOCK
    items = ATT_TILE // nq
    head_of_lane = lax.broadcasted_iota(jnp.int32, (nq, ATT_WIDTH), 1) >> (ATT_HEAD_DIM.bit_length() - 1)
    low = lax.broadcasted_iota(jnp.int32, (nq, LANES), 1) < ATT_HEAD_DIM

    qi = lax.broadcasted_iota(jnp.int32, (ATT_HEADS * nq, 2 * nq), 0) & (nq - 1)
    ki = lax.broadcasted_iota(jnp.int32, (ATT_HEADS * nq, 2 * nq), 1)
    ok = (ki >= qi) & (ki <= qi + ATT_SPAN)
    bias_ref[0] = jnp.where(ok, 0.0, NEG)
    bias_ref[1] = jnp.where(ok & (ki >= nq), 0.0, NEG)
    qs = slice(0, ATT_WIDTH)
    ks = slice(ATT_WIDTH, 2 * ATT_WIDTH)
    vs = slice(2 * ATT_WIDTH, 3 * ATT_WIDTH)

    branches = tuple(zip((c1_ref, c4_ref, c16_ref), (h1_ref, h4_ref, h16_ref), DILATIONS))

    def split(dil, idx):
        blocks = items // dil
        return idx >> (blocks.bit_length() - 1), idx & (blocks - 1)

    def window(cur_ref, halo_ref, dil, idx, cols):
        r, i = split(dil, idx)
        own = cur_ref[r, pl.ds(pl.multiple_of(i * nq, nq), nq), cols]
        if items == dil:
            prev = halo_ref[r, :, cols]
        else:
            prev0 = pl.multiple_of(jnp.maximum(i - 1, 0) * nq, nq)
            prev = jnp.where(i == 0, halo_ref[r, :, cols], cur_ref[r, pl.ds(prev0, nq), cols])
        return jnp.concatenate([prev, own], axis=0)

    def put(branch, quantity, halves, dil, idx):
        r, i = split(dil, idx)
        start = i * (nq * dil) + r
        rows = pl.ds(start, nq) if dil == 1 else pl.ds(start, nq, stride=dil)
        for half, piece in enumerate(halves):
            dst_ref[branch, quantity, half, rows, :] = piece

    for branch in range(len(branches)):
        s_ref[branch, 1] = jnp.zeros(s_ref.shape[2:], s_ref.dtype)
        p_ref[branch, 1] = jnp.zeros(p_ref.shape[2:], p_ref.dtype)

    def trip(j, slot):
        other = 1 - slot
        item_a = jnp.minimum(j, items - 1)
        item_b = jnp.clip(j - 1, 0, items - 1)
        item_c = jnp.clip(j - 2, 0, items - 1)

        for branch, (cur_ref, halo_ref, dil) in enumerate(branches):
            pv = jnp.dot(p_ref[branch, other], window(cur_ref, halo_ref, dil, item_c, vs),
                         preferred_element_type=F32)
            put(branch, 0, _by_lane_half(pv, low), dil, item_c)

        for branch, (cur_ref, halo_ref, dil) in enumerate(branches):
            s = s_ref[branch, other]
            m = jnp.max(s, axis=-1, keepdims=True)
            p = jnp.exp(s - m)
            l = jnp.sum(p, axis=-1, keepdims=True)
            p_ref[branch, slot] = p.astype(BF16)
            put(branch, 1, _by_lane_half(m, low), dil, item_b)
            put(branch, 2, _by_lane_half(l, low), dil, item_b)

        for branch, (cur_ref, halo_ref, dil) in enumerate(branches):
            r, i = split(dil, item_a)
            no_past = jnp.where((i == 0) & (t == 0), 1, 0)
            q = cur_ref[r, pl.ds(pl.multiple_of(i * nq, nq), nq), qs]
            s_ref[branch, slot] = _attn_scores(q, window(cur_ref, halo_ref, dil, item_a, ks),
                                               bias_ref[no_past], head_of_lane)

    def body(pair, carry):
        trip(2 * pair, 0)
        trip(2 * pair + 1, 1)
        return carry
    lax.fori_loop(0, (items + 2) // 2, body, 0)

    rows = ATT_BLOCK

    def merge(c, carry):
        sl = pl.ds(pl.multiple_of(c * rows, rows), rows)
        for half in range(2):
            maxes = [dst_ref[branch, 1, half, sl, :] for branch in range(len(branches))]
            top = functools.reduce(jnp.maximum, maxes)
            num = 0.0
            den = 0.0
            for branch, m_b in enumerate(maxes):
                e = jnp.exp(m_b - top)
                num = num + e * dst_ref[branch, 0, half, sl, :]
                den = den + e * dst_ref[branch, 2, half, sl, :]
            o_ref[sl, half * LANES:(half + 1) * LANES] = (num / den).astype(o_ref.dtype)
        return carry
    lax.fori_loop(0, ATT_TILE // rows, merge, 0)


def _attention(qkv_by_dilation, batch, seq):
    tiles = seq // ATT_TILE
    in_specs = []
    operands = []
    for dil, view in zip(DILATIONS, qkv_by_dilation):
        cur_rows = ATT_TILE // dil
        per_tile = cur_rows // ATT_BLOCK
        in_specs.append(pl.BlockSpec((None, dil, cur_rows, QKV_W), lambda b, t: (b, 0, t, 0)))
        in_specs.append(pl.BlockSpec(
            (None, dil, ATT_BLOCK, QKV_W),
            functools.partial(lambda b, t, per_tile: (b, 0, jnp.maximum(t * per_tile - 1, 0), 0), per_tile=per_tile)))
        operands += [view, view]
    stacked = (ATT_HEADS * ATT_BLOCK, 2 * ATT_BLOCK)
    return pl.pallas_call(
        _attn_kernel,
        out_shape=jax.ShapeDtypeStruct((batch, seq, ATT_WIDTH), BF16),
        grid=(batch, tiles),
        in_specs=in_specs,
        out_specs=pl.BlockSpec((None, ATT_TILE, ATT_WIDTH), lambda b, t: (b, t, 0)),
        scratch_shapes=[
            pltpu.VMEM((2,) + stacked, F32),
            pltpu.VMEM((len(DILATIONS), 3, 2, ATT_TILE, LANES), F32),
            pltpu.VMEM((len(DILATIONS), 2) + stacked, F32),
            pltpu.VMEM((len(DILATIONS), 2) + stacked, BF16),
        ],
        compiler_params=_params(("parallel", "parallel")),
        name="dilated_attention",
    )(*operands)


def _split3(x):
    hi = x.astype(BF16)
    r1 = x - hi.astype(F32)
    mid = r1.astype(BF16)
    lo = (r1 - mid.astype(F32)).astype(BF16)
    return hi, mid, lo


def _log_sigmoid(x):
    return -(jnp.maximum(-x, 0.0) + jnp.log1p(jnp.exp(-jnp.abs(x))))


def _mix_kernel(rest_ref, gate_ref, poolw_ref, pscale_ref, convw_ref, convb_ref, wqk_ref, wv_ref,
                gbias_ref, normw_ref, skip_ref, pool_ref, hm_ref,
                phist_ref, chist_ref, cstate_ref, mstate_ref, band_ref, shift_ref, upper_ref):
    s_idx = pl.program_id(1)
    n = SEQ_CHUNK
    dh = MLSTM_HEAD_DIM
    heads = MLSTM_HEADS
    group = range(SEQ_GROUP)

    @pl.when(s_idx == 0)
    def _():
        phist_ref[...] = jnp.zeros_like(phist_ref)
        chist_ref[...] = jnp.zeros_like(chist_ref)
        cstate_ref[...] = jnp.zeros_like(cstate_ref)
        mstate_ref[...] = jnp.zeros_like(mstate_ref)
        t_row = lax.broadcasted_iota(jnp.int32, (n, HIST + n), 0) + HIST
        col = lax.broadcasted_iota(jnp.int32, (n, HIST + n), 1)
        for g, wlen in enumerate(POOL_WINDOWS):
            band_ref[g * n:(g + 1) * n, :] = jnp.where((col <= t_row) & (col > t_row - wlen), 1.0, 0.0).astype(BF16)
        for back in range(1, MLSTM_CONV):
            shift_ref[(back - 1) * n:back * n, :] = jnp.where(col == t_row - back, 1.0, 0.0).astype(BF16)
        src = lax.broadcasted_iota(jnp.int32, (n, n), 0)
        dst = lax.broadcasted_iota(jnp.int32, (n, n), 1)
        upper_ref[...] = jnp.where(src <= dst, 1.0, 0.0).astype(BF16)

    lane = lax.broadcasted_iota(jnp.int32, (n, POOL_WIDTH), 1)
    pool_group = lane >> (POOL_GROUP_DIM.bit_length() - 1)
    pos = lax.broadcasted_iota(jnp.int32, (n, POOL_WIDTH), 0) + s_idx * n
    win_len = jnp.full((n, POOL_WIDTH), POOL_WINDOWS[0], jnp.int32)
    for g, wlen in enumerate(POOL_WINDOWS):
        win_len = jnp.where(pool_group == g, wlen, win_len)
    count = jnp.minimum(pos + 1, win_len).astype(F32)
    row = lax.broadcasted_iota(jnp.int32, (n, n), 0)
    col = lax.broadcasted_iota(jnp.int32, (n, n), 1)
    causal = col <= row
    lane_g = lax.broadcasted_iota(jnp.int32, (n, GATE_LANES), 1)
    row8 = lax.broadcasted_iota(jnp.int32, (SUBLANES, n), 0)
    ones = jnp.ones((n, dh), F32)

    def window_delta(b):
        p_b = rest_ref[b, :, 0:POOL_WIDTH]
        ext = jnp.concatenate([phist_ref[b].astype(BF16), p_b], axis=0)
        phist_ref[b] = p_b[n - HIST:, :].astype(F32)
        sums = jnp.dot(band_ref[...], ext, preferred_element_type=F32)
        win_sum = sums[0:n]
        for g in range(1, len(POOL_WINDOWS)):
            win_sum = jnp.where(pool_group == g, sums[g * n:(g + 1) * n], win_sum)
        return (win_sum / count - p_b.astype(F32)).astype(BF16)

    def front(b):
        xm_b = rest_ref[b, :, POOL_WIDTH:POOL_WIDTH + MLSTM_WIDTH]
        cext = jnp.concatenate([chist_ref[b].astype(BF16), xm_b], axis=0)
        chist_ref[b] = xm_b[n - HIST:, :].astype(F32)
        shifted = jnp.dot(shift_ref[...], cext, preferred_element_type=F32)
        conv = xm_b.astype(F32) * convw_ref[MLSTM_CONV - 1:MLSTM_CONV, :] + convb_ref[...]
        for back in range(1, MLSTM_CONV):
            tap = MLSTM_CONV - 1 - back
            conv = conv + shifted[(back - 1) * n:back * n] * convw_ref[tap:tap + 1, :]
        xc = conv * jax.nn.sigmoid(conv)
        return xc, xc.astype(BF16), xm_b

    def gate_rows(b):
        gates_t = (gate_ref[b] + gbias_ref[...]).T
        i_rows = gates_t[0:SUBLANES, :]
        return i_rows, _log_sigmoid(pltpu.roll(i_rows, SUBLANES - heads, 0))

    def gate_stats(b, i_rows, bcum):
        x_ib = i_rows - bcum
        b_last = bcum[:, n - 1:n]
        m_prev = mstate_ref[b, :, 0:1]
        a = b_last + x_ib
        m_new = jnp.maximum(b_last + m_prev, jnp.max(a, axis=1, keepdims=True))
        decay = jnp.exp(b_last + m_prev - m_new)
        w_state = jnp.exp(a - m_new)
        mstate_ref[b] = jnp.broadcast_to(m_new, mstate_ref.shape[1:])
        packed = jnp.concatenate(
            [jnp.where(row8 < heads, bcum, pltpu.roll(w_state, heads, 0)), bcum + m_prev,
             jnp.zeros((LANES - 2 * SUBLANES, n), F32)], axis=0)
        return x_ib, packed.T, decay

    def decay_log(x_ib, cols, h):
        log_d = jnp.where(causal, cols[:, h:h + 1] + x_ib[h:h + 1, :], NEG)
        return log_d, jnp.max(log_d, axis=-1, keepdims=True)

    def stabilisers(cols, row_maxes):
        intra_max = jnp.zeros((n, GATE_LANES), F32)
        for h, row_max in enumerate(row_maxes):
            intra_max = jnp.where(lane_g == 2 * heads + h, row_max, intra_max)
        m_all = jnp.maximum(cols, intra_max)
        return m_all, jnp.exp(cols - m_all), jnp.exp(-m_all)

    def head(b, h, qk, v, xc, cols, decay, log_d, m_all, inter_w_all, floor_all):
        hs = slice(h * dh, (h + 1) * dh)
        lane_m = 2 * heads + h
        q = (qk[:, :dh] * (dh ** -0.5)).astype(BF16)
        k = qk[:, dh:].astype(BF16)
        v_aug = jnp.concatenate([v, ones], axis=1)

        s = lax.dot_general(q, k, (((1,), (1,)), ((), ())), preferred_element_type=F32)
        sw = (s * jnp.exp(log_d - m_all[:, lane_m:lane_m + 1])).astype(BF16)
        c_prev = cstate_ref[b, h]
        out = inter_w_all[:, lane_m:lane_m + 1] * jnp.dot(q, c_prev.astype(BF16), preferred_element_type=F32) \
            + jnp.dot(sw, v_aug.astype(BF16), preferred_element_type=F32)
        hh = out[:, :dh] / jnp.maximum(jnp.abs(out[:, dh:]), floor_all[:, lane_m:lane_m + 1])

        wv_state = (cols[:, heads + h:heads + h + 1] * v_aug).astype(BF16)
        d_c = lax.dot_general(k, wv_state, (((0,), (0,)), ((), ())), preferred_element_type=F32)
        cstate_ref[b, h] = decay[h:h + 1, :] * c_prev + d_c

        og = rest_ref[b, :, POOL_WIDTH + MLSTM_WIDTH + h * dh:POOL_WIDTH + MLSTM_WIDTH + (h + 1) * dh].astype(F32)
        hh = hh * jax.nn.sigmoid(og)
        hh = _rms(hh) * normw_ref[:, hs]
        hm_ref[b, :, hs] = (hh + skip_ref[:, hs] * xc[:, hs]).astype(hm_ref.dtype)

    def rows_of(stacked, b, rows):
        return stacked[b * rows:(b + 1) * rows]

    raw = [gate_rows(b) for b in group]
    parts = [part for _, logf in raw for part in _split3(logf)]
    cums = jnp.dot(jnp.concatenate(parts, axis=0), upper_ref[...], preferred_element_type=F32)
    gated = []
    for b in group:
        piece = rows_of(cums, b, 3 * SUBLANES)
        bcum = piece[0:SUBLANES] + piece[SUBLANES:2 * SUBLANES] + piece[2 * SUBLANES:3 * SUBLANES]
        gated.append(gate_stats(b, raw[b][0], bcum))

    matmul_work = [lambda b=b: window_delta(b) for b in group] + [lambda b=b: front(b) for b in group]
    matmul_done = []
    decay = {}
    for b in group:
        x_ib, cols, _ = gated[b]
        for h in range(heads):
            decay[b, h] = decay_log(x_ib, cols, h)
            if matmul_work and h % 2 == 0:
                matmul_done.append(matmul_work.pop(0)())
    matmul_done += [work() for work in matmul_work]
    deltas, fronts = matmul_done[:len(group)], matmul_done[len(group):]
    stats = [stabilisers(gated[b][1], [decay[b, h][1] for h in range(heads)]) for b in group]

    pooled = jnp.dot(jnp.concatenate(deltas, axis=0), poolw_ref[...], preferred_element_type=F32) * pscale_ref[...]
    for b in group:
        pool_ref[b] = rows_of(pooled, b, n).astype(pool_ref.dtype)

    for h in range(heads):
        hs = slice(h * dh, (h + 1) * dh)
        qk_all = jnp.dot(jnp.concatenate([fronts[b][1][:, hs] for b in group], axis=0), wqk_ref[h],
                         preferred_element_type=F32)
        v_all = jnp.dot(jnp.concatenate([fronts[b][2][:, hs] for b in group], axis=0), wv_ref[h],
                        preferred_element_type=F32)
        for b in group:
            _, cols, state_decay = gated[b]
            m_all, inter_w_all, floor_all = stats[b]
            head(b, h, rows_of(qk_all, b, n), rows_of(v_all, b, n), fronts[b][0], cols, state_decay,
                 decay[b, h][0], m_all, inter_w_all, floor_all)


def _mix(rest, gates, pool_bd, pool_scale, conv_w, conv_b, w_qk, w_v, gate_bias, norm_w, skip, batch, seq):
    n = SEQ_CHUNK
    sg = SEQ_GROUP
    n_rest = rest.shape[-1]
    rest = rest.reshape(batch, seq, n_rest)
    gates = gates.reshape(batch, seq, GATE_LANES)
    c2 = lambda b, s: (0, 0)
    c3 = lambda b, s: (0, 0, 0)
    chunk = lambda b, s: (b, s, 0)
    return pl.pallas_call(
        _mix_kernel,
        out_shape=(
            jax.ShapeDtypeStruct((batch, seq, POOL_WIDTH), BF16),
            jax.ShapeDtypeStruct((batch, seq, MLSTM_WIDTH), BF16),
        ),
        grid=(batch // sg, seq // n),
        in_specs=[
            pl.BlockSpec((sg, n, n_rest), chunk),
            pl.BlockSpec((sg, n, GATE_LANES), chunk),
            pl.BlockSpec(pool_bd.shape, c2),
            pl.BlockSpec(pool_scale.shape, c2),
            pl.BlockSpec(conv_w.shape, c2),
            pl.BlockSpec(conv_b.shape, c2),
            pl.BlockSpec(w_qk.shape, c3),
            pl.BlockSpec(w_v.shape, c3),
            pl.BlockSpec(gate_bias.shape, c2),
            pl.BlockSpec(norm_w.shape, c2),
            pl.BlockSpec(skip.shape, c2),
        ],
        out_specs=(
            pl.BlockSpec((sg, n, POOL_WIDTH), chunk),
            pl.BlockSpec((sg, n, MLSTM_WIDTH), chunk),
        ),
        scratch_shapes=[
            pltpu.VMEM((sg, HIST, POOL_WIDTH), F32),
            pltpu.VMEM((sg, HIST, MLSTM_WIDTH), F32),
            pltpu.VMEM((sg, MLSTM_HEADS, MLSTM_HEAD_DIM, 2 * MLSTM_HEAD_DIM), F32),
            pltpu.VMEM((sg, SUBLANES, GATE_LANES), F32),
            pltpu.VMEM((len(POOL_WINDOWS) * n, HIST + n), BF16),
            pltpu.VMEM(((MLSTM_CONV - 1) * n, HIST + n), BF16),
            pltpu.VMEM((n, n), BF16),
        ],
        compiler_params=_params(("parallel", "arbitrary")),
        name="pool_mlstm",
    )(rest, gates, pool_bd, pool_scale, conv_w, conv_b, w_qk, w_v, gate_bias, norm_w, skip)


def kernel(x, c, ada_w, ada_b, pre_norm_w, post_norm_w, ffn_up, ffn_down, mix_in_w, mix_out_w, pool_w, pool_scale,
           mlstm_conv_w, mlstm_conv_b, mlstm_qkv_w, mlstm_gate_b, mlstm_norm_w, mlstm_skip):
    batch, seq, d = x.shape
    depth = ada_w.shape[0]
    assert d == ATT_WIDTH + POOL_WIDTH + MLSTM_WIDTH
    assert seq % ATT_TILE == 0 and seq % ROW_TILE == 0 and seq % SEQ_CHUNK == 0 and batch % SEQ_GROUP == 0

    up_b = ffn_up.astype(BF16)
    down_b = ffn_down.astype(BF16)
    out_b = mix_out_w.astype(BF16)
    n_main = QKV_W + POOL_WIDTH + 2 * MLSTM_WIDTH
    w_qkv = mix_in_w[:, :, :QKV_W].astype(BF16)
    w_rest = mix_in_w[:, :, QKV_W:n_main].astype(BF16)
    w_gate = jnp.pad(mix_in_w[:, :, n_main:], ((0, 0), (0, 0), (0, GATE_LANES - 2 * MLSTM_HEADS))).astype(BF16)
    gate_bias = jnp.pad(mlstm_gate_b.reshape(depth, 1, 2 * MLSTM_HEADS),
                        ((0, 0), (0, 0), (0, GATE_LANES - 2 * MLSTM_HEADS)))
    groups = len(POOL_WINDOWS)
    eye = jnp.eye(groups, dtype=pool_w.dtype)
    pool_bd = (pool_w[:, :, :, None, :] * eye[None, :, None, :, None]).reshape(depth, POOL_WIDTH, POOL_WIDTH).astype(BF16)
    w_qk = jnp.concatenate([mlstm_qkv_w[:, 0], mlstm_qkv_w[:, 1]], axis=-1).astype(BF16)
    w_v = mlstm_qkv_w[:, 2].astype(BF16)

    mod = _adaln(c, ada_w, ada_b).reshape(depth, batch, 9, d)

    h = x.reshape(batch * seq, d)
    for l in range(depth):
        h = _ffn(h, mod[l], pre_norm_w[l], post_norm_w[l], up_b, down_b, l, 0, seq)
        *qkv, rest, gates = _inproj(h, mod[l], pre_norm_w[l], w_qkv, w_rest, w_gate, l, batch, seq)
        att = _attention(qkv, batch, seq)
        pool, hm = _mix(rest, gates, pool_bd[l], pool_scale[l].reshape(1, -1), mlstm_conv_w[l],
                        mlstm_conv_b[l].reshape(1, -1), w_qk[l], w_v[l], gate_bias[l],
                        mlstm_norm_w[l].reshape(1, -1), mlstm_skip[l].reshape(1, -1), batch, seq)
        h = _outproj(h, att, pool, hm, mod[l], post_norm_w[l], out_b, l, seq)
        h = _ffn(h, mod[l], pre_norm_w[l], post_norm_w[l], up_b, down_b, l, 1, seq)
    return h.reshape(batch, seq, d)
```

```python
import functools

import jax
import jax.numpy as jnp
from jax import lax
from jax.experimental import pallas as pl
from jax.experimental.pallas import tpu as pltpu

F32 = jnp.float32
BF16 = jnp.bfloat16

ATT_HEADS = 4
ATT_HEAD_DIM = 64
ATT_WIDTH = ATT_HEADS * ATT_HEAD_DIM
DILATIONS = (1, 4, 16)
ATT_SPAN = 128
POOL_WINDOWS = (2, 4, 8, 16)
POOL_GROUP_DIM = 64
POOL_WIDTH = len(POOL_WINDOWS) * POOL_GROUP_DIM
MLSTM_HEADS = 4
MLSTM_HEAD_DIM = 128
MLSTM_WIDTH = MLSTM_HEADS * MLSTM_HEAD_DIM
MLSTM_CONV = 4
MACARON_WEIGHT = 0.5
EPS = 1e-6
NEG = -1e30

LANES = 128
SUBLANES = 8
VMEM_LIMIT_BYTES = 56 * 1024 * 1024

ROW_TILE = 1024
ROW_PARTS = 2
FFN_PARTS = 2
UP_TILE = 256
ATT_TILE = 2048
ATT_BLOCK = 128
SEQ_CHUNK = 256
SEQ_GROUP = 4
HIST = 16
GATE_LANES = LANES


def _params(sem, **flags):
    return pltpu.CompilerParams(dimension_semantics=sem, vmem_limit_bytes=VMEM_LIMIT_BYTES, flags=flags or None)


def _rms(x):
    return x * lax.rsqrt(jnp.mean(x * x, axis=-1, keepdims=True) + EPS)


def _adaln_kernel(c_ref, w_ref, b_ref, o_ref):
    c = c_ref[...]
    c_act = (c * jax.nn.sigmoid(c)).astype(BF16)
    o_ref[...] = jnp.dot(c_act, w_ref[...].astype(BF16), preferred_element_type=F32) + b_ref[...]


def _adaln(c, ada_w, ada_b):
    depth, d, n = ada_w.shape
    b = c.shape[0]
    tn = n // 8
    return pl.pallas_call(
        _adaln_kernel,
        out_shape=jax.ShapeDtypeStruct((depth, b, n), F32),
        grid=(depth, n // tn),
        in_specs=[
            pl.BlockSpec((b, d), lambda l, j: (0, 0)),
            pl.BlockSpec((None, d, tn), lambda l, j: (l, 0, j)),
            pl.BlockSpec((None, 1, tn), lambda l, j: (l, 0, j)),
        ],
        out_specs=pl.BlockSpec((None, b, tn), lambda l, j: (l, 0, j)),
        compiler_params=_params(("parallel", "parallel")),
        name="adaln",
    )(c, ada_w, ada_b.reshape(depth, 1, n))


def _modulated(x, mod_ref, prew_ref, sub):
    scale = prew_ref[sub:sub + 1, :] * (1.0 + mod_ref[3 * sub + 1:3 * sub + 2, :])
    return (_rms(x) * scale + mod_ref[3 * sub:3 * sub + 1, :]).astype(BF16)


def _part(ref, part, parts=None):
    rows = ref.shape[0] // (parts or ROW_PARTS)
    return slice(part * rows, (part + 1) * rows)


def _ffn_kernel(x_ref, mod_ref, prew_ref, postw_ref, wup_ref, wdown_ref, o_ref, *, sub):
    gate = postw_ref[sub:sub + 1, :] * (MACARON_WEIGHT * mod_ref[3 * sub + 2:3 * sub + 3, :])
    u = _modulated(x_ref[_part(x_ref, 0, FFN_PARTS), :], mod_ref, prew_ref, sub)
    for part in range(FFN_PARTS):
        rows = _part(x_ref, part, FFN_PARTS)
        ff = wdown_ref.shape[0]
        acts = []
        for c in range(ff // UP_TILE):
            g = jnp.dot(u, wup_ref[:, c * UP_TILE:(c + 1) * UP_TILE], preferred_element_type=F32)
            v = jnp.dot(u, wup_ref[:, ff + c * UP_TILE:ff + (c + 1) * UP_TILE], preferred_element_type=F32)
            acts.append((g * jax.nn.sigmoid(g) * v).astype(BF16))
        if part + 1 < FFN_PARTS:
            u = _modulated(x_ref[_part(x_ref, part + 1, FFN_PARTS), :], mod_ref, prew_ref, sub)
        a = jnp.concatenate(acts, axis=1)
        y = jnp.dot(a, wdown_ref[...], preferred_element_type=F32)
        o_ref[rows, :] = x_ref[rows, :] + _rms(y) * gate


def _inproj_kernel(x_ref, mod_ref, prew_ref, wqkv_ref, wrest_ref, wgate_ref,
                   qkv1_ref, qkv4_ref, qkv16_ref, rest_ref, gate_ref, z_ref, y_ref):
    slabs = z_ref.shape[0]
    step = DILATIONS[1]
    assert DILATIONS[2] == step * step

    def by_residue(part, r):
        rows = _part(x_ref, part)
        n = rows.stop - rows.start
        per, per2 = n // step, n // (step * step)
        base = rows.start + r * per
        for c in range(slabs):
            lanes = slice(c * LANES, (c + 1) * LANES)
            picked = z_ref[c, pl.ds(rows.start + r, per, stride=step), :]
            y_ref[c, base:base + per, :] = picked
            qkv4_ref[r, part * per:(part + 1) * per, lanes] = picked.astype(BF16)
        for q in range(step):
            for c in range(slabs):
                lanes = slice(c * LANES, (c + 1) * LANES)
                qkv16_ref[r + step * q, part * per2:(part + 1) * per2, lanes] = (
                    y_ref[c, pl.ds(base + q, per2, stride=step), :].astype(BF16))

    n_rest = rest_ref.shape[1]
    groups = n_rest // UP_TILE
    residues = list(range(step))
    share = -(-len(residues) // groups)
    u = _modulated(x_ref[_part(x_ref, 0), :], mod_ref, prew_ref, 1)
    for part in range(ROW_PARTS):
        rows = _part(x_ref, part)
        z = jnp.dot(u, wqkv_ref[...], preferred_element_type=F32)
        qkv1_ref[0, rows, :] = z.astype(BF16)
        for c in range(slabs):
            z_ref[c, rows, :] = z[:, c * LANES:(c + 1) * LANES]
        gate_ref[rows, :] = jnp.dot(u, wgate_ref[...], preferred_element_type=F32)
        for k in range(groups):
            cols = slice(k * UP_TILE, (k + 1) * UP_TILE)
            rest_ref[rows, cols] = jnp.dot(u, wrest_ref[:, cols], preferred_element_type=F32).astype(BF16)
            for r in residues[k * share:(k + 1) * share]:
                by_residue(part, r)
        if part + 1 < ROW_PARTS:
            u = _modulated(x_ref[_part(x_ref, part + 1), :], mod_ref, prew_ref, 1)


def _outproj_kernel(x_ref, att_ref, pool_ref, hm_ref, mod_ref, postw_ref, wa_ref, wp_ref, wh_ref, o_ref):
    gate = postw_ref[1:2, :] * mod_ref[5:6, :]
    ys = []
    for part in range(ROW_PARTS):
        rows = _part(x_ref, part)
        y = jnp.dot(att_ref[rows, :], wa_ref[...], preferred_element_type=F32)
        y = y + jnp.dot(pool_ref[rows, :], wp_ref[...], preferred_element_type=F32)
        ys.append(y + jnp.dot(hm_ref[rows, :], wh_ref[...], preferred_element_type=F32))
    for part in range(ROW_PARTS):
        rows = _part(x_ref, part)
        o_ref[rows, :] = x_ref[rows, :] + _rms(ys[part]) * gate


def _resident(shape, index):
    return pl.BlockSpec(shape, index, pipeline_mode=pl.Buffered(1))


def _row_specs(d, seq):
    tm = ROW_TILE
    per_batch = seq // tm
    return (pl.BlockSpec((tm, d), lambda i: (i, 0)),
            pl.BlockSpec((None, 9, d), lambda i: (i // per_batch, 0, 0)),
            pl.BlockSpec((3, d), lambda i: (0, 0)))


def _ffn(h, mod, pre_w, post_w, w_up, w_down, layer, slot, seq):
    t, d = h.shape
    ff = w_down.shape[2]
    rows, mods, gains = _row_specs(d, seq)
    return pl.pallas_call(
        functools.partial(_ffn_kernel, sub=2 * slot),
        out_shape=jax.ShapeDtypeStruct((t, d), F32),
        grid=(t // ROW_TILE,),
        in_specs=[
            rows, mods, gains, gains,
            _resident((None, None, d, 2 * ff), lambda i: (layer, slot, 0, 0)),
            _resident((None, None, ff, d), lambda i: (layer, slot, 0, 0)),
        ],
        out_specs=rows,
        compiler_params=_params(("parallel",)),
        name="ffn",
    )(h, mod, pre_w, post_w, w_up, w_down)


def _inproj(h, mod, pre_w, w_qkv, w_rest, w_gate, layer, batch, seq):
    t, d = h.shape
    tm = ROW_TILE
    per_batch = seq // tm
    n_qkv, n_rest, n_gate = w_qkv.shape[2], w_rest.shape[2], w_gate.shape[2]
    rows, mods, gains = _row_specs(d, seq)
    by_residue = lambda i: (i // per_batch, 0, i % per_batch, 0)
    qkv_shapes = [jax.ShapeDtypeStruct((batch, dil, seq // dil, n_qkv), BF16) for dil in DILATIONS]
    qkv_specs = [pl.BlockSpec((None, dil, tm // dil, n_qkv), by_residue) for dil in DILATIONS]
    return pl.pallas_call(
        _inproj_kernel,
        out_shape=(
            *qkv_shapes,
            jax.ShapeDtypeStruct((t, n_rest), BF16),
            jax.ShapeDtypeStruct((t, n_gate), F32),
        ),
        grid=(t // tm,),
        in_specs=[
            rows, mods, gains,
            _resident((None, d, n_qkv), lambda i: (layer, 0, 0)),
            _resident((None, d, n_rest), lambda i: (layer, 0, 0)),
            _resident((None, d, n_gate), lambda i: (layer, 0, 0)),
        ],
        out_specs=(
            *qkv_specs,
            pl.BlockSpec((tm, n_rest), lambda i: (i, 0)),
            pl.BlockSpec((tm, n_gate), lambda i: (i, 0)),
        ),
        scratch_shapes=[pltpu.VMEM((n_qkv // LANES, tm, LANES), F32)] * 2,
        compiler_params=_params(("parallel",)),
        name="inproj",
    )(h, mod, pre_w, w_qkv, w_rest, w_gate)


def _outproj(h, att, pool, hm, mod, post_w, w_out, layer, seq):
    t, d = h.shape
    tm = ROW_TILE
    rows, mods, gains = _row_specs(d, seq)
    widths = (att.shape[-1], pool.shape[-1], hm.shape[-1])
    w_specs = []
    off = 0
    for wdt in widths:
        w_specs.append(_resident((None, wdt, d), functools.partial(lambda i, blk: (layer, blk, 0), blk=off // wdt)))
        off += wdt
    return pl.pallas_call(
        _outproj_kernel,
        out_shape=jax.ShapeDtypeStruct((t, d), F32),
        grid=(t // tm,),
        in_specs=[
            rows,
            pl.BlockSpec((tm, widths[0]), lambda i: (i, 0)),
            pl.BlockSpec((tm, widths[1]), lambda i: (i, 0)),
            pl.BlockSpec((tm, widths[2]), lambda i: (i, 0)),
            mods, gains,
            *w_specs,
        ],
        out_specs=rows,
        compiler_params=_params(("parallel",)),
        name="outproj",
    )(h, att.reshape(t, widths[0]), pool.reshape(t, widths[1]), hm.reshape(t, widths[2]), mod, post_w,
      w_out, w_out, w_out)


QKV_W = 3 * ATT_WIDTH


def _attn_scores(q, kwin, bias, head_of_lane):
    q32 = q.astype(F32) * (ATT_HEAD_DIM ** -0.5)
    lhs = jnp.concatenate(
        [jnp.where(head_of_lane == h, q32, 0.0) for h in range(ATT_HEADS)], axis=0).astype(BF16)
    return lax.dot_general(lhs, kwin, (((1,), (1,)), ((), ())), preferred_element_type=F32) + bias


def _by_lane_half(x, low):
    nq = x.shape[0] // ATT_HEADS
    out = []
    for half in range(2):
        first = x[(2 * half) * nq:(2 * half + 1) * nq]
        second = x[(2 * half + 1) * nq:(2 * half + 2) * nq]
        if x.shape[1] != 1:
            first = first[:, half * LANES:(half + 1) * LANES]
            second = second[:, half * LANES:(half + 1) * LANES]
        out.append(jnp.where(low, first, second))
    return out


def _attn_kernel(c1_ref, h1_ref, c4_ref, h4_ref, c16_ref, h16_ref, o_ref, bias_ref, dst_ref, s_ref, p_ref):
    t = pl.program_id(1)
    nq = ATT_BLOCK
    items = ATT_TILE // nq
    head_of_lane = lax.broadcasted_iota(jnp.int32, (nq, ATT_WIDTH), 1) >> (ATT_HEAD_DIM.bit_length() - 1)
    low = lax.broadcasted_iota(jnp.int32, (nq, LANES), 1) < ATT_HEAD_DIM

    qi = lax.broadcasted_iota(jnp.int32, (ATT_HEADS * nq, 2 * nq), 0) & (nq - 1)
    ki = lax.broadcasted_iota(jnp.int32, (ATT_HEADS * nq, 2 * nq), 1)
    ok = (ki >= qi) & (ki <= qi + ATT_SPAN)
    bias_ref[0] = jnp.where(ok, 0.0, NEG)
    bias_ref[1] = jnp.where(ok & (ki >= nq), 0.0, NEG)
    qs = slice(0, ATT_WIDTH)
    ks = slice(ATT_WIDTH, 2 * ATT_WIDTH)
    vs = slice(2 * ATT_WIDTH, 3 * ATT_WIDTH)

    branches = tuple(zip((c1_ref, c4_ref, c16_ref), (h1_ref, h4_ref, h16_ref), DILATIONS))

    def split(dil, idx):
        blocks = items // dil
        return idx >> (blocks.bit_length() - 1), idx & (blocks - 1)

    def window(cur_ref, halo_ref, dil, idx, cols):
        r, i = split(dil, idx)
        own = cur_ref[r, pl.ds(pl.multiple_of(i * nq, nq), nq), cols]
        if items == dil:
            prev = halo_ref[r, :, cols]
        else:
            prev0 = pl.multiple_of(jnp.maximum(i - 1, 0) * nq, nq)
            prev = jnp.where(i == 0, halo_ref[r, :, cols], cur_ref[r, pl.ds(prev0, nq), cols])
        return jnp.concatenate([prev, own], axis=0)

    def put(branch, quantity, halves, dil, idx):
        r, i = split(dil, idx)
        start = i * (nq * dil) + r
        rows = pl.ds(start, nq) if dil == 1 else pl.ds(start, nq, stride=dil)
        for half, piece in enumerate(halves):
            dst_ref[branch, quantity, half, rows, :] = piece

    for branch in range(len(branches)):
        s_ref[branch, 1] = jnp.zeros(s_ref.shape[2:], s_ref.dtype)
        p_ref[branch, 1] = jnp.zeros(p_ref.shape[2:], p_ref.dtype)

    def trip(j, slot):
        other = 1 - slot
        item_a = jnp.minimum(j, items - 1)
        item_b = jnp.clip(j - 1, 0, items - 1)
        item_c = jnp.clip(j - 2, 0, items - 1)

        for branch, (cur_ref, halo_ref, dil) in enumerate(branches):
            pv = jnp.dot(p_ref[branch, other], window(cur_ref, halo_ref, dil, item_c, vs),
                         preferred_element_type=F32)
            put(branch, 0, _by_lane_half(pv, low), dil, item_c)

        for branch, (cur_ref, halo_ref, dil) in enumerate(branches):
            s = s_ref[branch, other]
            m = jnp.max(s, axis=-1, keepdims=True)
            p = jnp.exp(s - m)
            l = jnp.sum(p, axis=-1, keepdims=True)
            p_ref[branch, slot] = p.astype(BF16)
            put(branch, 1, _by_lane_half(m, low), dil, item_b)
            put(branch, 2, _by_lane_half(l, low), dil, item_b)

        for branch, (cur_ref, halo_ref, dil) in enumerate(branches):
            r, i = split(dil, item_a)
            no_past = jnp.where((i == 0) & (t == 0), 1, 0)
            q = cur_ref[r, pl.ds(pl.multiple_of(i * nq, nq), nq), qs]
            s_ref[branch, slot] = _attn_scores(q, window(cur_ref, halo_ref, dil, item_a, ks),
                                               bias_ref[no_past], head_of_lane)

    def body(pair, carry):
        trip(2 * pair, 0)
        trip(2 * pair + 1, 1)
        return carry
    lax.fori_loop(0, (items + 2) // 2, body, 0)

    rows = ATT_BLOCK

    def merge(c, carry):
        sl = pl.ds(pl.multiple_of(c * rows, rows), rows)
        for half in range(2):
            maxes = [dst_ref[branch, 1, half, sl, :] for branch in range(len(branches))]
            top = functools.reduce(jnp.maximum, maxes)
            num = 0.0
            den = 0.0
            for branch, m_b in enumerate(maxes):
                e = jnp.exp(m_b - top)
                num = num + e * dst_ref[branch, 0, half, sl, :]
                den = den + e * dst_ref[branch, 2, half, sl, :]
            o_ref[sl, half * LANES:(half + 1) * LANES] = (num / den).astype(o_ref.dtype)
        return carry
    lax.fori_loop(0, ATT_TILE // rows, merge, 0)


def _attention(qkv_by_dilation, batch, seq):
    tiles = seq // ATT_TILE
    in_specs = []
    operands = []
    for dil, view in zip(DILATIONS, qkv_by_dilation):
        cur_rows = ATT_TILE // dil
        per_tile = cur_rows // ATT_BLOCK
        in_specs.append(pl.BlockSpec((None, dil, cur_rows, QKV_W), lambda b, t: (b, 0, t, 0)))
        in_specs.append(pl.BlockSpec(
            (None, dil, ATT_BLOCK, QKV_W),
            functools.partial(lambda b, t, per_tile: (b, 0, jnp.maximum(t * per_tile - 1, 0), 0), per_tile=per_tile)))
        operands += [view, view]
    stacked = (ATT_HEADS * ATT_BLOCK, 2 * ATT_BLOCK)
    return pl.pallas_call(
        _attn_kernel,
        out_shape=jax.ShapeDtypeStruct((batch, seq, ATT_WIDTH), BF16),
        grid=(batch, tiles),
        in_specs=in_specs,
        out_specs=pl.BlockSpec((None, ATT_TILE, ATT_WIDTH), lambda b, t: (b, t, 0)),
        scratch_shapes=[
            pltpu.VMEM((2,) + stacked, F32),
            pltpu.VMEM((len(DILATIONS), 3, 2, ATT_TILE, LANES), F32),
            pltpu.VMEM((len(DILATIONS), 2) + stacked, F32),
            pltpu.VMEM((len(DILATIONS), 2) + stacked, BF16),
        ],
        compiler_params=_params(("parallel", "parallel")),
        name="dilated_attention",
    )(*operands)


def _split3(x):
    hi = x.astype(BF16)
    r1 = x - hi.astype(F32)
    mid = r1.astype(BF16)
    lo = (r1 - mid.astype(F32)).astype(BF16)
    return hi, mid, lo


def _log_sigmoid(x):
    return -(jnp.maximum(-x, 0.0) + jnp.log1p(jnp.exp(-jnp.abs(x))))


def _mix_kernel(rest_ref, gate_ref, poolw_ref, pscale_ref, convw_ref, convb_ref, wqk_ref, wv_ref,
                gbias_ref, normw_ref, skip_ref, pool_ref, hm_ref,
                phist_ref, chist_ref, cstate_ref, mstate_ref, band_ref, shift_ref, upper_ref):
    s_idx = pl.program_id(1)
    n = SEQ_CHUNK
    dh = MLSTM_HEAD_DIM
    heads = MLSTM_HEADS
    group = range(SEQ_GROUP)

    @pl.when(s_idx == 0)
    def _():
        phist_ref[...] = jnp.zeros_like(phist_ref)
        chist_ref[...] = jnp.zeros_like(chist_ref)
        cstate_ref[...] = jnp.zeros_like(cstate_ref)
        mstate_ref[...] = jnp.zeros_like(mstate_ref)
        t_row = lax.broadcasted_iota(jnp.int32, (n, HIST + n), 0) + HIST
        col = lax.broadcasted_iota(jnp.int32, (n, HIST + n), 1)
        for g, wlen in enumerate(POOL_WINDOWS):
            band_ref[g * n:(g + 1) * n, :] = jnp.where((col <= t_row) & (col > t_row - wlen), 1.0, 0.0).astype(BF16)
        for back in range(1, MLSTM_CONV):
            shift_ref[(back - 1) * n:back * n, :] = jnp.where(col == t_row - back, 1.0, 0.0).astype(BF16)
        src = lax.broadcasted_iota(jnp.int32, (n, n), 0)
        dst = lax.broadcasted_iota(jnp.int32, (n, n), 1)
        upper_ref[...] = jnp.where(src <= dst, 1.0, 0.0).astype(BF16)

    lane = lax.broadcasted_iota(jnp.int32, (n, POOL_WIDTH), 1)
    pool_group = lane >> (POOL_GROUP_DIM.bit_length() - 1)
    pos = lax.broadcasted_iota(jnp.int32, (n, POOL_WIDTH), 0) + s_idx * n
    win_len = jnp.full((n, POOL_WIDTH), POOL_WINDOWS[0], jnp.int32)
    for g, wlen in enumerate(POOL_WINDOWS):
        win_len = jnp.where(pool_group == g, wlen, win_len)
    count = jnp.minimum(pos + 1, win_len).astype(F32)
    row = lax.broadcasted_iota(jnp.int32, (n, n), 0)
    col = lax.broadcasted_iota(jnp.int32, (n, n), 1)
    causal = col <= row
    lane8 = lax.broadcasted_iota(jnp.int32, (SUBLANES, n), 1)
    row8 = lax.broadcasted_iota(jnp.int32, (SUBLANES, n), 0)
    ones = jnp.ones((n, dh), F32)

    def window_delta(b):
        p_b = rest_ref[b, :, 0:POOL_WIDTH]
        ext = jnp.concatenate([phist_ref[b].astype(BF16), p_b], axis=0)
        phist_ref[b] = p_b[n - HIST:, :].astype(F32)
        sums = jnp.dot(band_ref[...], ext, preferred_element_type=F32)
        win_sum = sums[0:n]
        for g in range(1, len(POOL_WINDOWS)):
            win_sum = jnp.where(pool_group == g, sums[g * n:(g + 1) * n], win_sum)
        return (win_sum / count - p_b.astype(F32)).astype(BF16)

    def front(b):
        xm_b = rest_ref[b, :, POOL_WIDTH:POOL_WIDTH + MLSTM_WIDTH]
        cext = jnp.concatenate([chist_ref[b].astype(BF16), xm_b], axis=0)
        chist_ref[b] = xm_b[n - HIST:, :].astype(F32)
        shifted = jnp.dot(shift_ref[...], cext, preferred_element_type=F32)
        conv = xm_b.astype(F32) * convw_ref[MLSTM_CONV - 1:MLSTM_CONV, :] + convb_ref[...]
        for back in range(1, MLSTM_CONV):
            tap = MLSTM_CONV - 1 - back
            conv = conv + shifted[(back - 1) * n:back * n] * convw_ref[tap:tap + 1, :]
        xc = conv * jax.nn.sigmoid(conv)
        return xc, xc.astype(BF16), xm_b

    def gate_rows(b):
        gates_t = (gate_ref[b] + gbias_ref[...]).T
        i_rows = gates_t[0:SUBLANES, :]
        return i_rows, _log_sigmoid(pltpu.roll(i_rows, SUBLANES - heads, 0))

    def gate_stats(b, i_rows, bcum):
        x_ib = i_rows - bcum
        b_last = bcum[:, n - 1:n]
        m_prev = mstate_ref[b, :, 0:1]
        a = b_last + x_ib
        m_new = jnp.maximum(b_last + m_prev, jnp.max(a, axis=1, keepdims=True))
        decay = jnp.exp(b_last + m_prev - m_new)
        w_state = jnp.exp(a - m_new)
        mstate_ref[b] = jnp.broadcast_to(m_new, mstate_ref.shape[1:])
        prefix_max = x_ib
        shift = 1
        while shift < n:
            prefix_max = jnp.maximum(prefix_max, jnp.where(lane8 >= shift, pltpu.roll(prefix_max, shift, 1), NEG))
            shift *= 2
        inter_log = bcum + m_prev
        m_t = jnp.maximum(inter_log, bcum + prefix_max)
        packed = jnp.concatenate(
            [jnp.where(row8 < heads, bcum - m_t, pltpu.roll(w_state, heads, 0)),
             jnp.where(row8 < heads, jnp.exp(inter_log - m_t), pltpu.roll(jnp.exp(-m_t), heads, 0)),
             jnp.zeros((LANES - 2 * SUBLANES, n), F32)], axis=0)
        return x_ib, packed.T, decay

    def head(b, h, qk, v, xc, x_ib, cols, decay):
        hs = slice(h * dh, (h + 1) * dh)
        q = (qk[:, :dh] * (dh ** -0.5)).astype(BF16)
        k = qk[:, dh:].astype(BF16)
        v_aug = jnp.concatenate([v, ones], axis=1)

        s = lax.dot_general(q, k, (((1,), (1,)), ((), ())), preferred_element_type=F32)
        log_d = jnp.where(causal, cols[:, h:h + 1] + x_ib[h:h + 1, :], NEG)
        sw = (s * jnp.exp(log_d)).astype(BF16)
        c_prev = cstate_ref[b, h]
        out = cols[:, 2 * heads + h:2 * heads + h + 1] * jnp.dot(q, c_prev.astype(BF16), preferred_element_type=F32) \
            + jnp.dot(sw, v_aug.astype(BF16), preferred_element_type=F32)
        hh = out[:, :dh] / jnp.maximum(jnp.abs(out[:, dh:]), cols[:, 3 * heads + h:3 * heads + h + 1])

        wv_state = (cols[:, heads + h:heads + h + 1] * v_aug).astype(BF16)
        d_c = lax.dot_general(k, wv_state, (((0,), (0,)), ((), ())), preferred_element_type=F32)
        cstate_ref[b, h] = decay[h:h + 1, :] * c_prev + d_c

        og = rest_ref[b, :, POOL_WIDTH + MLSTM_WIDTH + h * dh:POOL_WIDTH + MLSTM_WIDTH + (h + 1) * dh].astype(F32)
        hh = hh * jax.nn.sigmoid(og)
        hh = _rms(hh) * normw_ref[:, hs]
        hm_ref[b, :, hs] = (hh + skip_ref[:, hs] * xc[:, hs]).astype(hm_ref.dtype)

    def rows_of(stacked, b, rows):
        return stacked[b * rows:(b + 1) * rows]

    raw, deltas = [], []
    for b in group:
        deltas.append(window_delta(b))
        raw.append(gate_rows(b))
    pooled = jnp.dot(jnp.concatenate(deltas, axis=0), poolw_ref[...], preferred_element_type=F32) * pscale_ref[...]
    for b in group:
        pool_ref[b] = rows_of(pooled, b, n).astype(pool_ref.dtype)
    parts = [part for _, logf in raw for part in _split3(logf)]
    cums = jnp.dot(jnp.concatenate(parts, axis=0), upper_ref[...], preferred_element_type=F32)
    gated, fronts = [], []
    for b in group:
        fronts.append(front(b))
        piece = rows_of(cums, b, 3 * SUBLANES)
        bcum = piece[0:SUBLANES] + piece[SUBLANES:2 * SUBLANES] + piece[2 * SUBLANES:3 * SUBLANES]
        gated.append(gate_stats(b, raw[b][0], bcum))

    for h in range(heads):
        hs = slice(h * dh, (h + 1) * dh)
        qk_all = jnp.dot(jnp.concatenate([fronts[b][1][:, hs] for b in group], axis=0), wqk_ref[h],
                         preferred_element_type=F32)
        v_all = jnp.dot(jnp.concatenate([fronts[b][2][:, hs] for b in group], axis=0), wv_ref[h],
                        preferred_element_type=F32)
        for b in group:
            x_ib, cols, state_decay = gated[b]
            head(b, h, rows_of(qk_all, b, n), rows_of(v_all, b, n), fronts[b][0], x_ib, cols, state_decay)


def _mix(rest, gates, pool_bd, pool_scale, conv_w, conv_b, w_qk, w_v, gate_bias, norm_w, skip, batch, seq):
    n = SEQ_CHUNK
    sg = SEQ_GROUP
    n_rest = rest.shape[-1]
    rest = rest.reshape(batch, seq, n_rest)
    gates = gates.reshape(batch, seq, GATE_LANES)
    c2 = lambda b, s: (0, 0)
    c3 = lambda b, s: (0, 0, 0)
    chunk = lambda b, s: (b, s, 0)
    return pl.pallas_call(
        _mix_kernel,
        out_shape=(
            jax.ShapeDtypeStruct((batch, seq, POOL_WIDTH), BF16),
            jax.ShapeDtypeStruct((batch, seq, MLSTM_WIDTH), BF16),
        ),
        grid=(batch // sg, seq // n),
        in_specs=[
            pl.BlockSpec((sg, n, n_rest), chunk),
            pl.BlockSpec((sg, n, GATE_LANES), chunk),
            pl.BlockSpec(pool_bd.shape, c2),
            pl.BlockSpec(pool_scale.shape, c2),
            pl.BlockSpec(conv_w.shape, c2),
            pl.BlockSpec(conv_b.shape, c2),
            pl.BlockSpec(w_qk.shape, c3),
            pl.BlockSpec(w_v.shape, c3),
            pl.BlockSpec(gate_bias.shape, c2),
            pl.BlockSpec(norm_w.shape, c2),
            pl.BlockSpec(skip.shape, c2),
        ],
        out_specs=(
            pl.BlockSpec((sg, n, POOL_WIDTH), chunk),
            pl.BlockSpec((sg, n, MLSTM_WIDTH), chunk),
        ),
        scratch_shapes=[
            pltpu.VMEM((sg, HIST, POOL_WIDTH), F32),
            pltpu.VMEM((sg, HIST, MLSTM_WIDTH), F32),
            pltpu.VMEM((sg, MLSTM_HEADS, MLSTM_HEAD_DIM, 2 * MLSTM_HEAD_DIM), F32),
            pltpu.VMEM((sg, SUBLANES, GATE_LANES), F32),
            pltpu.VMEM((len(POOL_WINDOWS) * n, HIST + n), BF16),
            pltpu.VMEM(((MLSTM_CONV - 1) * n, HIST + n), BF16),
            pltpu.VMEM((n, n), BF16),
        ],
        compiler_params=_params(("parallel", "arbitrary")),
        name="pool_mlstm",
    )(rest, gates, pool_bd, pool_scale, conv_w, conv_b, w_qk, w_v, gate_bias, norm_w, skip)


def kernel(x, c, ada_w, ada_b, pre_norm_w, post_norm_w, ffn_up, ffn_down, mix_in_w, mix_out_w, pool_w, pool_scale,
           mlstm_conv_w, mlstm_conv_b, mlstm_qkv_w, mlstm_gate_b, mlstm_norm_w, mlstm_skip):
    batch, seq, d = x.shape
    depth = ada_w.shape[0]
    assert d == ATT_WIDTH + POOL_WIDTH + MLSTM_WIDTH
    assert seq % ATT_TILE == 0 and seq % ROW_TILE == 0 and seq % SEQ_CHUNK == 0 and batch % SEQ_GROUP == 0

    up_b = ffn_up.astype(BF16)
    down_b = ffn_down.astype(BF16)
    out_b = mix_out_w.astype(BF16)
    n_main = QKV_W + POOL_WIDTH + 2 * MLSTM_WIDTH
    w_qkv = mix_in_w[:, :, :QKV_W].astype(BF16)
    w_rest = mix_in_w[:, :, QKV_W:n_main].astype(BF16)
    w_gate = jnp.pad(mix_in_w[:, :, n_main:], ((0, 0), (0, 0), (0, GATE_LANES - 2 * MLSTM_HEADS))).astype(BF16)
    gate_bias = jnp.pad(mlstm_gate_b.reshape(depth, 1, 2 * MLSTM_HEADS),
                        ((0, 0), (0, 0), (0, GATE_LANES - 2 * MLSTM_HEADS)))
    groups = len(POOL_WINDOWS)
    eye = jnp.eye(groups, dtype=pool_w.dtype)
    pool_bd = (pool_w[:, :, :, None, :] * eye[None, :, None, :, None]).reshape(depth, POOL_WIDTH, POOL_WIDTH).astype(BF16)
    w_qk = jnp.concatenate([mlstm_qkv_w[:, 0], mlstm_qkv_w[:, 1]], axis=-1).astype(BF16)
    w_v = mlstm_qkv_w[:, 2].astype(BF16)

    mod = _adaln(c, ada_w, ada_b).reshape(depth, batch, 9, d)

    h = x.reshape(batch * seq, d)
    for l in range(depth):
        h = _ffn(h, mod[l], pre_norm_w[l], post_norm_w[l], up_b, down_b, l, 0, seq)
        *qkv, rest, gates = _inproj(h, mod[l], pre_norm_w[l], w_qkv, w_rest, w_gate, l, batch, seq)
        att = _attention(qkv, batch, seq)
        pool, hm = _mix(rest, gates, pool_bd[l], pool_scale[l].reshape(1, -1), mlstm_conv_w[l],
                        mlstm_conv_b[l].reshape(1, -1), w_qk[l], w_v[l], gate_bias[l],
                        mlstm_norm_w[l].reshape(1, -1), mlstm_skip[l].reshape(1, -1), batch, seq)
        h = _outproj(h, att, pool, hm, mod[l], post_norm_w[l], out_b, l, seq)
        h = _ffn(h, mod[l], pre_norm_w[l], post_norm_w[l], up_b, down_b, l, 1, seq)
    return h.reshape(batch, seq, d)
```

```python
import functools

import jax
import jax.numpy as jnp
from jax import lax
from jax.experimental import pallas as pl
from jax.experimental.pallas import tpu as pltpu

F32 = jnp.float32
BF16 = jnp.bfloat16

ATT_HEADS = 4
ATT_HEAD_DIM = 64
ATT_WIDTH = ATT_HEADS * ATT_HEAD_DIM
DILATIONS = (1, 4, 16)
ATT_SPAN = 128
POOL_WINDOWS = (2, 4, 8, 16)
POOL_GROUP_DIM = 64
POOL_WIDTH = len(POOL_WINDOWS) * POOL_GROUP_DIM
MLSTM_HEADS = 4
MLSTM_HEAD_DIM = 128
MLSTM_WIDTH = MLSTM_HEADS * MLSTM_HEAD_DIM
MLSTM_CONV = 4
MACARON_WEIGHT = 0.5
EPS = 1e-6
NEG = -1e30

LANES = 128
SUBLANES = 8
VMEM_LIMIT_BYTES = 56 * 1024 * 1024

ROW_TILE = 1024
ROW_PARTS = 2
FFN_PARTS = 2
UP_TILE = 256
ATT_TILE = 2048
ATT_BLOCK = 128
SEQ_CHUNK = 256
SEQ_GROUP = 4
HIST = 16
GATE_LANES = LANES


def _params(sem, **flags):
    return pltpu.CompilerParams(dimension_semantics=sem, vmem_limit_bytes=VMEM_LIMIT_BYTES, flags=flags or None)


def _rms(x):
    return x * lax.rsqrt(jnp.mean(x * x, axis=-1, keepdims=True) + EPS)


def _adaln_kernel(c_ref, w_ref, b_ref, o_ref):
    c = c_ref[...]
    c_act = (c * jax.nn.sigmoid(c)).astype(BF16)
    o_ref[...] = jnp.dot(c_act, w_ref[...].astype(BF16), preferred_element_type=F32) + b_ref[...]


def _adaln(c, ada_w, ada_b):
    depth, d, n = ada_w.shape
    b = c.shape[0]
    tn = n // 8
    return pl.pallas_call(
        _adaln_kernel,
        out_shape=jax.ShapeDtypeStruct((depth, b, n), F32),
        grid=(depth, n // tn),
        in_specs=[
            pl.BlockSpec((b, d), lambda l, j: (0, 0)),
            pl.BlockSpec((None, d, tn), lambda l, j: (l, 0, j)),
            pl.BlockSpec((None, 1, tn), lambda l, j: (l, 0, j)),
        ],
        out_specs=pl.BlockSpec((None, b, tn), lambda l, j: (l, 0, j)),
        compiler_params=_params(("parallel", "parallel")),
        name="adaln",
    )(c, ada_w, ada_b.reshape(depth, 1, n))


def _modulated(x, mod_ref, prew_ref, sub):
    scale = prew_ref[sub:sub + 1, :] * (1.0 + mod_ref[3 * sub + 1:3 * sub + 2, :])
    return (_rms(x) * scale + mod_ref[3 * sub:3 * sub + 1, :]).astype(BF16)


def _part(ref, part, parts=None):
    rows = ref.shape[0] // (parts or ROW_PARTS)
    return slice(part * rows, (part + 1) * rows)


def _ffn_kernel(x_ref, mod_ref, prew_ref, postw_ref, wup_ref, wdown_ref, o_ref, *, sub):
    gate = postw_ref[sub:sub + 1, :] * (MACARON_WEIGHT * mod_ref[3 * sub + 2:3 * sub + 3, :])
    u = _modulated(x_ref[_part(x_ref, 0, FFN_PARTS), :], mod_ref, prew_ref, sub)
    for part in range(FFN_PARTS):
        rows = _part(x_ref, part, FFN_PARTS)
        ff = wdown_ref.shape[0]
        acts = []
        for c in range(ff // UP_TILE):
            g = jnp.dot(u, wup_ref[:, c * UP_TILE:(c + 1) * UP_TILE], preferred_element_type=F32)
            v = jnp.dot(u, wup_ref[:, ff + c * UP_TILE:ff + (c + 1) * UP_TILE], preferred_element_type=F32)
            acts.append((g * jax.nn.sigmoid(g) * v).astype(BF16))
        if part + 1 < FFN_PARTS:
            u = _modulated(x_ref[_part(x_ref, part + 1, FFN_PARTS), :], mod_ref, prew_ref, sub)
        a = jnp.concatenate(acts, axis=1)
        y = jnp.dot(a, wdown_ref[...], preferred_element_type=F32)
        o_ref[rows, :] = x_ref[rows, :] + _rms(y) * gate


def _inproj_kernel(x_ref, mod_ref, prew_ref, wqkv_ref, wrest_ref, wgate_ref,
                   qkv1_ref, qkv4_ref, qkv16_ref, rest_ref, gate_ref, z_ref, y_ref):
    slabs = z_ref.shape[0]
    step = DILATIONS[1]
    assert DILATIONS[2] == step * step

    def by_residue(part, r):
        rows = _part(x_ref, part)
        n = rows.stop - rows.start
        per, per2 = n // step, n // (step * step)
        base = rows.start + r * per
        for c in range(slabs):
            lanes = slice(c * LANES, (c + 1) * LANES)
            picked = z_ref[c, pl.ds(rows.start + r, per, stride=step), :]
            y_ref[c, base:base + per, :] = picked
            qkv4_ref[r, part * per:(part + 1) * per, lanes] = picked.astype(BF16)
        for q in range(step):
            for c in range(slabs):
                lanes = slice(c * LANES, (c + 1) * LANES)
                qkv16_ref[r + step * q, part * per2:(part + 1) * per2, lanes] = (
                    y_ref[c, pl.ds(base + q, per2, stride=step), :].astype(BF16))

    n_rest = rest_ref.shape[1]
    groups = n_rest // UP_TILE
    residues = list(range(step))
    share = -(-len(residues) // groups)
    u = _modulated(x_ref[_part(x_ref, 0), :], mod_ref, prew_ref, 1)
    for part in range(ROW_PARTS):
        rows = _part(x_ref, part)
        z = jnp.dot(u, wqkv_ref[...], preferred_element_type=F32)
        qkv1_ref[0, rows, :] = z.astype(BF16)
        for c in range(slabs):
            z_ref[c, rows, :] = z[:, c * LANES:(c + 1) * LANES]
        gate_ref[rows, :] = jnp.dot(u, wgate_ref[...], preferred_element_type=F32)
        for k in range(groups):
            cols = slice(k * UP_TILE, (k + 1) * UP_TILE)
            rest_ref[rows, cols] = jnp.dot(u, wrest_ref[:, cols], preferred_element_type=F32).astype(BF16)
            for r in residues[k * share:(k + 1) * share]:
                by_residue(part, r)
        if part + 1 < ROW_PARTS:
            u = _modulated(x_ref[_part(x_ref, part + 1), :], mod_ref, prew_ref, 1)


def _outproj_kernel(x_ref, att_ref, pool_ref, hm_ref, mod_ref, postw_ref, wa_ref, wp_ref, wh_ref, o_ref):
    gate = postw_ref[1:2, :] * mod_ref[5:6, :]
    ys = []
    for part in range(ROW_PARTS):
        rows = _part(x_ref, part)
        y = jnp.dot(att_ref[rows, :], wa_ref[...], preferred_element_type=F32)
        y = y + jnp.dot(pool_ref[rows, :], wp_ref[...], preferred_element_type=F32)
        ys.append(y + jnp.dot(hm_ref[rows, :], wh_ref[...], preferred_element_type=F32))
    for part in range(ROW_PARTS):
        rows = _part(x_ref, part)
        o_ref[rows, :] = x_ref[rows, :] + _rms(ys[part]) * gate


def _resident(shape, index):
    return pl.BlockSpec(shape, index, pipeline_mode=pl.Buffered(1))


def _row_specs(d, seq):
    tm = ROW_TILE
    per_batch = seq // tm
    return (pl.BlockSpec((tm, d), lambda i: (i, 0)),
            pl.BlockSpec((None, 9, d), lambda i: (i // per_batch, 0, 0)),
            pl.BlockSpec((3, d), lambda i: (0, 0)))


def _ffn(h, mod, pre_w, post_w, w_up, w_down, layer, slot, seq):
    t, d = h.shape
    ff = w_down.shape[2]
    rows, mods, gains = _row_specs(d, seq)
    return pl.pallas_call(
        functools.partial(_ffn_kernel, sub=2 * slot),
        out_shape=jax.ShapeDtypeStruct((t, d), F32),
        grid=(t // ROW_TILE,),
        in_specs=[
            rows, mods, gains, gains,
            _resident((None, None, d, 2 * ff), lambda i: (layer, slot, 0, 0)),
            _resident((None, None, ff, d), lambda i: (layer, slot, 0, 0)),
        ],
        out_specs=rows,
        compiler_params=_params(("parallel",)),
        name="ffn",
    )(h, mod, pre_w, post_w, w_up, w_down)


def _inproj(h, mod, pre_w, w_qkv, w_rest, w_gate, layer, batch, seq):
    t, d = h.shape
    tm = ROW_TILE
    per_batch = seq // tm
    n_qkv, n_rest, n_gate = w_qkv.shape[2], w_rest.shape[2], w_gate.shape[2]
    rows, mods, gains = _row_specs(d, seq)
    by_residue = lambda i: (i // per_batch, 0, i % per_batch, 0)
    qkv_shapes = [jax.ShapeDtypeStruct((batch, dil, seq // dil, n_qkv), BF16) for dil in DILATIONS]
    qkv_specs = [pl.BlockSpec((None, dil, tm // dil, n_qkv), by_residue) for dil in DILATIONS]
    return pl.pallas_call(
        _inproj_kernel,
        out_shape=(
            *qkv_shapes,
            jax.ShapeDtypeStruct((t, n_rest), BF16),
            jax.ShapeDtypeStruct((t, n_gate), F32),
        ),
        grid=(t // tm,),
        in_specs=[
            rows, mods, gains,
            _resident((None, d, n_qkv), lambda i: (layer, 0, 0)),
            _resident((None, d, n_rest), lambda i: (layer, 0, 0)),
            _resident((None, d, n_gate), lambda i: (layer, 0, 0)),
        ],
        out_specs=(
            *qkv_specs,
            pl.BlockSpec((tm, n_rest), lambda i: (i, 0)),
            pl.BlockSpec((tm, n_gate), lambda i: (i, 0)),
        ),
        scratch_shapes=[pltpu.VMEM((n_qkv // LANES, tm, LANES), F32)] * 2,
        compiler_params=_params(("parallel",)),
        name="inproj",
    )(h, mod, pre_w, w_qkv, w_rest, w_gate)


def _outproj(h, att, pool, hm, mod, post_w, w_out, layer, seq):
    t, d = h.shape
    tm = ROW_TILE
    rows, mods, gains = _row_specs(d, seq)
    widths = (att.shape[-1], pool.shape[-1], hm.shape[-1])
    w_specs = []
    off = 0
    for wdt in widths:
        w_specs.append(_resident((None, wdt, d), functools.partial(lambda i, blk: (layer, blk, 0), blk=off // wdt)))
        off += wdt
    return pl.pallas_call(
        _outproj_kernel,
        out_shape=jax.ShapeDtypeStruct((t, d), F32),
        grid=(t // tm,),
        in_specs=[
            rows,
            pl.BlockSpec((tm, widths[0]), lambda i: (i, 0)),
            pl.BlockSpec((tm, widths[1]), lambda i: (i, 0)),
            pl.BlockSpec((tm, widths[2]), lambda i: (i, 0)),
            mods, gains,
            *w_specs,
        ],
        out_specs=rows,
        compiler_params=_params(("parallel",)),
        name="outproj",
    )(h, att.reshape(t, widths[0]), pool.reshape(t, widths[1]), hm.reshape(t, widths[2]), mod, post_w,
      w_out, w_out, w_out)


QKV_W = 3 * ATT_WIDTH


def _attn_scores(q, kwin, bias, head_of_lane):
    q32 = q.astype(F32) * (ATT_HEAD_DIM ** -0.5)
    lhs = jnp.concatenate(
        [jnp.where(head_of_lane == h, q32, 0.0) for h in range(ATT_HEADS)], axis=0).astype(BF16)
    return lax.dot_general(lhs, kwin, (((1,), (1,)), ((), ())), preferred_element_type=F32) + bias


def _by_lane_half(x, low):
    nq = x.shape[0] // ATT_HEADS
    out = []
    for half in range(2):
        first = x[(2 * half) * nq:(2 * half + 1) * nq]
        second = x[(2 * half + 1) * nq:(2 * half + 2) * nq]
        if x.shape[1] != 1:
            first = first[:, half * LANES:(half + 1) * LANES]
            second = second[:, half * LANES:(half + 1) * LANES]
        out.append(jnp.where(low, first, second))
    return out


def _attn_kernel(c1_ref, h1_ref, c4_ref, h4_ref, c16_ref, h16_ref, o_ref, bias_ref, dst_ref, s_ref, p_ref):
    t = pl.program_id(1)
    nq = ATT_BLOCK
    items = ATT_TILE // nq
    head_of_lane = lax.broadcasted_iota(jnp.int32, (nq, ATT_WIDTH), 1) >> (ATT_HEAD_DIM.bit_length() - 1)
    low = lax.broadcasted_iota(jnp.int32, (nq, LANES), 1) < ATT_HEAD_DIM

    qi = lax.broadcasted_iota(jnp.int32, (ATT_HEADS * nq, 2 * nq), 0) & (nq - 1)
    ki = lax.broadcasted_iota(jnp.int32, (ATT_HEADS * nq, 2 * nq), 1)
    ok = (ki >= qi) & (ki <= qi + ATT_SPAN)
    bias_ref[0] = jnp.where(ok, 0.0, NEG)
    bias_ref[1] = jnp.where(ok & (ki >= nq), 0.0, NEG)
    qs = slice(0, ATT_WIDTH)
    ks = slice(ATT_WIDTH, 2 * ATT_WIDTH)
    vs = slice(2 * ATT_WIDTH, 3 * ATT_WIDTH)

    branches = tuple(zip((c1_ref, c4_ref, c16_ref), (h1_ref, h4_ref, h16_ref), DILATIONS))

    def split(dil, idx):
        blocks = items // dil
        return idx >> (blocks.bit_length() - 1), idx & (blocks - 1)

    def window(cur_ref, halo_ref, dil, idx, cols):
        r, i = split(dil, idx)
        own = cur_ref[r, pl.ds(pl.multiple_of(i * nq, nq), nq), cols]
        if items == dil:
            prev = halo_ref[r, :, cols]
        else:
            prev0 = pl.multiple_of(jnp.maximum(i - 1, 0) * nq, nq)
            prev = jnp.where(i == 0, halo_ref[r, :, cols], cur_ref[r, pl.ds(prev0, nq), cols])
        return jnp.concatenate([prev, own], axis=0)

    def put(branch, quantity, halves, dil, idx):
        r, i = split(dil, idx)
        start = i * (nq * dil) + r
        rows = pl.ds(start, nq) if dil == 1 else pl.ds(start, nq, stride=dil)
        for half, piece in enumerate(halves):
            dst_ref[branch, quantity, half, rows, :] = piece

    for branch in range(len(branches)):
        s_ref[branch, 1] = jnp.zeros(s_ref.shape[2:], s_ref.dtype)
        p_ref[branch, 1] = jnp.zeros(p_ref.shape[2:], p_ref.dtype)

    def trip(j, slot):
        other = 1 - slot
        item_a = jnp.minimum(j, items - 1)
        item_b = jnp.clip(j - 1, 0, items - 1)
        item_c = jnp.clip(j - 2, 0, items - 1)

        for branch, (cur_ref, halo_ref, dil) in enumerate(branches):
            pv = jnp.dot(p_ref[branch, other], window(cur_ref, halo_ref, dil, item_c, vs),
                         preferred_element_type=F32)
            put(branch, 0, _by_lane_half(pv, low), dil, item_c)

        for branch, (cur_ref, halo_ref, dil) in enumerate(branches):
            s = s_ref[branch, other]
            m = jnp.max(s, axis=-1, keepdims=True)
            p = jnp.exp(s - m)
            l = jnp.sum(p, axis=-1, keepdims=True)
            p_ref[branch, slot] = p.astype(BF16)
            put(branch, 1, _by_lane_half(m, low), dil, item_b)
            put(branch, 2, _by_lane_half(l, low), dil, item_b)

        for branch, (cur_ref, halo_ref, dil) in enumerate(branches):
            r, i = split(dil, item_a)
            no_past = jnp.where((i == 0) & (t == 0), 1, 0)
            q = cur_ref[r, pl.ds(pl.multiple_of(i * nq, nq), nq), qs]
            s_ref[branch, slot] = _attn_scores(q, window(cur_ref, halo_ref, dil, item_a, ks),
                                               bias_ref[no_past], head_of_lane)

    def body(pair, carry):
        trip(2 * pair, 0)
        trip(2 * pair + 1, 1)
        return carry
    lax.fori_loop(0, (items + 2) // 2, body, 0)

    rows = ATT_BLOCK

    def merge(c, carry):
        sl = pl.ds(pl.multiple_of(c * rows, rows), rows)
        for half in range(2):
            maxes = [dst_ref[branch, 1, half, sl, :] for branch in range(len(branches))]
            top = functools.reduce(jnp.maximum, maxes)
            num = 0.0
            den = 0.0
            for branch, m_b in enumerate(maxes):
                e = jnp.exp(m_b - top)
                num = num + e * dst_ref[branch, 0, half, sl, :]
                den = den + e * dst_ref[branch, 2, half, sl, :]
            o_ref[sl, half * LANES:(half + 1) * LANES] = (num / den).astype(o_ref.dtype)
        return carry
    lax.fori_loop(0, ATT_TILE // rows, merge, 0)


def _attention(qkv_by_dilation, batch, seq):
    tiles = seq // ATT_TILE
    in_specs = []
    operands = []
    for dil, view in zip(DILATIONS, qkv_by_dilation):
        cur_rows = ATT_TILE // dil
        per_tile = cur_rows // ATT_BLOCK
        in_specs.append(pl.BlockSpec((None, dil, cur_rows, QKV_W), lambda b, t: (b, 0, t, 0)))
        in_specs.append(pl.BlockSpec(
            (None, dil, ATT_BLOCK, QKV_W),
            functools.partial(lambda b, t, per_tile: (b, 0, jnp.maximum(t * per_tile - 1, 0), 0), per_tile=per_tile)))
        operands += [view, view]
    stacked = (ATT_HEADS * ATT_BLOCK, 2 * ATT_BLOCK)
    return pl.pallas_call(
        _attn_kernel,
        out_shape=jax.ShapeDtypeStruct((batch, seq, ATT_WIDTH), BF16),
        grid=(batch, tiles),
        in_specs=in_specs,
        out_specs=pl.BlockSpec((None, ATT_TILE, ATT_WIDTH), lambda b, t: (b, t, 0)),
        scratch_shapes=[
            pltpu.VMEM((2,) + stacked, F32),
            pltpu.VMEM((len(DILATIONS), 3, 2, ATT_TILE, LANES), F32),
            pltpu.VMEM((len(DILATIONS), 2) + stacked, F32),
            pltpu.VMEM((len(DILATIONS), 2) + stacked, BF16),
        ],
        compiler_params=_params(("parallel", "parallel")),
        name="dilated_attention",
    )(*operands)


def _split3(x):
    hi = x.astype(BF16)
    r1 = x - hi.astype(F32)
    mid = r1.astype(BF16)
    lo = (r1 - mid.astype(F32)).astype(BF16)
    return hi, mid, lo


def _log_sigmoid(x):
    return -(jnp.maximum(-x, 0.0) + jnp.log1p(jnp.exp(-jnp.abs(x))))


def _mix_kernel(rest_ref, gate_ref, poolw_ref, pscale_ref, convw_ref, convb_ref, wqk_ref, wv_ref,
                gbias_ref, normw_ref, skip_ref, pool_ref, hm_ref,
                phist_ref, chist_ref, cstate_ref, mstate_ref, band_ref, shift_ref, upper_ref):
    s_idx = pl.program_id(1)
    n = SEQ_CHUNK
    dh = MLSTM_HEAD_DIM
    heads = MLSTM_HEADS
    group = range(SEQ_GROUP)

    @pl.when(s_idx == 0)
    def _():
        phist_ref[...] = jnp.zeros_like(phist_ref)
        chist_ref[...] = jnp.zeros_like(chist_ref)
        cstate_ref[...] = jnp.zeros_like(cstate_ref)
        mstate_ref[...] = jnp.zeros_like(mstate_ref)
        t_row = lax.broadcasted_iota(jnp.int32, (n, HIST + n), 0) + HIST
        col = lax.broadcasted_iota(jnp.int32, (n, HIST + n), 1)
        for g, wlen in enumerate(POOL_WINDOWS):
            band_ref[g * n:(g + 1) * n, :] = jnp.where((col <= t_row) & (col > t_row - wlen), 1.0, 0.0).astype(BF16)
        own_row = lax.broadcasted_iota(jnp.int32, (n, n), 0)
        own_col = lax.broadcasted_iota(jnp.int32, (n, n), 1)
        for back in range(1, MLSTM_CONV):
            shift_ref[(back - 1) * n:back * n, :] = jnp.where(own_col == own_row - back, 1.0, 0.0).astype(BF16)
        src = lax.broadcasted_iota(jnp.int32, (n, n), 0)
        dst = lax.broadcasted_iota(jnp.int32, (n, n), 1)
        upper_ref[...] = jnp.where(src <= dst, 1.0, 0.0).astype(BF16)

    lane = lax.broadcasted_iota(jnp.int32, (n, POOL_WIDTH), 1)
    pool_group = lane >> (POOL_GROUP_DIM.bit_length() - 1)
    pos = lax.broadcasted_iota(jnp.int32, (n, POOL_WIDTH), 0) + s_idx * n
    win_len = jnp.full((n, POOL_WIDTH), POOL_WINDOWS[0], jnp.int32)
    for g, wlen in enumerate(POOL_WINDOWS):
        win_len = jnp.where(pool_group == g, wlen, win_len)
    count = jnp.minimum(pos + 1, win_len).astype(F32)
    row = lax.broadcasted_iota(jnp.int32, (n, n), 0)
    col = lax.broadcasted_iota(jnp.int32, (n, n), 1)
    causal = col <= row
    lane8 = lax.broadcasted_iota(jnp.int32, (SUBLANES, n), 1)
    row8 = lax.broadcasted_iota(jnp.int32, (SUBLANES, n), 0)
    row_tile = lax.broadcasted_iota(jnp.int32, (SUBLANES, MLSTM_WIDTH), 0)
    ones = jnp.ones((n, dh), F32)

    def window_delta(b):
        p_b = rest_ref[b, :, 0:POOL_WIDTH]
        ext = jnp.concatenate([phist_ref[b].astype(BF16), p_b], axis=0)
        phist_ref[b] = p_b[n - HIST:, :].astype(F32)
        sums = jnp.dot(band_ref[...], ext, preferred_element_type=F32)
        win_sum = sums[0:n]
        for g in range(1, len(POOL_WINDOWS)):
            win_sum = jnp.where(pool_group == g, sums[g * n:(g + 1) * n], win_sum)
        return (win_sum / count - p_b.astype(F32)).astype(BF16)

    def front(b):
        xm_b = rest_ref[b, :, POOL_WIDTH:POOL_WIDTH + MLSTM_WIDTH]
        tail = chist_ref[b]
        chist_ref[b] = xm_b[n - SUBLANES:, :].astype(F32)
        shifted = jnp.dot(shift_ref[...], xm_b, preferred_element_type=F32)
        conv = xm_b.astype(F32) * convw_ref[MLSTM_CONV - 1:MLSTM_CONV, :] + convb_ref[...]
        head_rows = jnp.zeros((SUBLANES, MLSTM_WIDTH), F32)
        for back in range(1, MLSTM_CONV):
            tap = MLSTM_CONV - 1 - back
            conv = conv + shifted[(back - 1) * n:back * n] * convw_ref[tap:tap + 1, :]
            head_rows = head_rows + jnp.where(row_tile < back, pltpu.roll(tail, back, 0), 0.0) * convw_ref[tap:tap + 1, :]
        conv = conv + jnp.concatenate([head_rows, jnp.zeros((n - SUBLANES, MLSTM_WIDTH), F32)], axis=0)
        xc = conv * jax.nn.sigmoid(conv)
        return xc, xc.astype(BF16), xm_b

    def gate_rows(b):
        gates_t = (gate_ref[b] + gbias_ref[...]).T
        i_rows = gates_t[0:SUBLANES, :]
        return i_rows, _log_sigmoid(pltpu.roll(i_rows, SUBLANES - heads, 0))

    def gate_stats(b, i_rows, bcum):
        x_ib = i_rows - bcum
        b_last = bcum[:, n - 1:n]
        m_prev = mstate_ref[b, :, 0:1]
        a = b_last + x_ib
        m_new = jnp.maximum(b_last + m_prev, jnp.max(a, axis=1, keepdims=True))
        decay = jnp.exp(b_last + m_prev - m_new)
        w_state = jnp.exp(a - m_new)
        mstate_ref[b] = jnp.broadcast_to(m_new, mstate_ref.shape[1:])
        prefix_max = x_ib
        shift = 1
        while shift < n:
            prefix_max = jnp.maximum(prefix_max, jnp.where(lane8 >= shift, pltpu.roll(prefix_max, shift, 1), NEG))
            shift *= 2
        inter_log = bcum + m_prev
        m_t = jnp.maximum(inter_log, bcum + prefix_max)
        packed = jnp.concatenate(
            [jnp.where(row8 < heads, bcum - m_t, pltpu.roll(w_state, heads, 0)),
             jnp.where(row8 < heads, jnp.exp(inter_log - m_t), pltpu.roll(jnp.exp(-m_t), heads, 0)),
             jnp.zeros((LANES - 2 * SUBLANES, n), F32)], axis=0)
        return x_ib, packed.T, decay

    def head(b, h, qk, v, xc, x_ib, cols, decay):
        hs = slice(h * dh, (h + 1) * dh)
        q = (qk[:, :dh] * (dh ** -0.5)).astype(BF16)
        k = qk[:, dh:].astype(BF16)
        v_aug = jnp.concatenate([v, ones], axis=1)

        s = lax.dot_general(q, k, (((1,), (1,)), ((), ())), preferred_element_type=F32)
        log_d = jnp.where(causal, cols[:, h:h + 1] + x_ib[h:h + 1, :], NEG)
        sw = (s * jnp.exp(log_d)).astype(BF16)
        c_prev = cstate_ref[b, h]
        out = cols[:, 2 * heads + h:2 * heads + h + 1] * jnp.dot(q, c_prev.astype(BF16), preferred_element_type=F32) \
            + jnp.dot(sw, v_aug.astype(BF16), preferred_element_type=F32)
        hh = out[:, :dh] / jnp.maximum(jnp.abs(out[:, dh:]), cols[:, 3 * heads + h:3 * heads + h + 1])

        wv_state = (cols[:, heads + h:heads + h + 1] * v_aug).astype(BF16)
        d_c = lax.dot_general(k, wv_state, (((0,), (0,)), ((), ())), preferred_element_type=F32)
        cstate_ref[b, h] = decay[h:h + 1, :] * c_prev + d_c

        og = rest_ref[b, :, POOL_WIDTH + MLSTM_WIDTH + h * dh:POOL_WIDTH + MLSTM_WIDTH + (h + 1) * dh].astype(F32)
        hh = hh * jax.nn.sigmoid(og)
        hh = _rms(hh) * normw_ref[:, hs]
        hm_ref[b, :, hs] = (hh + skip_ref[:, hs] * xc[:, hs]).astype(hm_ref.dtype)

    def rows_of(stacked, b, rows):
        return stacked[b * rows:(b + 1) * rows]

    raw, deltas = [], []
    for b in group:
        deltas.append(window_delta(b))
        raw.append(gate_rows(b))
    pooled = jnp.dot(jnp.concatenate(deltas, axis=0), poolw_ref[...], preferred_element_type=F32) * pscale_ref[...]
    for b in group:
        pool_ref[b] = rows_of(pooled, b, n).astype(pool_ref.dtype)
    parts = [part for _, logf in raw for part in _split3(logf)]
    cums = jnp.dot(jnp.concatenate(parts, axis=0), upper_ref[...], preferred_element_type=F32)
    gated, fronts = [], []
    for b in group:
        fronts.append(front(b))
        piece = rows_of(cums, b, 3 * SUBLANES)
        bcum = piece[0:SUBLANES] + piece[SUBLANES:2 * SUBLANES] + piece[2 * SUBLANES:3 * SUBLANES]
        gated.append(gate_stats(b, raw[b][0], bcum))

    for h in range(heads):
        hs = slice(h * dh, (h + 1) * dh)
        qk_all = jnp.dot(jnp.concatenate([fronts[b][1][:, hs] for b in group], axis=0), wqk_ref[h],
                         preferred_element_type=F32)
        v_all = jnp.dot(jnp.concatenate([fronts[b][2][:, hs] for b in group], axis=0), wv_ref[h],
                        preferred_element_type=F32)
        for b in group:
            x_ib, cols, state_decay = gated[b]
            head(b, h, rows_of(qk_all, b, n), rows_of(v_all, b, n), fronts[b][0], x_ib, cols, state_decay)


def _mix(rest, gates, pool_bd, pool_scale, conv_w, conv_b, w_qk, w_v, gate_bias, norm_w, skip, batch, seq):
    n = SEQ_CHUNK
    sg = SEQ_GROUP
    n_rest = rest.shape[-1]
    rest = rest.reshape(batch, seq, n_rest)
    gates = gates.reshape(batch, seq, GATE_LANES)
    c2 = lambda b, s: (0, 0)
    c3 = lambda b, s: (0, 0, 0)
    chunk = lambda b, s: (b, s, 0)
    return pl.pallas_call(
        _mix_kernel,
        out_shape=(
            jax.ShapeDtypeStruct((batch, seq, POOL_WIDTH), BF16),
            jax.ShapeDtypeStruct((batch, seq, MLSTM_WIDTH), BF16),
        ),
        grid=(batch // sg, seq // n),
        in_specs=[
            pl.BlockSpec((sg, n, n_rest), chunk),
            pl.BlockSpec((sg, n, GATE_LANES), chunk),
            pl.BlockSpec(pool_bd.shape, c2),
            pl.BlockSpec(pool_scale.shape, c2),
            pl.BlockSpec(conv_w.shape, c2),
            pl.BlockSpec(conv_b.shape, c2),
            pl.BlockSpec(w_qk.shape, c3),
            pl.BlockSpec(w_v.shape, c3),
            pl.BlockSpec(gate_bias.shape, c2),
            pl.BlockSpec(norm_w.shape, c2),
            pl.BlockSpec(skip.shape, c2),
        ],
        out_specs=(
            pl.BlockSpec((sg, n, POOL_WIDTH), chunk),
            pl.BlockSpec((sg, n, MLSTM_WIDTH), chunk),
        ),
        scratch_shapes=[
            pltpu.VMEM((sg, HIST, POOL_WIDTH), F32),
            pltpu.VMEM((sg, SUBLANES, MLSTM_WIDTH), F32),
            pltpu.VMEM((sg, MLSTM_HEADS, MLSTM_HEAD_DIM, 2 * MLSTM_HEAD_DIM), F32),
            pltpu.VMEM((sg, SUBLANES, GATE_LANES), F32),
            pltpu.VMEM((len(POOL_WINDOWS) * n, HIST + n), BF16),
            pltpu.VMEM(((MLSTM_CONV - 1) * n, n), BF16),
            pltpu.VMEM((n, n), BF16),
        ],
        compiler_params=_params(("parallel", "arbitrary")),
        name="pool_mlstm",
    )(rest, gates, pool_bd, pool_scale, conv_w, conv_b, w_qk, w_v, gate_bias, norm_w, skip)


def kernel(x, c, ada_w, ada_b, pre_norm_w, post_norm_w, ffn_up, ffn_down, mix_in_w, mix_out_w, pool_w, pool_scale,
           mlstm_conv_w, mlstm_conv_b, mlstm_qkv_w, mlstm_gate_b, mlstm_norm_w, mlstm_skip):
    batch, seq, d = x.shape
    depth = ada_w.shape[0]
    assert d == ATT_WIDTH + POOL_WIDTH + MLSTM_WIDTH
    assert seq % ATT_TILE == 0 and seq % ROW_TILE == 0 and seq % SEQ_CHUNK == 0 and batch % SEQ_GROUP == 0

    up_b = ffn_up.astype(BF16)
    down_b = ffn_down.astype(BF16)
    out_b = mix_out_w.astype(BF16)
    n_main = QKV_W + POOL_WIDTH + 2 * MLSTM_WIDTH
    w_qkv = mix_in_w[:, :, :QKV_W].astype(BF16)
    w_rest = mix_in_w[:, :, QKV_W:n_main].astype(BF16)
    w_gate = jnp.pad(mix_in_w[:, :, n_main:], ((0, 0), (0, 0), (0, GATE_LANES - 2 * MLSTM_HEADS))).astype(BF16)
    gate_bias = jnp.pad(mlstm_gate_b.reshape(depth, 1, 2 * MLSTM_HEADS),
                        ((0, 0), (0, 0), (0, GATE_LANES - 2 * MLSTM_HEADS)))
    groups = len(POOL_WINDOWS)
    eye = jnp.eye(groups, dtype=pool_w.dtype)
    pool_bd = (pool_w[:, :, :, None, :] * eye[None, :, None, :, None]).reshape(depth, POOL_WIDTH, POOL_WIDTH).astype(BF16)
    w_qk = jnp.concatenate([mlstm_qkv_w[:, 0], mlstm_qkv_w[:, 1]], axis=-1).astype(BF16)
    w_v = mlstm_qkv_w[:, 2].astype(BF16)

    mod = _adaln(c, ada_w, ada_b).reshape(depth, batch, 9, d)

    h = x.reshape(batch * seq, d)
    for l in range(depth):
        h = _ffn(h, mod[l], pre_norm_w[l], post_norm_w[l], up_b, down_b, l, 0, seq)
        *qkv, rest, gates = _inproj(h, mod[l], pre_norm_w[l], w_qkv, w_rest, w_gate, l, batch, seq)
        att = _attention(qkv, batch, seq)
        pool, hm = _mix(rest, gates, pool_bd[l], pool_scale[l].reshape(1, -1), mlstm_conv_w[l],
                        mlstm_conv_b[l].reshape(1, -1), w_qk[l], w_v[l], gate_bias[l],
                        mlstm_norm_w[l].reshape(1, -1), mlstm_skip[l].reshape(1, -1), batch, seq)
        h = _outproj(h, att, pool, hm, mod[l], post_norm_w[l], out_b, l, seq)
        h = _ffn(h, mod[l], pre_norm_w[l], post_norm_w[l], up_b, down_b, l, 1, seq)
    return h.reshape(batch, seq, d)
```

```python
import functools

import jax
import jax.numpy as jnp
from jax import lax
from jax.experimental import pallas as pl
from jax.experimental.pallas import tpu as pltpu

F32 = jnp.float32
BF16 = jnp.bfloat16

ATT_HEADS = 4
ATT_HEAD_DIM = 64
ATT_WIDTH = ATT_HEADS * ATT_HEAD_DIM
DILATIONS = (1, 4, 16)
ATT_SPAN = 128
POOL_WINDOWS = (2, 4, 8, 16)
POOL_GROUP_DIM = 64
POOL_WIDTH = len(POOL_WINDOWS) * POOL_GROUP_DIM
MLSTM_HEADS = 4
MLSTM_HEAD_DIM = 128
MLSTM_WIDTH = MLSTM_HEADS * MLSTM_HEAD_DIM
MLSTM_CONV = 4
MACARON_WEIGHT = 0.5
EPS = 1e-6
NEG = -1e30

LANES = 128
SUBLANES = 8
VMEM_LIMIT_BYTES = 56 * 1024 * 1024

ROW_TILE = 1024
ROW_PARTS = 2
FFN_PARTS = 2
UP_TILE = 256
ATT_TILE = 2048
ATT_BLOCK = 128
SEQ_CHUNK = 256
SEQ_GROUP = 4
HIST = 16
GATE_LANES = LANES


def _params(sem, **flags):
    return pltpu.CompilerParams(dimension_semantics=sem, vmem_limit_bytes=VMEM_LIMIT_BYTES, flags=flags or None)


def _rms(x):
    return x * lax.rsqrt(jnp.mean(x * x, axis=-1, keepdims=True) + EPS)


def _adaln_kernel(c_ref, w_ref, b_ref, o_ref):
    c = c_ref[...]
    c_act = (c * jax.nn.sigmoid(c)).astype(BF16)
    o_ref[...] = jnp.dot(c_act, w_ref[...].astype(BF16), preferred_element_type=F32) + b_ref[...]


def _adaln(c, ada_w, ada_b):
    depth, d, n = ada_w.shape
    b = c.shape[0]
    tn = n // 8
    return pl.pallas_call(
        _adaln_kernel,
        out_shape=jax.ShapeDtypeStruct((depth, b, n), F32),
        grid=(depth, n // tn),
        in_specs=[
            pl.BlockSpec((b, d), lambda l, j: (0, 0)),
            pl.BlockSpec((None, d, tn), lambda l, j: (l, 0, j)),
            pl.BlockSpec((None, 1, tn), lambda l, j: (l, 0, j)),
        ],
        out_specs=pl.BlockSpec((None, b, tn), lambda l, j: (l, 0, j)),
        compiler_params=_params(("parallel", "parallel")),
        name="adaln",
    )(c, ada_w, ada_b.reshape(depth, 1, n))


def _modulated(x, mod_ref, prew_ref, sub):
    scale = prew_ref[sub:sub + 1, :] * (1.0 + mod_ref[3 * sub + 1:3 * sub + 2, :])
    return (_rms(x) * scale + mod_ref[3 * sub:3 * sub + 1, :]).astype(BF16)


def _part(ref, part, parts=None):
    rows = ref.shape[0] // (parts or ROW_PARTS)
    return slice(part * rows, (part + 1) * rows)


def _ffn_kernel(x_ref, mod_ref, prew_ref, postw_ref, wup_ref, wdown_ref, o_ref, *, sub):
    gate = postw_ref[sub:sub + 1, :] * (MACARON_WEIGHT * mod_ref[3 * sub + 2:3 * sub + 3, :])
    u = _modulated(x_ref[_part(x_ref, 0, FFN_PARTS), :], mod_ref, prew_ref, sub)
    for part in range(FFN_PARTS):
        rows = _part(x_ref, part, FFN_PARTS)
        ff = wdown_ref.shape[0]
        acts = []
        for c in range(ff // UP_TILE):
            g = jnp.dot(u, wup_ref[:, c * UP_TILE:(c + 1) * UP_TILE], preferred_element_type=F32)
            v = jnp.dot(u, wup_ref[:, ff + c * UP_TILE:ff + (c + 1) * UP_TILE], preferred_element_type=F32)
            acts.append((g * jax.nn.sigmoid(g) * v).astype(BF16))
        if part + 1 < FFN_PARTS:
            u = _modulated(x_ref[_part(x_ref, part + 1, FFN_PARTS), :], mod_ref, prew_ref, sub)
        a = jnp.concatenate(acts, axis=1)
        y = jnp.dot(a, wdown_ref[...], preferred_element_type=F32)
        o_ref[rows, :] = x_ref[rows, :] + _rms(y) * gate


def _inproj_kernel(x_ref, mod_ref, prew_ref, wqkv_ref, wrest_ref, wgate_ref,
                   qkv1_ref, qkv4_ref, qkv16_ref, rest_ref, gate_ref, z_ref, y_ref):
    slabs = z_ref.shape[0]
    step = DILATIONS[1]
    assert DILATIONS[2] == step * step

    def by_residue(part, r):
        rows = _part(x_ref, part)
        n = rows.stop - rows.start
        per, per2 = n // step, n // (step * step)
        base = rows.start + r * per
        for c in range(slabs):
            lanes = slice(c * LANES, (c + 1) * LANES)
            picked = z_ref[c, pl.ds(rows.start + r, per, stride=step), :]
            y_ref[c, base:base + per, :] = picked
            qkv4_ref[r, part * per:(part + 1) * per, lanes] = picked.astype(BF16)
        for q in range(step):
            for c in range(slabs):
                lanes = slice(c * LANES, (c + 1) * LANES)
                qkv16_ref[r + step * q, part * per2:(part + 1) * per2, lanes] = (
                    y_ref[c, pl.ds(base + q, per2, stride=step), :].astype(BF16))

    n_rest = rest_ref.shape[1]
    groups = n_rest // UP_TILE
    residues = list(range(step))
    share = -(-len(residues) // groups)
    u = _modulated(x_ref[_part(x_ref, 0), :], mod_ref, prew_ref, 1)
    for part in range(ROW_PARTS):
        rows = _part(x_ref, part)
        z = jnp.dot(u, wqkv_ref[...], preferred_element_type=F32)
        qkv1_ref[0, rows, :] = z.astype(BF16)
        for c in range(slabs):
            z_ref[c, rows, :] = z[:, c * LANES:(c + 1) * LANES]
        gate_ref[rows, :] = jnp.dot(u, wgate_ref[...], preferred_element_type=F32)
        for k in range(groups):
            cols = slice(k * UP_TILE, (k + 1) * UP_TILE)
            rest_ref[rows, cols] = jnp.dot(u, wrest_ref[:, cols], preferred_element_type=F32).astype(BF16)
            for r in residues[k * share:(k + 1) * share]:
                by_residue(part, r)
        if part + 1 < ROW_PARTS:
            u = _modulated(x_ref[_part(x_ref, part + 1), :], mod_ref, prew_ref, 1)


def _outproj_kernel(x_ref, att_ref, pool_ref, hm_ref, mod_ref, postw_ref, wa_ref, wp_ref, wh_ref, o_ref):
    gate = postw_ref[1:2, :] * mod_ref[5:6, :]
    ys = []
    for part in range(ROW_PARTS):
        rows = _part(x_ref, part)
        y = jnp.dot(att_ref[rows, :], wa_ref[...], preferred_element_type=F32)
        y = y + jnp.dot(pool_ref[rows, :], wp_ref[...], preferred_element_type=F32)
        ys.append(y + jnp.dot(hm_ref[rows, :], wh_ref[...], preferred_element_type=F32))
    for part in range(ROW_PARTS):
        rows = _part(x_ref, part)
        o_ref[rows, :] = x_ref[rows, :] + _rms(ys[part]) * gate


def _outproj_ffn_kernel(x_ref, att_ref, pool_ref, hm_ref, mod_ref, prew_ref, postw_ref,
                        wa_ref, wp_ref, wh_ref, wup_ref, wdown_ref, o_ref):
    sub = 2
    mix_gate = postw_ref[1:2, :] * mod_ref[5:6, :]
    ffn_gate = postw_ref[sub:sub + 1, :] * (MACARON_WEIGHT * mod_ref[3 * sub + 2:3 * sub + 3, :])
    ff = wdown_ref.shape[0]

    def mixed(part):
        rows = _part(x_ref, part, FFN_PARTS)
        y = jnp.dot(att_ref[rows, :], wa_ref[...], preferred_element_type=F32)
        y = y + jnp.dot(pool_ref[rows, :], wp_ref[...], preferred_element_type=F32)
        y = y + jnp.dot(hm_ref[rows, :], wh_ref[...], preferred_element_type=F32)
        return x_ref[rows, :] + _rms(y) * mix_gate

    h = mixed(0)
    for part in range(FFN_PARTS):
        rows = _part(x_ref, part, FFN_PARTS)
        u = _modulated(h, mod_ref, prew_ref, sub)
        acts = []
        for c in range(ff // UP_TILE):
            g = jnp.dot(u, wup_ref[:, c * UP_TILE:(c + 1) * UP_TILE], preferred_element_type=F32)
            v = jnp.dot(u, wup_ref[:, ff + c * UP_TILE:ff + (c + 1) * UP_TILE], preferred_element_type=F32)
            acts.append((g * jax.nn.sigmoid(g) * v).astype(BF16))
        h_next = mixed(part + 1) if part + 1 < FFN_PARTS else None
        y = jnp.dot(jnp.concatenate(acts, axis=1), wdown_ref[...], preferred_element_type=F32)
        o_ref[rows, :] = h + _rms(y) * ffn_gate
        h = h_next


def _resident(shape, index):
    return pl.BlockSpec(shape, index, pipeline_mode=pl.Buffered(1))


def _row_specs(d, seq):
    tm = ROW_TILE
    per_batch = seq // tm
    return (pl.BlockSpec((tm, d), lambda i: (i, 0)),
            pl.BlockSpec((None, 9, d), lambda i: (i // per_batch, 0, 0)),
            pl.BlockSpec((3, d), lambda i: (0, 0)))


def _ffn(h, mod, pre_w, post_w, w_up, w_down, layer, slot, seq):
    t, d = h.shape
    ff = w_down.shape[2]
    rows, mods, gains = _row_specs(d, seq)
    return pl.pallas_call(
        functools.partial(_ffn_kernel, sub=2 * slot),
        out_shape=jax.ShapeDtypeStruct((t, d), F32),
        grid=(t // ROW_TILE,),
        in_specs=[
            rows, mods, gains, gains,
            _resident((None, None, d, 2 * ff), lambda i: (layer, slot, 0, 0)),
            _resident((None, None, ff, d), lambda i: (layer, slot, 0, 0)),
        ],
        out_specs=rows,
        compiler_params=_params(("parallel",)),
        name="ffn",
    )(h, mod, pre_w, post_w, w_up, w_down)


def _inproj(h, mod, pre_w, w_qkv, w_rest, w_gate, layer, batch, seq):
    t, d = h.shape
    tm = ROW_TILE
    per_batch = seq // tm
    n_qkv, n_rest, n_gate = w_qkv.shape[2], w_rest.shape[2], w_gate.shape[2]
    rows, mods, gains = _row_specs(d, seq)
    by_residue = lambda i: (i // per_batch, 0, i % per_batch, 0)
    qkv_shapes = [jax.ShapeDtypeStruct((batch, dil, seq // dil, n_qkv), BF16) for dil in DILATIONS]
    qkv_specs = [pl.BlockSpec((None, dil, tm // dil, n_qkv), by_residue) for dil in DILATIONS]
    return pl.pallas_call(
        _inproj_kernel,
        out_shape=(
            *qkv_shapes,
            jax.ShapeDtypeStruct((t, n_rest), BF16),
            jax.ShapeDtypeStruct((t, n_gate), F32),
        ),
        grid=(t // tm,),
        in_specs=[
            rows, mods, gains,
            _resident((None, d, n_qkv), lambda i: (layer, 0, 0)),
            _resident((None, d, n_rest), lambda i: (layer, 0, 0)),
            _resident((None, d, n_gate), lambda i: (layer, 0, 0)),
        ],
        out_specs=(
            *qkv_specs,
            pl.BlockSpec((tm, n_rest), lambda i: (i, 0)),
            pl.BlockSpec((tm, n_gate), lambda i: (i, 0)),
        ),
        scratch_shapes=[pltpu.VMEM((n_qkv // LANES, tm, LANES), F32)] * 2,
        compiler_params=_params(("parallel",)),
        name="inproj",
    )(h, mod, pre_w, w_qkv, w_rest, w_gate)


def _outproj(h, att, pool, hm, mod, post_w, w_out, layer, seq):
    t, d = h.shape
    tm = ROW_TILE
    rows, mods, gains = _row_specs(d, seq)
    widths = (att.shape[-1], pool.shape[-1], hm.shape[-1])
    w_specs = []
    off = 0
    for wdt in widths:
        w_specs.append(_resident((None, wdt, d), functools.partial(lambda i, blk: (layer, blk, 0), blk=off // wdt)))
        off += wdt
    return pl.pallas_call(
        _outproj_kernel,
        out_shape=jax.ShapeDtypeStruct((t, d), F32),
        grid=(t // tm,),
        in_specs=[
            rows,
            pl.BlockSpec((tm, widths[0]), lambda i: (i, 0)),
            pl.BlockSpec((tm, widths[1]), lambda i: (i, 0)),
            pl.BlockSpec((tm, widths[2]), lambda i: (i, 0)),
            mods, gains,
            *w_specs,
        ],
        out_specs=rows,
        compiler_params=_params(("parallel",)),
        name="outproj",
    )(h, att.reshape(t, widths[0]), pool.reshape(t, widths[1]), hm.reshape(t, widths[2]), mod, post_w,
      w_out, w_out, w_out)


def _outproj_ffn(h, att, pool, hm, mod, pre_w, post_w, w_out, w_up, w_down, layer, seq):
    t, d = h.shape
    ff = w_down.shape[2]
    tm = ROW_TILE
    rows, mods, gains = _row_specs(d, seq)
    widths = (att.shape[-1], pool.shape[-1], hm.shape[-1])
    w_specs = []
    off = 0
    for wdt in widths:
        w_specs.append(_resident((None, wdt, d), functools.partial(lambda i, blk: (layer, blk, 0), blk=off // wdt)))
        off += wdt
    return pl.pallas_call(
        _outproj_ffn_kernel,
        out_shape=jax.ShapeDtypeStruct((t, d), F32),
        grid=(t // tm,),
        in_specs=[
            rows,
            pl.BlockSpec((tm, widths[0]), lambda i: (i, 0)),
            pl.BlockSpec((tm, widths[1]), lambda i: (i, 0)),
            pl.BlockSpec((tm, widths[2]), lambda i: (i, 0)),
            mods, gains, gains,
            *w_specs,
            _resident((None, None, d, 2 * ff), lambda i: (layer, 1, 0, 0)),
            _resident((None, None, ff, d), lambda i: (layer, 1, 0, 0)),
        ],
        out_specs=rows,
        compiler_params=_params(("parallel",)),
        name="outproj_ffn",
    )(h, att.reshape(t, widths[0]), pool.reshape(t, widths[1]), hm.reshape(t, widths[2]), mod, pre_w, post_w,
      w_out, w_out, w_out, w_up, w_down)


QKV_W = 3 * ATT_WIDTH


def _attn_scores(q, kwin, bias, head_of_lane):
    q32 = q.astype(F32) * (ATT_HEAD_DIM ** -0.5)
    lhs = jnp.concatenate(
        [jnp.where(head_of_lane == h, q32, 0.0) for h in range(ATT_HEADS)], axis=0).astype(BF16)
    return lax.dot_general(lhs, kwin, (((1,), (1,)), ((), ())), preferred_element_type=F32) + bias


def _by_lane_half(x, low):
    nq = x.shape[0] // ATT_HEADS
    out = []
    for half in range(2):
        first = x[(2 * half) * nq:(2 * half + 1) * nq]
        second = x[(2 * half + 1) * nq:(2 * half + 2) * nq]
        if x.shape[1] != 1:
            first = first[:, half * LANES:(half + 1) * LANES]
            second = second[:, half * LANES:(half + 1) * LANES]
        out.append(jnp.where(low, first, second))
    return out


def _attn_kernel(c1_ref, h1_ref, c4_ref, h4_ref, c16_ref, h16_ref, o_ref, bias_ref, dst_ref, s_ref, p_ref):
    t = pl.program_id(1)
    nq = ATT_BLOCK
    items = ATT_TILE // nq
    head_of_lane = lax.broadcasted_iota(jnp.int32, (nq, ATT_WIDTH), 1) >> (ATT_HEAD_DIM.bit_length() - 1)
    low = lax.broadcasted_iota(jnp.int32, (nq, LANES), 1) < ATT_HEAD_DIM

    qi = lax.broadcasted_iota(jnp.int32, (ATT_HEADS * nq, 2 * nq), 0) & (nq - 1)
    ki = lax.broadcasted_iota(jnp.int32, (ATT_HEADS * nq, 2 * nq), 1)
    ok = (ki >= qi) & (ki <= qi + ATT_SPAN)
    bias_ref[0] = jnp.where(ok, 0.0, NEG)
    bias_ref[1] = jnp.where(ok & (ki >= nq), 0.0, NEG)
    qs = slice(0, ATT_WIDTH)
    ks = slice(ATT_WIDTH, 2 * ATT_WIDTH)
    vs = slice(2 * ATT_WIDTH, 3 * ATT_WIDTH)

    branches = tuple(zip((c1_ref, c4_ref, c16_ref), (h1_ref, h4_ref, h16_ref), DILATIONS))

    def split(dil, idx):
        blocks = items // dil
        return idx >> (blocks.bit_length() - 1), idx & (blocks - 1)

    def window(cur_ref, halo_ref, dil, idx, cols):
        r, i = split(dil, idx)
        own = cur_ref[r, pl.ds(pl.multiple_of(i * nq, nq), nq), cols]
        if items == dil:
            prev = halo_ref[r, :, cols]
        else:
            prev0 = pl.multiple_of(jnp.maximum(i - 1, 0) * nq, nq)
            prev = jnp.where(i == 0, halo_ref[r, :, cols], cur_ref[r, pl.ds(prev0, nq), cols])
        return jnp.concatenate([prev, own], axis=0)

    def put(branch, quantity, halves, dil, idx):
        r, i = split(dil, idx)
        start = i * (nq * dil) + r
        rows = pl.ds(start, nq) if dil == 1 else pl.ds(start, nq, stride=dil)
        for half, piece in enumerate(halves):
            dst_ref[branch, quantity, half, rows, :] = piece

    for branch in range(len(branches)):
        s_ref[branch, 1] = jnp.zeros(s_ref.shape[2:], s_ref.dtype)
        p_ref[branch, 1] = jnp.zeros(p_ref.shape[2:], p_ref.dtype)

    def trip(j, slot):
        other = 1 - slot
        item_a = jnp.minimum(j, items - 1)
        item_b = jnp.clip(j - 1, 0, items - 1)
        item_c = jnp.clip(j - 2, 0, items - 1)

        for branch, (cur_ref, halo_ref, dil) in enumerate(branches):
            pv = jnp.dot(p_ref[branch, other], window(cur_ref, halo_ref, dil, item_c, vs),
                         preferred_element_type=F32)
            put(branch, 0, _by_lane_half(pv, low), dil, item_c)

        for branch, (cur_ref, halo_ref, dil) in enumerate(branches):
            s = s_ref[branch, other]
            m = jnp.max(s, axis=-1, keepdims=True)
            p = jnp.exp(s - m)
            l = jnp.sum(p, axis=-1, keepdims=True)
            p_ref[branch, slot] = p.astype(BF16)
            put(branch, 1, _by_lane_half(m, low), dil, item_b)
            put(branch, 2, _by_lane_half(l, low), dil, item_b)

        for branch, (cur_ref, halo_ref, dil) in enumerate(branches):
            r, i = split(dil, item_a)
            no_past = jnp.where((i == 0) & (t == 0), 1, 0)
            q = cur_ref[r, pl.ds(pl.multiple_of(i * nq, nq), nq), qs]
            s_ref[branch, slot] = _attn_scores(q, window(cur_ref, halo_ref, dil, item_a, ks),
                                               bias_ref[no_past], head_of_lane)

    def body(pair, carry):
        trip(2 * pair, 0)
        trip(2 * pair + 1, 1)
        return carry
    lax.fori_loop(0, (items + 2) // 2, body, 0)

    rows = ATT_BLOCK

    def merge(c, carry):
        sl = pl.ds(pl.multiple_of(c * rows, rows), rows)
        for half in range(2):
            maxes = [dst_ref[branch, 1, half, sl, :] for branch in range(len(branches))]
            top = functools.reduce(jnp.maximum, maxes)
            num = 0.0
            den = 0.0
            for branch, m_b in enumerate(maxes):
                e = jnp.exp(m_b - top)
                num = num + e * dst_ref[branch, 0, half, sl, :]
                den = den + e * dst_ref[branch, 2, half, sl, :]
            o_ref[sl, half * LANES:(half + 1) * LANES] = (num / den).astype(o_ref.dtype)
        return carry
    lax.fori_loop(0, ATT_TILE // rows, merge, 0)


def _attention(qkv_by_dilation, batch, seq):
    tiles = seq // ATT_TILE
    in_specs = []
    operands = []
    for dil, view in zip(DILATIONS, qkv_by_dilation):
        cur_rows = ATT_TILE // dil
        per_tile = cur_rows // ATT_BLOCK
        in_specs.append(pl.BlockSpec((None, dil, cur_rows, QKV_W), lambda b, t: (b, 0, t, 0)))
        in_specs.append(pl.BlockSpec(
            (None, dil, ATT_BLOCK, QKV_W),
            functools.partial(lambda b, t, per_tile: (b, 0, jnp.maximum(t * per_tile - 1, 0), 0), per_tile=per_tile)))
        operands += [view, view]
    stacked = (ATT_HEADS * ATT_BLOCK, 2 * ATT_BLOCK)
    return pl.pallas_call(
        _attn_kernel,
        out_shape=jax.ShapeDtypeStruct((batch, seq, ATT_WIDTH), BF16),
        grid=(batch, tiles),
        in_specs=in_specs,
        out_specs=pl.BlockSpec((None, ATT_TILE, ATT_WIDTH), lambda b, t: (b, t, 0)),
        scratch_shapes=[
            pltpu.VMEM((2,) + stacked, F32),
            pltpu.VMEM((len(DILATIONS), 3, 2, ATT_TILE, LANES), F32),
            pltpu.VMEM((len(DILATIONS), 2) + stacked, F32),
            pltpu.VMEM((len(DILATIONS), 2) + stacked, BF16),
        ],
        compiler_params=_params(("parallel", "parallel")),
        name="dilated_attention",
    )(*operands)


def _split3(x):
    hi = x.astype(BF16)
    r1 = x - hi.astype(F32)
    mid = r1.astype(BF16)
    lo = (r1 - mid.astype(F32)).astype(BF16)
    return hi, mid, lo


def _log_sigmoid(x):
    return -(jnp.maximum(-x, 0.0) + jnp.log1p(jnp.exp(-jnp.abs(x))))


def _mix_kernel(rest_ref, gate_ref, poolw_ref, pscale_ref, convw_ref, convb_ref, wqk_ref, wv_ref,
                gbias_ref, normw_ref, skip_ref, pool_ref, hm_ref,
                phist_ref, chist_ref, cstate_ref, mstate_ref, band_ref, shift_ref, upper_ref):
    s_idx = pl.program_id(1)
    n = SEQ_CHUNK
    dh = MLSTM_HEAD_DIM
    heads = MLSTM_HEADS
    group = range(SEQ_GROUP)

    @pl.when(s_idx == 0)
    def _():
        phist_ref[...] = jnp.zeros_like(phist_ref)
        chist_ref[...] = jnp.zeros_like(chist_ref)
        cstate_ref[...] = jnp.zeros_like(cstate_ref)
        mstate_ref[...] = jnp.zeros_like(mstate_ref)
        t_row = lax.broadcasted_iota(jnp.int32, (n, HIST + n), 0) + HIST
        col = lax.broadcasted_iota(jnp.int32, (n, HIST + n), 1)
        for g, wlen in enumerate(POOL_WINDOWS):
            band_ref[g * n:(g + 1) * n, :] = jnp.where((col <= t_row) & (col > t_row - wlen), 1.0, 0.0).astype(BF16)
        own_row = lax.broadcasted_iota(jnp.int32, (n, n), 0)
        own_col = lax.broadcasted_iota(jnp.int32, (n, n), 1)
        for back in range(1, MLSTM_CONV):
            shift_ref[(back - 1) * n:back * n, :] = jnp.where(own_col == own_row - back, 1.0, 0.0).astype(BF16)
        src = lax.broadcasted_iota(jnp.int32, (n, n), 0)
        dst = lax.broadcasted_iota(jnp.int32, (n, n), 1)
        upper_ref[...] = jnp.where(src <= dst, 1.0, 0.0).astype(BF16)

    lane = lax.broadcasted_iota(jnp.int32, (n, POOL_WIDTH), 1)
    pool_group = lane >> (POOL_GROUP_DIM.bit_length() - 1)
    pos = lax.broadcasted_iota(jnp.int32, (n, POOL_WIDTH), 0) + s_idx * n
    win_len = jnp.full((n, POOL_WIDTH), POOL_WINDOWS[0], jnp.int32)
    for g, wlen in enumerate(POOL_WINDOWS):
        win_len = jnp.where(pool_group == g, wlen, win_len)
    count = jnp.minimum(pos + 1, win_len).astype(F32)
    row = lax.broadcasted_iota(jnp.int32, (n, n), 0)
    col = lax.broadcasted_iota(jnp.int32, (n, n), 1)
    causal = col <= row
    lane8 = lax.broadcasted_iota(jnp.int32, (SUBLANES, n), 1)
    row8 = lax.broadcasted_iota(jnp.int32, (SUBLANES, n), 0)
    row_tile = lax.broadcasted_iota(jnp.int32, (SUBLANES, MLSTM_WIDTH), 0)
    ones = jnp.ones((n, dh), F32)

    def window_delta(b):
        p_b = rest_ref[b, :, 0:POOL_WIDTH]
        ext = jnp.concatenate([phist_ref[b].astype(BF16), p_b], axis=0)
        phist_ref[b] = p_b[n - HIST:, :].astype(F32)
        sums = jnp.dot(band_ref[...], ext, preferred_element_type=F32)
        win_sum = sums[0:n]
        for g in range(1, len(POOL_WINDOWS)):
            win_sum = jnp.where(pool_group == g, sums[g * n:(g + 1) * n], win_sum)
        return (win_sum / count - p_b.astype(F32)).astype(BF16)

    def front(b):
        xm_b = rest_ref[b, :, POOL_WIDTH:POOL_WIDTH + MLSTM_WIDTH]
        tail = chist_ref[b]
        chist_ref[b] = xm_b[n - SUBLANES:, :].astype(F32)
        shifted = jnp.dot(shift_ref[...], xm_b, preferred_element_type=F32)
        conv = xm_b.astype(F32) * convw_ref[MLSTM_CONV - 1:MLSTM_CONV, :] + convb_ref[...]
        head_rows = jnp.zeros((SUBLANES, MLSTM_WIDTH), F32)
        for back in range(1, MLSTM_CONV):
            tap = MLSTM_CONV - 1 - back
            conv = conv + shifted[(back - 1) * n:back * n] * convw_ref[tap:tap + 1, :]
            head_rows = head_rows + jnp.where(row_tile < back, pltpu.roll(tail, back, 0), 0.0) * convw_ref[tap:tap + 1, :]
        conv = conv + jnp.concatenate([head_rows, jnp.zeros((n - SUBLANES, MLSTM_WIDTH), F32)], axis=0)
        xc = conv * jax.nn.sigmoid(conv)
        return xc, xc.astype(BF16), xm_b

    def gate_rows(b):
        gates_t = (gate_ref[b] + gbias_ref[...]).T
        i_rows = gates_t[0:SUBLANES, :]
        return i_rows, _log_sigmoid(pltpu.roll(i_rows, SUBLANES - heads, 0))

    def gate_stats(b, i_rows, bcum):
        x_ib = i_rows - bcum
        b_last = bcum[:, n - 1:n]
        m_prev = mstate_ref[b, :, 0:1]
        a = b_last + x_ib
        m_new = jnp.maximum(b_last + m_prev, jnp.max(a, axis=1, keepdims=True))
        decay = jnp.exp(b_last + m_prev - m_new)
        w_state = jnp.exp(a - m_new)
        mstate_ref[b] = jnp.broadcast_to(m_new, mstate_ref.shape[1:])
        prefix_max = x_ib
        shift = 1
        while shift < n:
            prefix_max = jnp.maximum(prefix_max, jnp.where(lane8 >= shift, pltpu.roll(prefix_max, shift, 1), NEG))
            shift *= 2
        inter_log = bcum + m_prev
        m_t = jnp.maximum(inter_log, bcum + prefix_max)
        packed = jnp.concatenate(
            [jnp.where(row8 < heads, bcum - m_t, pltpu.roll(w_state, heads, 0)),
             jnp.where(row8 < heads, jnp.exp(inter_log - m_t), pltpu.roll(jnp.exp(-m_t), heads, 0)),
             jnp.zeros((LANES - 2 * SUBLANES, n), F32)], axis=0)
        return x_ib, packed.T, decay

    def head(b, h, qk, v, xc, x_ib, cols, decay):
        hs = slice(h * dh, (h + 1) * dh)
        q = (qk[:, :dh] * (dh ** -0.5)).astype(BF16)
        k = qk[:, dh:].astype(BF16)
        v_aug = jnp.concatenate([v, ones], axis=1)

        s = lax.dot_general(q, k, (((1,), (1,)), ((), ())), preferred_element_type=F32)
        log_d = jnp.where(causal, cols[:, h:h + 1] + x_ib[h:h + 1, :], NEG)
        sw = (s * jnp.exp(log_d)).astype(BF16)
        c_prev = cstate_ref[b, h]
        out = cols[:, 2 * heads + h:2 * heads + h + 1] * jnp.dot(q, c_prev.astype(BF16), preferred_element_type=F32) \
            + jnp.dot(sw, v_aug.astype(BF16), preferred_element_type=F32)
        hh = out[:, :dh] / jnp.maximum(jnp.abs(out[:, dh:]), cols[:, 3 * heads + h:3 * heads + h + 1])

        wv_state = (cols[:, heads + h:heads + h + 1] * v_aug).astype(BF16)
        d_c = lax.dot_general(k, wv_state, (((0,), (0,)), ((), ())), preferred_element_type=F32)
        cstate_ref[b, h] = decay[h:h + 1, :] * c_prev + d_c

        og = rest_ref[b, :, POOL_WIDTH + MLSTM_WIDTH + h * dh:POOL_WIDTH + MLSTM_WIDTH + (h + 1) * dh].astype(F32)
        hh = hh * jax.nn.sigmoid(og)
        hh = _rms(hh) * normw_ref[:, hs]
        hm_ref[b, :, hs] = (hh + skip_ref[:, hs] * xc[:, hs]).astype(hm_ref.dtype)

    def rows_of(stacked, b, rows):
        return stacked[b * rows:(b + 1) * rows]

    raw, deltas = [], []
    for b in group:
        deltas.append(window_delta(b))
        raw.append(gate_rows(b))
    pooled = jnp.dot(jnp.concatenate(deltas, axis=0), poolw_ref[...], preferred_element_type=F32) * pscale_ref[...]
    for b in group:
        pool_ref[b] = rows_of(pooled, b, n).astype(pool_ref.dtype)
    parts = [part for _, logf in raw for part in _split3(logf)]
    cums = jnp.dot(jnp.concatenate(parts, axis=0), upper_ref[...], preferred_element_type=F32)
    gated, fronts = [], []
    for b in group:
        fronts.append(front(b))
        piece = rows_of(cums, b, 3 * SUBLANES)
        bcum = piece[0:SUBLANES] + piece[SUBLANES:2 * SUBLANES] + piece[2 * SUBLANES:3 * SUBLANES]
        gated.append(gate_stats(b, raw[b][0], bcum))

    for h in range(heads):
        hs = slice(h * dh, (h + 1) * dh)
        qk_all = jnp.dot(jnp.concatenate([fronts[b][1][:, hs] for b in group], axis=0), wqk_ref[h],
                         preferred_element_type=F32)
        v_all = jnp.dot(jnp.concatenate([fronts[b][2][:, hs] for b in group], axis=0), wv_ref[h],
                        preferred_element_type=F32)
        for b in group:
            x_ib, cols, state_decay = gated[b]
            head(b, h, rows_of(qk_all, b, n), rows_of(v_all, b, n), fronts[b][0], x_ib, cols, state_decay)


def _mix(rest, gates, pool_bd, pool_scale, conv_w, conv_b, w_qk, w_v, gate_bias, norm_w, skip, batch, seq):
    n = SEQ_CHUNK
    sg = SEQ_GROUP
    n_rest = rest.shape[-1]
    rest = rest.reshape(batch, seq, n_rest)
    gates = gates.reshape(batch, seq, GATE_LANES)
    c2 = lambda b, s: (0, 0)
    c3 = lambda b, s: (0, 0, 0)
    chunk = lambda b, s: (b, s, 0)
    return pl.pallas_call(
        _mix_kernel,
        out_shape=(
            jax.ShapeDtypeStruct((batch, seq, POOL_WIDTH), BF16),
            jax.ShapeDtypeStruct((batch, seq, MLSTM_WIDTH), BF16),
        ),
        grid=(batch // sg, seq // n),
        in_specs=[
            pl.BlockSpec((sg, n, n_rest), chunk),
            pl.BlockSpec((sg, n, GATE_LANES), chunk),
            pl.BlockSpec(pool_bd.shape, c2),
            pl.BlockSpec(pool_scale.shape, c2),
            pl.BlockSpec(conv_w.shape, c2),
            pl.BlockSpec(conv_b.shape, c2),
            pl.BlockSpec(w_qk.shape, c3),
            pl.BlockSpec(w_v.shape, c3),
            pl.BlockSpec(gate_bias.shape, c2),
            pl.BlockSpec(norm_w.shape, c2),
            pl.BlockSpec(skip.shape, c2),
        ],
        out_specs=(
            pl.BlockSpec((sg, n, POOL_WIDTH), chunk),
            pl.BlockSpec((sg, n, MLSTM_WIDTH), chunk),
        ),
        scratch_shapes=[
            pltpu.VMEM((sg, HIST, POOL_WIDTH), F32),
            pltpu.VMEM((sg, SUBLANES, MLSTM_WIDTH), F32),
            pltpu.VMEM((sg, MLSTM_HEADS, MLSTM_HEAD_DIM, 2 * MLSTM_HEAD_DIM), F32),
            pltpu.VMEM((sg, SUBLANES, GATE_LANES), F32),
            pltpu.VMEM((len(POOL_WINDOWS) * n, HIST + n), BF16),
            pltpu.VMEM(((MLSTM_CONV - 1) * n, n), BF16),
            pltpu.VMEM((n, n), BF16),
        ],
        compiler_params=_params(("parallel", "arbitrary")),
        name="pool_mlstm",
    )(rest, gates, pool_bd, pool_scale, conv_w, conv_b, w_qk, w_v, gate_bias, norm_w, skip)


def kernel(x, c, ada_w, ada_b, pre_norm_w, post_norm_w, ffn_up, ffn_down, mix_in_w, mix_out_w, pool_w, pool_scale,
           mlstm_conv_w, mlstm_conv_b, mlstm_qkv_w, mlstm_gate_b, mlstm_norm_w, mlstm_skip):
    batch, seq, d = x.shape
    depth = ada_w.shape[0]
    assert d == ATT_WIDTH + POOL_WIDTH + MLSTM_WIDTH
    assert seq % ATT_TILE == 0 and seq % ROW_TILE == 0 and seq % SEQ_CHUNK == 0 and batch % SEQ_GROUP == 0

    up_b = ffn_up.astype(BF16)
    down_b = ffn_down.astype(BF16)
    out_b = mix_out_w.astype(BF16)
    n_main = QKV_W + POOL_WIDTH + 2 * MLSTM_WIDTH
    w_qkv = mix_in_w[:, :, :QKV_W].astype(BF16)
    w_rest = mix_in_w[:, :, QKV_W:n_main].astype(BF16)
    w_gate = jnp.pad(mix_in_w[:, :, n_main:], ((0, 0), (0, 0), (0, GATE_LANES - 2 * MLSTM_HEADS))).astype(BF16)
    gate_bias = jnp.pad(mlstm_gate_b.reshape(depth, 1, 2 * MLSTM_HEADS),
                        ((0, 0), (0, 0), (0, GATE_LANES - 2 * MLSTM_HEADS)))
    groups = len(POOL_WINDOWS)
    eye = jnp.eye(groups, dtype=pool_w.dtype)
    pool_bd = (pool_w[:, :, :, None, :] * eye[None, :, None, :, None]).reshape(depth, POOL_WIDTH, POOL_WIDTH).astype(BF16)
    w_qk = jnp.concatenate([mlstm_qkv_w[:, 0], mlstm_qkv_w[:, 1]], axis=-1).astype(BF16)
    w_v = mlstm_qkv_w[:, 2].astype(BF16)

    mod = _adaln(c, ada_w, ada_b).reshape(depth, batch, 9, d)

    h = x.reshape(batch * seq, d)
    for l in range(depth):
        h = _ffn(h, mod[l], pre_norm_w[l], post_norm_w[l], up_b, down_b, l, 0, seq)
        *qkv, rest, gates = _inproj(h, mod[l], pre_norm_w[l], w_qkv, w_rest, w_gate, l, batch, seq)
        att = _attention(qkv, batch, seq)
        pool, hm = _mix(rest, gates, pool_bd[l], pool_scale[l].reshape(1, -1), mlstm_conv_w[l],
                        mlstm_conv_b[l].reshape(1, -1), w_qk[l], w_v[l], gate_bias[l],
                        mlstm_norm_w[l].reshape(1, -1), mlstm_skip[l].reshape(1, -1), batch, seq)
        h = _outproj_ffn(h, att, pool, hm, mod[l], pre_norm_w[l], post_norm_w[l], out_b, up_b, down_b, l, seq)
    return h.reshape(batch, seq, d)
```

```python
import functools

import jax
import jax.numpy as jnp
from jax import lax
from jax.experimental import pallas as pl
from jax.experimental.pallas import tpu as pltpu

F32 = jnp.float32
BF16 = jnp.bfloat16

ATT_HEADS = 4
ATT_HEAD_DIM = 64
ATT_WIDTH = ATT_HEADS * ATT_HEAD_DIM
DILATIONS = (1, 4, 16)
ATT_SPAN = 128
POOL_WINDOWS = (2, 4, 8, 16)
POOL_GROUP_DIM = 64
POOL_WIDTH = len(POOL_WINDOWS) * POOL_GROUP_DIM
MLSTM_HEADS = 4
MLSTM_HEAD_DIM = 128
MLSTM_WIDTH = MLSTM_HEADS * MLSTM_HEAD_DIM
MLSTM_CONV = 4
MACARON_WEIGHT = 0.5
EPS = 1e-6
NEG = -1e30

LANES = 128
SUBLANES = 8
VMEM_LIMIT_BYTES = 56 * 1024 * 1024

ROW_TILE = 1024
ROW_PARTS = 2
FFN_PARTS = 2
UP_TILE = 256
ATT_TILE = 2048
ATT_BLOCK = 128
SEQ_CHUNK = 256
SEQ_GROUP = 4
HIST = 16
GATE_LANES = LANES


def _params(sem, **flags):
    return pltpu.CompilerParams(dimension_semantics=sem, vmem_limit_bytes=VMEM_LIMIT_BYTES, flags=flags or None)


def _rms(x):
    return x * lax.rsqrt(jnp.mean(x * x, axis=-1, keepdims=True) + EPS)


def _adaln_kernel(c_ref, w_ref, b_ref, o_ref):
    c = c_ref[...]
    c_act = (c * jax.nn.sigmoid(c)).astype(BF16)
    o_ref[...] = jnp.dot(c_act, w_ref[...].astype(BF16), preferred_element_type=F32) + b_ref[...]


def _adaln(c, ada_w, ada_b):
    depth, d, n = ada_w.shape
    b = c.shape[0]
    tn = n // 8
    return pl.pallas_call(
        _adaln_kernel,
        out_shape=jax.ShapeDtypeStruct((depth, b, n), F32),
        grid=(depth, n // tn),
        in_specs=[
            pl.BlockSpec((b, d), lambda l, j: (0, 0)),
            pl.BlockSpec((None, d, tn), lambda l, j: (l, 0, j)),
            pl.BlockSpec((None, 1, tn), lambda l, j: (l, 0, j)),
        ],
        out_specs=pl.BlockSpec((None, b, tn), lambda l, j: (l, 0, j)),
        compiler_params=_params(("parallel", "parallel")),
        name="adaln",
    )(c, ada_w, ada_b.reshape(depth, 1, n))


def _modulated(x, mod_ref, prew_ref, sub):
    scale = prew_ref[sub:sub + 1, :] * (1.0 + mod_ref[3 * sub + 1:3 * sub + 2, :])
    return (_rms(x) * scale + mod_ref[3 * sub:3 * sub + 1, :]).astype(BF16)


def _part(ref, part, parts=None):
    rows = ref.shape[0] // (parts or ROW_PARTS)
    return slice(part * rows, (part + 1) * rows)


def _ffn_kernel(x_ref, mod_ref, prew_ref, postw_ref, wup_ref, wdown_ref, o_ref, *, sub):
    gate = postw_ref[sub:sub + 1, :] * (MACARON_WEIGHT * mod_ref[3 * sub + 2:3 * sub + 3, :])
    u = _modulated(x_ref[_part(x_ref, 0, FFN_PARTS), :], mod_ref, prew_ref, sub)
    for part in range(FFN_PARTS):
        rows = _part(x_ref, part, FFN_PARTS)
        ff = wdown_ref.shape[0]
        acts = []
        for c in range(ff // UP_TILE):
            g = jnp.dot(u, wup_ref[:, c * UP_TILE:(c + 1) * UP_TILE], preferred_element_type=F32)
            v = jnp.dot(u, wup_ref[:, ff + c * UP_TILE:ff + (c + 1) * UP_TILE], preferred_element_type=F32)
            acts.append((g * jax.nn.sigmoid(g) * v).astype(BF16))
        if part + 1 < FFN_PARTS:
            u = _modulated(x_ref[_part(x_ref, part + 1, FFN_PARTS), :], mod_ref, prew_ref, sub)
        a = jnp.concatenate(acts, axis=1)
        y = jnp.dot(a, wdown_ref[...], preferred_element_type=F32)
        o_ref[rows, :] = x_ref[rows, :] + _rms(y) * gate


def _inproj_kernel(x_ref, mod_ref, prew_ref, wqkv_ref, wrest_ref, wgate_ref,
                   qkv1_ref, qkv4_ref, qkv16_ref, rest_ref, gate_ref, z_ref, y_ref):
    slabs = z_ref.shape[0]
    step = DILATIONS[1]
    assert DILATIONS[2] == step * step

    def by_residue(part, r):
        rows = _part(x_ref, part)
        n = rows.stop - rows.start
        per, per2 = n // step, n // (step * step)
        base = rows.start + r * per
        for c in range(slabs):
            lanes = slice(c * LANES, (c + 1) * LANES)
            picked = z_ref[c, pl.ds(rows.start + r, per, stride=step), :]
            y_ref[c, base:base + per, :] = picked
            qkv4_ref[r, part * per:(part + 1) * per, lanes] = picked.astype(BF16)
        for q in range(step):
            for c in range(slabs):
                lanes = slice(c * LANES, (c + 1) * LANES)
                qkv16_ref[r + step * q, part * per2:(part + 1) * per2, lanes] = (
                    y_ref[c, pl.ds(base + q, per2, stride=step), :].astype(BF16))

    n_rest = rest_ref.shape[1]
    groups = n_rest // UP_TILE
    residues = list(range(step))
    share = -(-len(residues) // groups)
    u = _modulated(x_ref[_part(x_ref, 0), :], mod_ref, prew_ref, 1)
    for part in range(ROW_PARTS):
        rows = _part(x_ref, part)
        z = jnp.dot(u, wqkv_ref[...], preferred_element_type=F32)
        qkv1_ref[0, rows, :] = z.astype(BF16)
        for c in range(slabs):
            z_ref[c, rows, :] = z[:, c * LANES:(c + 1) * LANES]
        gate_ref[rows, :] = jnp.dot(u, wgate_ref[...], preferred_element_type=F32)
        for k in range(groups):
            cols = slice(k * UP_TILE, (k + 1) * UP_TILE)
            rest_ref[rows, cols] = jnp.dot(u, wrest_ref[:, cols], preferred_element_type=F32).astype(BF16)
            for r in residues[k * share:(k + 1) * share]:
                by_residue(part, r)
        if part + 1 < ROW_PARTS:
            u = _modulated(x_ref[_part(x_ref, part + 1), :], mod_ref, prew_ref, 1)


def _outproj_kernel(x_ref, att_ref, pool_ref, hm_ref, mod_ref, postw_ref, wa_ref, wp_ref, wh_ref, o_ref):
    gate = postw_ref[1:2, :] * mod_ref[5:6, :]
    ys = []
    for part in range(ROW_PARTS):
        rows = _part(x_ref, part)
        y = jnp.dot(att_ref[rows, :], wa_ref[...], preferred_element_type=F32)
        y = y + jnp.dot(pool_ref[rows, :], wp_ref[...], preferred_element_type=F32)
        ys.append(y + jnp.dot(hm_ref[rows, :], wh_ref[...], preferred_element_type=F32))
    for part in range(ROW_PARTS):
        rows = _part(x_ref, part)
        o_ref[rows, :] = x_ref[rows, :] + _rms(ys[part]) * gate


def _outproj_ffn_kernel(x_ref, att_ref, pool_ref, hm_ref, mod_ref, prew_ref, postw_ref,
                        wa_ref, wp_ref, wh_ref, wup_ref, wdown_ref, o_ref):
    sub = 2
    mix_gate = postw_ref[1:2, :] * mod_ref[5:6, :]
    ffn_gate = postw_ref[sub:sub + 1, :] * (MACARON_WEIGHT * mod_ref[3 * sub + 2:3 * sub + 3, :])
    ff = wdown_ref.shape[0]

    def mixed(part):
        rows = _part(x_ref, part, FFN_PARTS)
        y = jnp.dot(att_ref[rows, :], wa_ref[...], preferred_element_type=F32)
        y = y + jnp.dot(pool_ref[rows, :], wp_ref[...], preferred_element_type=F32)
        y = y + jnp.dot(hm_ref[rows, :], wh_ref[...], preferred_element_type=F32)
        return x_ref[rows, :] + _rms(y) * mix_gate

    h = mixed(0)
    for part in range(FFN_PARTS):
        rows = _part(x_ref, part, FFN_PARTS)
        u = _modulated(h, mod_ref, prew_ref, sub)
        acts = []
        for c in range(ff // UP_TILE):
            g = jnp.dot(u, wup_ref[:, c * UP_TILE:(c + 1) * UP_TILE], preferred_element_type=F32)
            v = jnp.dot(u, wup_ref[:, ff + c * UP_TILE:ff + (c + 1) * UP_TILE], preferred_element_type=F32)
            acts.append((g * jax.nn.sigmoid(g) * v).astype(BF16))
        h_next = mixed(part + 1) if part + 1 < FFN_PARTS else None
        y = jnp.dot(jnp.concatenate(acts, axis=1), wdown_ref[...], preferred_element_type=F32)
        o_ref[rows, :] = h + _rms(y) * ffn_gate
        h = h_next


def _resident(shape, index):
    return pl.BlockSpec(shape, index, pipeline_mode=pl.Buffered(1))


def _row_specs(d, seq):
    tm = ROW_TILE
    per_batch = seq // tm
    return (pl.BlockSpec((tm, d), lambda i: (i, 0)),
            pl.BlockSpec((None, 9, d), lambda i: (i // per_batch, 0, 0)),
            pl.BlockSpec((3, d), lambda i: (0, 0)))


def _ffn(h, mod, pre_w, post_w, w_up, w_down, layer, slot, seq):
    t, d = h.shape
    ff = w_down.shape[2]
    rows, mods, gains = _row_specs(d, seq)
    return pl.pallas_call(
        functools.partial(_ffn_kernel, sub=2 * slot),
        out_shape=jax.ShapeDtypeStruct((t, d), F32),
        grid=(t // ROW_TILE,),
        in_specs=[
            rows, mods, gains, gains,
            _resident((None, None, d, 2 * ff), lambda i: (layer, slot, 0, 0)),
            _resident((None, None, ff, d), lambda i: (layer, slot, 0, 0)),
        ],
        out_specs=rows,
        compiler_params=_params(("parallel",)),
        name="ffn",
    )(h, mod, pre_w, post_w, w_up, w_down)


def _inproj(h, mod, pre_w, w_qkv, w_rest, w_gate, layer, batch, seq):
    t, d = h.shape
    tm = ROW_TILE
    per_batch = seq // tm
    n_qkv, n_rest, n_gate = w_qkv.shape[2], w_rest.shape[2], w_gate.shape[2]
    rows, mods, gains = _row_specs(d, seq)
    by_residue = lambda i: (i // per_batch, 0, i % per_batch, 0)
    qkv_shapes = [jax.ShapeDtypeStruct((batch, dil, seq // dil, n_qkv), BF16) for dil in DILATIONS]
    qkv_specs = [pl.BlockSpec((None, dil, tm // dil, n_qkv), by_residue) for dil in DILATIONS]
    return pl.pallas_call(
        _inproj_kernel,
        out_shape=(
            *qkv_shapes,
            jax.ShapeDtypeStruct((t, n_rest), BF16),
            jax.ShapeDtypeStruct((t, n_gate), F32),
        ),
        grid=(t // tm,),
        in_specs=[
            rows, mods, gains,
            _resident((None, d, n_qkv), lambda i: (layer, 0, 0)),
            _resident((None, d, n_rest), lambda i: (layer, 0, 0)),
            _resident((None, d, n_gate), lambda i: (layer, 0, 0)),
        ],
        out_specs=(
            *qkv_specs,
            pl.BlockSpec((tm, n_rest), lambda i: (i, 0)),
            pl.BlockSpec((tm, n_gate), lambda i: (i, 0)),
        ),
        scratch_shapes=[pltpu.VMEM((n_qkv // LANES, tm, LANES), F32)] * 2,
        compiler_params=_params(("parallel",)),
        name="inproj",
    )(h, mod, pre_w, w_qkv, w_rest, w_gate)


def _outproj(h, att, pool, hm, mod, post_w, w_out, layer, seq):
    t, d = h.shape
    tm = ROW_TILE
    rows, mods, gains = _row_specs(d, seq)
    widths = (att.shape[-1], pool.shape[-1], hm.shape[-1])
    w_specs = []
    off = 0
    for wdt in widths:
        w_specs.append(_resident((None, wdt, d), functools.partial(lambda i, blk: (layer, blk, 0), blk=off // wdt)))
        off += wdt
    return pl.pallas_call(
        _outproj_kernel,
        out_shape=jax.ShapeDtypeStruct((t, d), F32),
        grid=(t // tm,),
        in_specs=[
            rows,
            pl.BlockSpec((tm, widths[0]), lambda i: (i, 0)),
            pl.BlockSpec((tm, widths[1]), lambda i: (i, 0)),
            pl.BlockSpec((tm, widths[2]), lambda i: (i, 0)),
            mods, gains,
            *w_specs,
        ],
        out_specs=rows,
        compiler_params=_params(("parallel",)),
        name="outproj",
    )(h, att.reshape(t, widths[0]), pool.reshape(t, widths[1]), hm.reshape(t, widths[2]), mod, post_w,
      w_out, w_out, w_out)


def _outproj_ffn(h, att, pool, hm, mod, pre_w, post_w, w_out, w_up, w_down, layer, seq):
    t, d = h.shape
    ff = w_down.shape[2]
    tm = ROW_TILE
    rows, mods, gains = _row_specs(d, seq)
    widths = (att.shape[-1], pool.shape[-1], hm.shape[-1])
    w_specs = []
    off = 0
    for wdt in widths:
        w_specs.append(_resident((None, wdt, d), functools.partial(lambda i, blk: (layer, blk, 0), blk=off // wdt)))
        off += wdt
    return pl.pallas_call(
        _outproj_ffn_kernel,
        out_shape=jax.ShapeDtypeStruct((t, d), F32),
        grid=(t // tm,),
        in_specs=[
            rows,
            pl.BlockSpec((tm, widths[0]), lambda i: (i, 0)),
            pl.BlockSpec((tm, widths[1]), lambda i: (i, 0)),
            pl.BlockSpec((tm, widths[2]), lambda i: (i, 0)),
            mods, gains, gains,
            *w_specs,
            _resident((None, None, d, 2 * ff), lambda i: (layer, 1, 0, 0)),
            _resident((None, None, ff, d), lambda i: (layer, 1, 0, 0)),
        ],
        out_specs=rows,
        compiler_params=_params(("parallel",)),
        name="outproj_ffn",
    )(h, att.reshape(t, widths[0]), pool.reshape(t, widths[1]), hm.reshape(t, widths[2]), mod, pre_w, post_w,
      w_out, w_out, w_out, w_up, w_down)


QKV_W = 3 * ATT_WIDTH


def _attn_scores(q, kwin, bias, head_of_lane):
    q32 = q.astype(F32) * (ATT_HEAD_DIM ** -0.5)
    lhs = jnp.concatenate(
        [jnp.where(head_of_lane == h, q32, 0.0) for h in range(ATT_HEADS)], axis=0).astype(BF16)
    return lax.dot_general(lhs, kwin, (((1,), (1,)), ((), ())), preferred_element_type=F32) + bias


def _by_lane_half(x, low):
    nq = x.shape[0] // ATT_HEADS
    out = []
    for half in range(2):
        first = x[(2 * half) * nq:(2 * half + 1) * nq]
        second = x[(2 * half + 1) * nq:(2 * half + 2) * nq]
        if x.shape[1] != 1:
            first = first[:, half * LANES:(half + 1) * LANES]
            second = second[:, half * LANES:(half + 1) * LANES]
        out.append(jnp.where(low, first, second))
    return out


def _attn_kernel(c1_ref, h1_ref, c4_ref, h4_ref, c16_ref, h16_ref, o_ref, bias_ref, dst_ref, s_ref, p_ref, nat_ref):
    t = pl.program_id(1)
    nq = ATT_BLOCK
    items = ATT_TILE // nq
    head_of_lane = lax.broadcasted_iota(jnp.int32, (nq, ATT_WIDTH), 1) >> (ATT_HEAD_DIM.bit_length() - 1)
    low = lax.broadcasted_iota(jnp.int32, (nq, LANES), 1) < ATT_HEAD_DIM

    qi = lax.broadcasted_iota(jnp.int32, (ATT_HEADS * nq, 2 * nq), 0) & (nq - 1)
    ki = lax.broadcasted_iota(jnp.int32, (ATT_HEADS * nq, 2 * nq), 1)
    ok = (ki >= qi) & (ki <= qi + ATT_SPAN)
    bias_ref[0] = jnp.where(ok, 0.0, NEG)
    bias_ref[1] = jnp.where(ok & (ki >= nq), 0.0, NEG)
    qs = slice(0, ATT_WIDTH)
    ks = slice(ATT_WIDTH, 2 * ATT_WIDTH)
    vs = slice(2 * ATT_WIDTH, 3 * ATT_WIDTH)

    branches = tuple(zip((c1_ref, c4_ref, c16_ref), (h1_ref, h4_ref, h16_ref), DILATIONS))

    def split(dil, idx):
        blocks = items // dil
        return idx >> (blocks.bit_length() - 1), idx & (blocks - 1)

    def window(cur_ref, halo_ref, dil, idx, cols):
        r, i = split(dil, idx)
        own = cur_ref[r, pl.ds(pl.multiple_of(i * nq, nq), nq), cols]
        if items == dil:
            prev = halo_ref[r, :, cols]
        else:
            prev0 = pl.multiple_of(jnp.maximum(i - 1, 0) * nq, nq)
            prev = jnp.where(i == 0, halo_ref[r, :, cols], cur_ref[r, pl.ds(prev0, nq), cols])
        return jnp.concatenate([prev, own], axis=0)

    step = DILATIONS[1]
    assert DILATIONS == (1, step, step * step)

    def put(branch, quantity, halves, dil, idx):
        r, i = split(dil, idx)
        if dil == 1:
            rows = pl.ds(i * nq, nq)
        elif dil == step:
            rows = pl.ds(i * (nq * dil) + r, nq, stride=step)
        else:
            rows = pl.ds((r & (step - 1)) * (ATT_TILE // step) + (r >> (step.bit_length() - 1)), nq, stride=step)
        for half, piece in enumerate(halves):
            dst_ref[branch, quantity, half, rows, :] = piece

    for branch in range(len(branches)):
        s_ref[branch, 1] = jnp.zeros(s_ref.shape[2:], s_ref.dtype)
        p_ref[branch, 1] = jnp.zeros(p_ref.shape[2:], p_ref.dtype)

    def trip(j, slot):
        other = 1 - slot
        item_a = jnp.minimum(j, items - 1)
        item_b = jnp.clip(j - 1, 0, items - 1)
        item_c = jnp.clip(j - 2, 0, items - 1)

        for branch, (cur_ref, halo_ref, dil) in enumerate(branches):
            pv = jnp.dot(p_ref[branch, other], window(cur_ref, halo_ref, dil, item_c, vs),
                         preferred_element_type=F32)
            put(branch, 0, _by_lane_half(pv, low), dil, item_c)

        for branch, (cur_ref, halo_ref, dil) in enumerate(branches):
            s = s_ref[branch, other]
            m = jnp.max(s, axis=-1, keepdims=True)
            p = jnp.exp(s - m)
            l = jnp.sum(p, axis=-1, keepdims=True)
            p_ref[branch, slot] = p.astype(BF16)
            put(branch, 1, _by_lane_half(m, low), dil, item_b)
            put(branch, 2, _by_lane_half(l, low), dil, item_b)

        for branch, (cur_ref, halo_ref, dil) in enumerate(branches):
            r, i = split(dil, item_a)
            no_past = jnp.where((i == 0) & (t == 0), 1, 0)
            q = cur_ref[r, pl.ds(pl.multiple_of(i * nq, nq), nq), qs]
            s_ref[branch, slot] = _attn_scores(q, window(cur_ref, halo_ref, dil, item_a, ks),
                                               bias_ref[no_past], head_of_lane)

    def body(pair, carry):
        trip(2 * pair, 0)
        trip(2 * pair + 1, 1)
        return carry
    lax.fori_loop(0, (items + 2) // 2, body, 0)

    rows = ATT_BLOCK

    def merge(c, carry):
        sl = pl.ds(pl.multiple_of(c * rows, rows), rows)
        def get(branch, quantity, half):
            if branch < 2:
                return dst_ref[branch, quantity, half, sl, :]
            per = rows // step
            for cls in range(step):
                src = pl.ds(pl.multiple_of(cls * (ATT_TILE // step) + c * per, per), per)
                nat_ref[quantity, half, pl.ds(cls, per, stride=step), :] = dst_ref[branch, quantity, half, src, :]
            return nat_ref[quantity, half]

        for half in range(2):
            maxes = [get(branch, 1, half) for branch in range(len(branches))]
            top = functools.reduce(jnp.maximum, maxes)
            num = 0.0
            den = 0.0
            for branch, m_b in enumerate(maxes):
                e = jnp.exp(m_b - top)
                num = num + e * get(branch, 0, half)
                den = den + e * get(branch, 2, half)
            o_ref[sl, half * LANES:(half + 1) * LANES] = (num / den).astype(o_ref.dtype)
        return carry
    lax.fori_loop(0, ATT_TILE // rows, merge, 0)


def _attention(qkv_by_dilation, batch, seq):
    tiles = seq // ATT_TILE
    in_specs = []
    operands = []
    for dil, view in zip(DILATIONS, qkv_by_dilation):
        cur_rows = ATT_TILE // dil
        per_tile = cur_rows // ATT_BLOCK
        in_specs.append(pl.BlockSpec((None, dil, cur_rows, QKV_W), lambda b, t: (b, 0, t, 0)))
        in_specs.append(pl.BlockSpec(
            (None, dil, ATT_BLOCK, QKV_W),
            functools.partial(lambda b, t, per_tile: (b, 0, jnp.maximum(t * per_tile - 1, 0), 0), per_tile=per_tile)))
        operands += [view, view]
    stacked = (ATT_HEADS * ATT_BLOCK, 2 * ATT_BLOCK)
    return pl.pallas_call(
        _attn_kernel,
        out_shape=jax.ShapeDtypeStruct((batch, seq, ATT_WIDTH), BF16),
        grid=(batch, tiles),
        in_specs=in_specs,
        out_specs=pl.BlockSpec((None, ATT_TILE, ATT_WIDTH), lambda b, t: (b, t, 0)),
        scratch_shapes=[
            pltpu.VMEM((2,) + stacked, F32),
            pltpu.VMEM((len(DILATIONS), 3, 2, ATT_TILE, LANES), F32),
            pltpu.VMEM((len(DILATIONS), 2) + stacked, F32),
            pltpu.VMEM((len(DILATIONS), 2) + stacked, BF16),
            pltpu.VMEM((3, 2, ATT_BLOCK, LANES), F32),
        ],
        compiler_params=_params(("parallel", "parallel")),
        name="dilated_attention",
    )(*operands)


def _split3(x):
    hi = x.astype(BF16)
    r1 = x - hi.astype(F32)
    mid = r1.astype(BF16)
    lo = (r1 - mid.astype(F32)).astype(BF16)
    return hi, mid, lo


def _log_sigmoid(x):
    return -(jnp.maximum(-x, 0.0) + jnp.log1p(jnp.exp(-jnp.abs(x))))


def _mix_kernel(rest_ref, gate_ref, poolw_ref, pscale_ref, convw_ref, convb_ref, wqk_ref, wv_ref,
                gbias_ref, normw_ref, skip_ref, pool_ref, hm_ref,
                phist_ref, chist_ref, cstate_ref, mstate_ref, band_ref, shift_ref, upper_ref):
    s_idx = pl.program_id(1)
    n = SEQ_CHUNK
    dh = MLSTM_HEAD_DIM
    heads = MLSTM_HEADS
    group = range(SEQ_GROUP)

    @pl.when(s_idx == 0)
    def _():
        phist_ref[...] = jnp.zeros_like(phist_ref)
        chist_ref[...] = jnp.zeros_like(chist_ref)
        cstate_ref[...] = jnp.zeros_like(cstate_ref)
        mstate_ref[...] = jnp.zeros_like(mstate_ref)
        t_row = lax.broadcasted_iota(jnp.int32, (n, HIST + n), 0) + HIST
        col = lax.broadcasted_iota(jnp.int32, (n, HIST + n), 1)
        for g, wlen in enumerate(POOL_WINDOWS):
            band_ref[g * n:(g + 1) * n, :] = jnp.where((col <= t_row) & (col > t_row - wlen), 1.0, 0.0).astype(BF16)
        own_row = lax.broadcasted_iota(jnp.int32, (n, n), 0)
        own_col = lax.broadcasted_iota(jnp.int32, (n, n), 1)
        for back in range(1, MLSTM_CONV):
            shift_ref[(back - 1) * n:back * n, :] = jnp.where(own_col == own_row - back, 1.0, 0.0).astype(BF16)
        src = lax.broadcasted_iota(jnp.int32, (n, n), 0)
        dst = lax.broadcasted_iota(jnp.int32, (n, n), 1)
        upper_ref[...] = jnp.where(src <= dst, 1.0, 0.0).astype(BF16)

    lane = lax.broadcasted_iota(jnp.int32, (n, POOL_WIDTH), 1)
    pool_group = lane >> (POOL_GROUP_DIM.bit_length() - 1)
    pos = lax.broadcasted_iota(jnp.int32, (n, POOL_WIDTH), 0) + s_idx * n
    win_len = jnp.full((n, POOL_WIDTH), POOL_WINDOWS[0], jnp.int32)
    for g, wlen in enumerate(POOL_WINDOWS):
        win_len = jnp.where(pool_group == g, wlen, win_len)
    count = jnp.minimum(pos + 1, win_len).astype(F32)
    row = lax.broadcasted_iota(jnp.int32, (n, n), 0)
    col = lax.broadcasted_iota(jnp.int32, (n, n), 1)
    causal = col <= row
    lane8 = lax.broadcasted_iota(jnp.int32, (SUBLANES, n), 1)
    row8 = lax.broadcasted_iota(jnp.int32, (SUBLANES, n), 0)
    row_tile = lax.broadcasted_iota(jnp.int32, (SUBLANES, MLSTM_WIDTH), 0)
    ones = jnp.ones((n, dh), F32)

    def window_delta(b):
        p_b = rest_ref[b, :, 0:POOL_WIDTH]
        ext = jnp.concatenate([phist_ref[b].astype(BF16), p_b], axis=0)
        phist_ref[b] = p_b[n - HIST:, :].astype(F32)
        sums = jnp.dot(band_ref[...], ext, preferred_element_type=F32)
        win_sum = sums[0:n]
        for g in range(1, len(POOL_WINDOWS)):
            win_sum = jnp.where(pool_group == g, sums[g * n:(g + 1) * n], win_sum)
        return (win_sum / count - p_b.astype(F32)).astype(BF16)

    def front(b):
        xm_b = rest_ref[b, :, POOL_WIDTH:POOL_WIDTH + MLSTM_WIDTH]
        tail = chist_ref[b]
        chist_ref[b] = xm_b[n - SUBLANES:, :].astype(F32)
        shifted = jnp.dot(shift_ref[...], xm_b, preferred_element_type=F32)
        conv = xm_b.astype(F32) * convw_ref[MLSTM_CONV - 1:MLSTM_CONV, :] + convb_ref[...]
        head_rows = jnp.zeros((SUBLANES, MLSTM_WIDTH), F32)
        for back in range(1, MLSTM_CONV):
            tap = MLSTM_CONV - 1 - back
            conv = conv + shifted[(back - 1) * n:back * n] * convw_ref[tap:tap + 1, :]
            head_rows = head_rows + jnp.where(row_tile < back, pltpu.roll(tail, back, 0), 0.0) * convw_ref[tap:tap + 1, :]
        conv = conv + jnp.concatenate([head_rows, jnp.zeros((n - SUBLANES, MLSTM_WIDTH), F32)], axis=0)
        xc = conv * jax.nn.sigmoid(conv)
        return xc, xc.astype(BF16), xm_b

    def gate_rows(b):
        gates_t = (gate_ref[b] + gbias_ref[...]).T
        i_rows = gates_t[0:SUBLANES, :]
        return i_rows, _log_sigmoid(pltpu.roll(i_rows, SUBLANES - heads, 0))

    def gate_stats(b, i_rows, bcum):
        x_ib = i_rows - bcum
        b_last = bcum[:, n - 1:n]
        m_prev = mstate_ref[b, :, 0:1]
        a = b_last + x_ib
        m_new = jnp.maximum(b_last + m_prev, jnp.max(a, axis=1, keepdims=True))
        decay = jnp.exp(b_last + m_prev - m_new)
        w_state = jnp.exp(a - m_new)
        mstate_ref[b] = jnp.broadcast_to(m_new, mstate_ref.shape[1:])
        prefix_max = x_ib
        shift = 1
        while shift < n:
            prefix_max = jnp.maximum(prefix_max, jnp.where(lane8 >= shift, pltpu.roll(prefix_max, shift, 1), NEG))
            shift *= 2
        inter_log = bcum + m_prev
        m_t = jnp.maximum(inter_log, bcum + prefix_max)
        packed = jnp.concatenate(
            [jnp.where(row8 < heads, bcum - m_t, pltpu.roll(w_state, heads, 0)),
             jnp.where(row8 < heads, jnp.exp(inter_log - m_t), pltpu.roll(jnp.exp(-m_t), heads, 0)),
             jnp.zeros((LANES - 2 * SUBLANES, n), F32)], axis=0)
        return x_ib, packed.T, decay

    def head(b, h, qk, v, xc, x_ib, cols, decay):
        hs = slice(h * dh, (h + 1) * dh)
        q = (qk[:, :dh] * (dh ** -0.5)).astype(BF16)
        k = qk[:, dh:].astype(BF16)
        v_aug = jnp.concatenate([v, ones], axis=1)

        s = lax.dot_general(q, k, (((1,), (1,)), ((), ())), preferred_element_type=F32)
        log_d = jnp.where(causal, cols[:, h:h + 1] + x_ib[h:h + 1, :], NEG)
        sw = (s * jnp.exp(log_d)).astype(BF16)
        c_prev = cstate_ref[b, h]
        out = cols[:, 2 * heads + h:2 * heads + h + 1] * jnp.dot(q, c_prev.astype(BF16), preferred_element_type=F32) \
            + jnp.dot(sw, v_aug.astype(BF16), preferred_element_type=F32)
        hh = out[:, :dh] / jnp.maximum(jnp.abs(out[:, dh:]), cols[:, 3 * heads + h:3 * heads + h + 1])

        wv_state = (cols[:, heads + h:heads + h + 1] * v_aug).astype(BF16)
        d_c = lax.dot_general(k, wv_state, (((0,), (0,)), ((), ())), preferred_element_type=F32)
        cstate_ref[b, h] = decay[h:h + 1, :] * c_prev + d_c

        og = rest_ref[b, :, POOL_WIDTH + MLSTM_WIDTH + h * dh:POOL_WIDTH + MLSTM_WIDTH + (h + 1) * dh].astype(F32)
        hh = hh * jax.nn.sigmoid(og)
        hh = _rms(hh) * normw_ref[:, hs]
        hm_ref[b, :, hs] = (hh + skip_ref[:, hs] * xc[:, hs]).astype(hm_ref.dtype)

    def rows_of(stacked, b, rows):
        return stacked[b * rows:(b + 1) * rows]

    raw, deltas = [], []
    for b in group:
        deltas.append(window_delta(b))
        raw.append(gate_rows(b))
    pooled = jnp.dot(jnp.concatenate(deltas, axis=0), poolw_ref[...], preferred_element_type=F32) * pscale_ref[...]
    for b in group:
        pool_ref[b] = rows_of(pooled, b, n).astype(pool_ref.dtype)
    parts = [part for _, logf in raw for part in _split3(logf)]
    cums = jnp.dot(jnp.concatenate(parts, axis=0), upper_ref[...], preferred_element_type=F32)
    gated, fronts = [], []
    for b in group:
        fronts.append(front(b))
        piece = rows_of(cums, b, 3 * SUBLANES)
        bcum = piece[0:SUBLANES] + piece[SUBLANES:2 * SUBLANES] + piece[2 * SUBLANES:3 * SUBLANES]
        gated.append(gate_stats(b, raw[b][0], bcum))

    for h in range(heads):
        hs = slice(h * dh, (h + 1) * dh)
        qk_all = jnp.dot(jnp.concatenate([fronts[b][1][:, hs] for b in group], axis=0), wqk_ref[h],
                         preferred_element_type=F32)
        v_all = jnp.dot(jnp.concatenate([fronts[b][2][:, hs] for b in group], axis=0), wv_ref[h],
                        preferred_element_type=F32)
        for b in group:
            x_ib, cols, state_decay = gated[b]
            head(b, h, rows_of(qk_all, b, n), rows_of(v_all, b, n), fronts[b][0], x_ib, cols, state_decay)


def _mix(rest, gates, pool_bd, pool_scale, conv_w, conv_b, w_qk, w_v, gate_bias, norm_w, skip, batch, seq):
    n = SEQ_CHUNK
    sg = SEQ_GROUP
    n_rest = rest.shape[-1]
    rest = rest.reshape(batch, seq, n_rest)
    gates = gates.reshape(batch, seq, GATE_LANES)
    c2 = lambda b, s: (0, 0)
    c3 = lambda b, s: (0, 0, 0)
    chunk = lambda b, s: (b, s, 0)
    return pl.pallas_call(
        _mix_kernel,
        out_shape=(
            jax.ShapeDtypeStruct((batch, seq, POOL_WIDTH), BF16),
            jax.ShapeDtypeStruct((batch, seq, MLSTM_WIDTH), BF16),
        ),
        grid=(batch // sg, seq // n),
        in_specs=[
            pl.BlockSpec((sg, n, n_rest), chunk),
            pl.BlockSpec((sg, n, GATE_LANES), chunk),
            pl.BlockSpec(pool_bd.shape, c2),
            pl.BlockSpec(pool_scale.shape, c2),
            pl.BlockSpec(conv_w.shape, c2),
            pl.BlockSpec(conv_b.shape, c2),
            pl.BlockSpec(w_qk.shape, c3),
            pl.BlockSpec(w_v.shape, c3),
            pl.BlockSpec(gate_bias.shape, c2),
            pl.BlockSpec(norm_w.shape, c2),
            pl.BlockSpec(skip.shape, c2),
        ],
        out_specs=(
            pl.BlockSpec((sg, n, POOL_WIDTH), chunk),
            pl.BlockSpec((sg, n, MLSTM_WIDTH), chunk),
        ),
        scratch_shapes=[
            pltpu.VMEM((sg, HIST, POOL_WIDTH), F32),
            pltpu.VMEM((sg, SUBLANES, MLSTM_WIDTH), F32),
            pltpu.VMEM((sg, MLSTM_HEADS, MLSTM_HEAD_DIM, 2 * MLSTM_HEAD_DIM), F32),
            pltpu.VMEM((sg, SUBLANES, GATE_LANES), F32),
            pltpu.VMEM((len(POOL_WINDOWS) * n, HIST + n), BF16),
            pltpu.VMEM(((MLSTM_CONV - 1) * n, n), BF16),
            pltpu.VMEM((n, n), BF16),
        ],
        compiler_params=_params(("parallel", "arbitrary")),
        name="pool_mlstm",
    )(rest, gates, pool_bd, pool_scale, conv_w, conv_b, w_qk, w_v, gate_bias, norm_w, skip)


def kernel(x, c, ada_w, ada_b, pre_norm_w, post_norm_w, ffn_up, ffn_down, mix_in_w, mix_out_w, pool_w, pool_scale,
           mlstm_conv_w, mlstm_conv_b, mlstm_qkv_w, mlstm_gate_b, mlstm_norm_w, mlstm_skip):
    batch, seq, d = x.shape
    depth = ada_w.shape[0]
    assert d == ATT_WIDTH + POOL_WIDTH + MLSTM_WIDTH
    assert seq % ATT_TILE == 0 and seq % ROW_TILE == 0 and seq % SEQ_CHUNK == 0 and batch % SEQ_GROUP == 0

    up_b = ffn_up.astype(BF16)
    down_b = ffn_down.astype(BF16)
    out_b = mix_out_w.astype(BF16)
    n_main = QKV_W + POOL_WIDTH + 2 * MLSTM_WIDTH
    w_qkv = mix_in_w[:, :, :QKV_W].astype(BF16)
    w_rest = mix_in_w[:, :, QKV_W:n_main].astype(BF16)
    w_gate = jnp.pad(mix_in_w[:, :, n_main:], ((0, 0), (0, 0), (0, GATE_LANES - 2 * MLSTM_HEADS))).astype(BF16)
    gate_bias = jnp.pad(mlstm_gate_b.reshape(depth, 1, 2 * MLSTM_HEADS),
                        ((0, 0), (0, 0), (0, GATE_LANES - 2 * MLSTM_HEADS)))
    groups = len(POOL_WINDOWS)
    eye = jnp.eye(groups, dtype=pool_w.dtype)
    pool_bd = (pool_w[:, :, :, None, :] * eye[None, :, None, :, None]).reshape(depth, POOL_WIDTH, POOL_WIDTH).astype(BF16)
    w_qk = jnp.concatenate([mlstm_qkv_w[:, 0], mlstm_qkv_w[:, 1]], axis=-1).astype(BF16)
    w_v = mlstm_qkv_w[:, 2].astype(BF16)

    mod = _adaln(c, ada_w, ada_b).reshape(depth, batch, 9, d)

    h = x.reshape(batch * seq, d)
    for l in range(depth):
        h = _ffn(h, mod[l], pre_norm_w[l], post_norm_w[l], up_b, down_b, l, 0, seq)
        *qkv, rest, gates = _inproj(h, mod[l], pre_norm_w[l], w_qkv, w_rest, w_gate, l, batch, seq)
        att = _attention(qkv, batch, seq)
        pool, hm = _mix(rest, gates, pool_bd[l], pool_scale[l].reshape(1, -1), mlstm_conv_w[l],
                        mlstm_conv_b[l].reshape(1, -1), w_qk[l], w_v[l], gate_bias[l],
                        mlstm_norm_w[l].reshape(1, -1), mlstm_skip[l].reshape(1, -1), batch, seq)
        h = _outproj_ffn(h, att, pool, hm, mod[l], pre_norm_w[l], post_norm_w[l], out_b, up_b, down_b, l, seq)
    return h.reshape(batch, seq, d)
```

```python
import functools

import jax
import jax.numpy as jnp
from jax import lax
from jax.experimental import pallas as pl
from jax.experimental.pallas import tpu as pltpu

F32 = jnp.float32
BF16 = jnp.bfloat16

ATT_HEADS = 4
ATT_HEAD_DIM = 64
ATT_WIDTH = ATT_HEADS * ATT_HEAD_DIM
DILATIONS = (1, 4, 16)
ATT_SPAN = 128
POOL_WINDOWS = (2, 4, 8, 16)
POOL_GROUP_DIM = 64
POOL_WIDTH = len(POOL_WINDOWS) * POOL_GROUP_DIM
MLSTM_HEADS = 4
MLSTM_HEAD_DIM = 128
MLSTM_WIDTH = MLSTM_HEADS * MLSTM_HEAD_DIM
MLSTM_CONV = 4
MACARON_WEIGHT = 0.5
EPS = 1e-6
NEG = -1e30

LANES = 128
SUBLANES = 8
VMEM_LIMIT_BYTES = 56 * 1024 * 1024

ROW_TILE = 1024
ROW_PARTS = 2
FFN_PARTS = 2
UP_TILE = 256
ATT_TILE = 2048
ATT_BLOCK = 128
SEQ_CHUNK = 256
SEQ_GROUP = 4
HIST = 16
GATE_LANES = LANES


def _params(sem, **flags):
    return pltpu.CompilerParams(dimension_semantics=sem, vmem_limit_bytes=VMEM_LIMIT_BYTES, flags=flags or None)


def _rms(x):
    return x * lax.rsqrt(jnp.mean(x * x, axis=-1, keepdims=True) + EPS)


def _adaln_kernel(c_ref, w_ref, b_ref, o_ref):
    c = c_ref[...]
    c_act = (c * jax.nn.sigmoid(c)).astype(BF16)
    o_ref[...] = jnp.dot(c_act, w_ref[...].astype(BF16), preferred_element_type=F32) + b_ref[...]


def _adaln(c, ada_w, ada_b):
    depth, d, n = ada_w.shape
    b = c.shape[0]
    tn = n // 8
    return pl.pallas_call(
        _adaln_kernel,
        out_shape=jax.ShapeDtypeStruct((depth, b, n), F32),
        grid=(depth, n // tn),
        in_specs=[
            pl.BlockSpec((b, d), lambda l, j: (0, 0)),
            pl.BlockSpec((None, d, tn), lambda l, j: (l, 0, j)),
            pl.BlockSpec((None, 1, tn), lambda l, j: (l, 0, j)),
        ],
        out_specs=pl.BlockSpec((None, b, tn), lambda l, j: (l, 0, j)),
        compiler_params=_params(("parallel", "parallel")),
        name="adaln",
    )(c, ada_w, ada_b.reshape(depth, 1, n))


def _modulated(x, mod_ref, prew_ref, sub):
    scale = prew_ref[sub:sub + 1, :] * (1.0 + mod_ref[3 * sub + 1:3 * sub + 2, :])
    return (_rms(x) * scale + mod_ref[3 * sub:3 * sub + 1, :]).astype(BF16)


def _part(ref, part, parts=None):
    rows = ref.shape[0] // (parts or ROW_PARTS)
    return slice(part * rows, (part + 1) * rows)


def _ffn_kernel(x_ref, mod_ref, prew_ref, postw_ref, wup_ref, wdown_ref, o_ref, *, sub):
    gate = postw_ref[sub:sub + 1, :] * (MACARON_WEIGHT * mod_ref[3 * sub + 2:3 * sub + 3, :])
    u = _modulated(x_ref[_part(x_ref, 0, FFN_PARTS), :], mod_ref, prew_ref, sub)
    for part in range(FFN_PARTS):
        rows = _part(x_ref, part, FFN_PARTS)
        ff = wdown_ref.shape[0]
        acts = []
        for c in range(ff // UP_TILE):
            g = jnp.dot(u, wup_ref[:, c * UP_TILE:(c + 1) * UP_TILE], preferred_element_type=F32)
            v = jnp.dot(u, wup_ref[:, ff + c * UP_TILE:ff + (c + 1) * UP_TILE], preferred_element_type=F32)
            acts.append((g * jax.nn.sigmoid(g) * v).astype(BF16))
        if part + 1 < FFN_PARTS:
            u = _modulated(x_ref[_part(x_ref, part + 1, FFN_PARTS), :], mod_ref, prew_ref, sub)
        a = jnp.concatenate(acts, axis=1)
        y = jnp.dot(a, wdown_ref[...], preferred_element_type=F32)
        o_ref[rows, :] = x_ref[rows, :] + _rms(y) * gate


def _inproj_kernel(x_ref, mod_ref, prew_ref, wqkv_ref, wrest_ref, wgate_ref,
                   qkv1_ref, qkv4_ref, qkv16_ref, rest_ref, gate_ref, z_ref, y_ref):
    slabs = z_ref.shape[0]
    step = DILATIONS[1]
    assert DILATIONS[2] == step * step

    def by_residue(part, r):
        rows = _part(x_ref, part)
        n = rows.stop - rows.start
        per, per2 = n // step, n // (step * step)
        base = rows.start + r * per
        for c in range(slabs):
            lanes = slice(c * LANES, (c + 1) * LANES)
            picked = z_ref[c, pl.ds(rows.start + r, per, stride=step), :]
            y_ref[c, base:base + per, :] = picked
            qkv4_ref[r, part * per:(part + 1) * per, lanes] = picked.astype(BF16)
        for q in range(step):
            for c in range(slabs):
                lanes = slice(c * LANES, (c + 1) * LANES)
                qkv16_ref[r + step * q, part * per2:(part + 1) * per2, lanes] = (
                    y_ref[c, pl.ds(base + q, per2, stride=step), :].astype(BF16))

    n_rest = rest_ref.shape[1]
    groups = n_rest // UP_TILE
    residues = list(range(step))
    share = -(-len(residues) // groups)
    u = _modulated(x_ref[_part(x_ref, 0), :], mod_ref, prew_ref, 1)
    for part in range(ROW_PARTS):
        rows = _part(x_ref, part)
        z = jnp.dot(u, wqkv_ref[...], preferred_element_type=F32)
        qkv1_ref[0, rows, :] = z.astype(BF16)
        for c in range(slabs):
            z_ref[c, rows, :] = z[:, c * LANES:(c + 1) * LANES]
        gate_ref[rows, :] = jnp.dot(u, wgate_ref[...], preferred_element_type=F32)
        for k in range(groups):
            cols = slice(k * UP_TILE, (k + 1) * UP_TILE)
            rest_ref[rows, cols] = jnp.dot(u, wrest_ref[:, cols], preferred_element_type=F32).astype(BF16)
            for r in residues[k * share:(k + 1) * share]:
                by_residue(part, r)
        if part + 1 < ROW_PARTS:
            u = _modulated(x_ref[_part(x_ref, part + 1), :], mod_ref, prew_ref, 1)


def _outproj_ffn_kernel(x_ref, att_ref, pool_ref, hm_ref, mod_ref, prew_ref, postw_ref,
                        wa_ref, wp_ref, wh_ref, wup_ref, wdown_ref, o_ref):
    sub = 2
    mix_gate = postw_ref[1:2, :] * mod_ref[5:6, :]
    ffn_gate = postw_ref[sub:sub + 1, :] * (MACARON_WEIGHT * mod_ref[3 * sub + 2:3 * sub + 3, :])
    ff = wdown_ref.shape[0]

    def mixed(part):
        rows = _part(x_ref, part, FFN_PARTS)
        y = jnp.dot(att_ref[rows, :], wa_ref[...], preferred_element_type=F32)
        y = y + jnp.dot(pool_ref[rows, :], wp_ref[...], preferred_element_type=F32)
        y = y + jnp.dot(hm_ref[rows, :], wh_ref[...], preferred_element_type=F32)
        return x_ref[rows, :] + _rms(y) * mix_gate

    h = mixed(0)
    for part in range(FFN_PARTS):
        rows = _part(x_ref, part, FFN_PARTS)
        u = _modulated(h, mod_ref, prew_ref, sub)
        acts = []
        for c in range(ff // UP_TILE):
            g = jnp.dot(u, wup_ref[:, c * UP_TILE:(c + 1) * UP_TILE], preferred_element_type=F32)
            v = jnp.dot(u, wup_ref[:, ff + c * UP_TILE:ff + (c + 1) * UP_TILE], preferred_element_type=F32)
            acts.append((g * jax.nn.sigmoid(g) * v).astype(BF16))
        h_next = mixed(part + 1) if part + 1 < FFN_PARTS else None
        y = jnp.dot(jnp.concatenate(acts, axis=1), wdown_ref[...], preferred_element_type=F32)
        o_ref[rows, :] = h + _rms(y) * ffn_gate
        h = h_next


def _resident(shape, index):
    return pl.BlockSpec(shape, index, pipeline_mode=pl.Buffered(1))


def _row_specs(d, seq):
    tm = ROW_TILE
    per_batch = seq // tm
    return (pl.BlockSpec((tm, d), lambda i: (i, 0)),
            pl.BlockSpec((None, 9, d), lambda i: (i // per_batch, 0, 0)),
            pl.BlockSpec((3, d), lambda i: (0, 0)))


def _ffn(h, mod, pre_w, post_w, w_up, w_down, layer, slot, seq):
    t, d = h.shape
    ff = w_down.shape[2]
    rows, mods, gains = _row_specs(d, seq)
    return pl.pallas_call(
        functools.partial(_ffn_kernel, sub=2 * slot),
        out_shape=jax.ShapeDtypeStruct((t, d), F32),
        grid=(t // ROW_TILE,),
        in_specs=[
            rows, mods, gains, gains,
            _resident((None, None, d, 2 * ff), lambda i: (layer, slot, 0, 0)),
            _resident((None, None, ff, d), lambda i: (layer, slot, 0, 0)),
        ],
        out_specs=rows,
        compiler_params=_params(("parallel",)),
        name="ffn",
    )(h, mod, pre_w, post_w, w_up, w_down)


def _inproj(h, mod, pre_w, w_qkv, w_rest, w_gate, layer, batch, seq):
    t, d = h.shape
    tm = ROW_TILE
    per_batch = seq // tm
    n_qkv, n_rest, n_gate = w_qkv.shape[2], w_rest.shape[2], w_gate.shape[2]
    rows, mods, gains = _row_specs(d, seq)
    by_residue = lambda i: (i // per_batch, 0, i % per_batch, 0)
    qkv_shapes = [jax.ShapeDtypeStruct((batch, dil, seq // dil, n_qkv), BF16) for dil in DILATIONS]
    qkv_specs = [pl.BlockSpec((None, dil, tm // dil, n_qkv), by_residue) for dil in DILATIONS]
    return pl.pallas_call(
        _inproj_kernel,
        out_shape=(
            *qkv_shapes,
            jax.ShapeDtypeStruct((t, n_rest), BF16),
            jax.ShapeDtypeStruct((t, n_gate), F32),
        ),
        grid=(t // tm,),
        in_specs=[
            rows, mods, gains,
            _resident((None, d, n_qkv), lambda i: (layer, 0, 0)),
            _resident((None, d, n_rest), lambda i: (layer, 0, 0)),
            _resident((None, d, n_gate), lambda i: (layer, 0, 0)),
        ],
        out_specs=(
            *qkv_specs,
            pl.BlockSpec((tm, n_rest), lambda i: (i, 0)),
            pl.BlockSpec((tm, n_gate), lambda i: (i, 0)),
        ),
        scratch_shapes=[pltpu.VMEM((n_qkv // LANES, tm, LANES), F32)] * 2,
        compiler_params=_params(("parallel",)),
        name="inproj",
    )(h, mod, pre_w, w_qkv, w_rest, w_gate)


def _outproj_ffn(h, att, pool, hm, mod, pre_w, post_w, w_out, w_up, w_down, layer, seq):
    t, d = h.shape
    ff = w_down.shape[2]
    tm = ROW_TILE
    rows, mods, gains = _row_specs(d, seq)
    widths = (att.shape[-1], pool.shape[-1], hm.shape[-1])
    w_specs = []
    off = 0
    for wdt in widths:
        w_specs.append(_resident((None, wdt, d), functools.partial(lambda i, blk: (layer, blk, 0), blk=off // wdt)))
        off += wdt
    return pl.pallas_call(
        _outproj_ffn_kernel,
        out_shape=jax.ShapeDtypeStruct((t, d), F32),
        grid=(t // tm,),
        in_specs=[
            rows,
            pl.BlockSpec((tm, widths[0]), lambda i: (i, 0)),
            pl.BlockSpec((tm, widths[1]), lambda i: (i, 0)),
            pl.BlockSpec((tm, widths[2]), lambda i: (i, 0)),
            mods, gains, gains,
            *w_specs,
            _resident((None, None, d, 2 * ff), lambda i: (layer, 1, 0, 0)),
            _resident((None, None, ff, d), lambda i: (layer, 1, 0, 0)),
        ],
        out_specs=rows,
        compiler_params=_params(("parallel",)),
        name="outproj_ffn",
    )(h, att.reshape(t, widths[0]), pool.reshape(t, widths[1]), hm.reshape(t, widths[2]), mod, pre_w, post_w,
      w_out, w_out, w_out, w_up, w_down)


QKV_W = 3 * ATT_WIDTH


def _attn_scores(q, kwin, bias, head_of_lane):
    q32 = q.astype(F32) * (ATT_HEAD_DIM ** -0.5)
    lhs = jnp.concatenate(
        [jnp.where(head_of_lane == h, q32, 0.0) for h in range(ATT_HEADS)], axis=0).astype(BF16)
    return lax.dot_general(lhs, kwin, (((1,), (1,)), ((), ())), preferred_element_type=F32) + bias


def _by_lane_half(x, low):
    nq = x.shape[0] // ATT_HEADS
    out = []
    for half in range(2):
        first = x[(2 * half) * nq:(2 * half + 1) * nq]
        second = x[(2 * half + 1) * nq:(2 * half + 2) * nq]
        if x.shape[1] != 1:
            first = first[:, half * LANES:(half + 1) * LANES]
            second = second[:, half * LANES:(half + 1) * LANES]
        out.append(jnp.where(low, first, second))
    return out


def _attn_kernel(c1_ref, h1_ref, c4_ref, h4_ref, c16_ref, h16_ref, o_ref, bias_ref, dst_ref, s_ref, p_ref, nat_ref):
    t = pl.program_id(1)
    nq = ATT_BLOCK
    items = ATT_TILE // nq
    head_of_lane = lax.broadcasted_iota(jnp.int32, (nq, ATT_WIDTH), 1) >> (ATT_HEAD_DIM.bit_length() - 1)
    low = lax.broadcasted_iota(jnp.int32, (nq, LANES), 1) < ATT_HEAD_DIM

    qi = lax.broadcasted_iota(jnp.int32, (ATT_HEADS * nq, 2 * nq), 0) & (nq - 1)
    ki = lax.broadcasted_iota(jnp.int32, (ATT_HEADS * nq, 2 * nq), 1)
    ok = (ki >= qi) & (ki <= qi + ATT_SPAN)
    bias_ref[0] = jnp.where(ok, 0.0, NEG)
    bias_ref[1] = jnp.where(ok & (ki >= nq), 0.0, NEG)
    qs = slice(0, ATT_WIDTH)
    ks = slice(ATT_WIDTH, 2 * ATT_WIDTH)
    vs = slice(2 * ATT_WIDTH, 3 * ATT_WIDTH)

    branches = tuple(zip((c1_ref, c4_ref, c16_ref), (h1_ref, h4_ref, h16_ref), DILATIONS))

    def split(dil, idx):
        blocks = items // dil
        return idx >> (blocks.bit_length() - 1), idx & (blocks - 1)

    def window(cur_ref, halo_ref, dil, idx, cols):
        r, i = split(dil, idx)
        own = cur_ref[r, pl.ds(pl.multiple_of(i * nq, nq), nq), cols]
        if items == dil:
            prev = halo_ref[r, :, cols]
        else:
            prev0 = pl.multiple_of(jnp.maximum(i - 1, 0) * nq, nq)
            prev = jnp.where(i == 0, halo_ref[r, :, cols], cur_ref[r, pl.ds(prev0, nq), cols])
        return jnp.concatenate([prev, own], axis=0)

    step = DILATIONS[1]
    assert DILATIONS == (1, step, step * step)

    def put(branch, quantity, halves, dil, idx):
        r, i = split(dil, idx)
        if dil == 1:
            rows = pl.ds(pl.multiple_of(i * nq, nq), nq)
        elif dil == step:
            rows = pl.ds(pl.multiple_of(r * (ATT_TILE // step) + i * nq, nq), nq)
        else:
            rows = pl.ds((r & (step - 1)) * (ATT_TILE // step) + (r >> (step.bit_length() - 1)), nq, stride=step)
        for half, piece in enumerate(halves):
            dst_ref[branch, quantity, half, rows, :] = piece

    for branch in range(len(branches)):
        s_ref[branch, 1] = jnp.zeros(s_ref.shape[2:], s_ref.dtype)
        p_ref[branch, 1] = jnp.zeros(p_ref.shape[2:], p_ref.dtype)

    def trip(j, slot):
        other = 1 - slot
        item_a = jnp.minimum(j, items - 1)
        item_b = jnp.clip(j - 1, 0, items - 1)
        item_c = jnp.clip(j - 2, 0, items - 1)

        for branch, (cur_ref, halo_ref, dil) in enumerate(branches):
            pv = jnp.dot(p_ref[branch, other], window(cur_ref, halo_ref, dil, item_c, vs),
                         preferred_element_type=F32)
            put(branch, 0, _by_lane_half(pv, low), dil, item_c)

        for branch, (cur_ref, halo_ref, dil) in enumerate(branches):
            s = s_ref[branch, other]
            m = jnp.max(s, axis=-1, keepdims=True)
            p = jnp.exp(s - m)
            l = jnp.sum(p, axis=-1, keepdims=True)
            p_ref[branch, slot] = p.astype(BF16)
            put(branch, 1, _by_lane_half(m, low), dil, item_b)
            put(branch, 2, _by_lane_half(l, low), dil, item_b)

        for branch, (cur_ref, halo_ref, dil) in enumerate(branches):
            r, i = split(dil, item_a)
            no_past = jnp.where((i == 0) & (t == 0), 1, 0)
            q = cur_ref[r, pl.ds(pl.multiple_of(i * nq, nq), nq), qs]
            s_ref[branch, slot] = _attn_scores(q, window(cur_ref, halo_ref, dil, item_a, ks),
                                               bias_ref[no_past], head_of_lane)

    def body(pair, carry):
        trip(2 * pair, 0)
        trip(2 * pair + 1, 1)
        return carry
    lax.fori_loop(0, (items + 2) // 2, body, 0)

    rows = ATT_BLOCK
    span = rows * step

    def merge(c, carry):
        base = pl.multiple_of(c * span, span)
        for cls in range(step):
            grouped = pl.ds(pl.multiple_of(cls * (ATT_TILE // step) + c * rows, rows), rows)
            natural = pl.ds(base + cls, rows, stride=step)

            def get(branch, quantity, half):
                return dst_ref[branch, quantity, half, natural if branch == 0 else grouped, :]

            for half in range(2):
                maxes = [get(branch, 1, half) for branch in range(len(branches))]
                top = functools.reduce(jnp.maximum, maxes)
                num = 0.0
                den = 0.0
                for branch, m_b in enumerate(maxes):
                    e = jnp.exp(m_b - top)
                    num = num + e * get(branch, 0, half)
                    den = den + e * get(branch, 2, half)
                nat_ref[half, pl.ds(cls, rows, stride=step), :] = num / den
        for half in range(2):
            o_ref[pl.ds(base, span), half * LANES:(half + 1) * LANES] = nat_ref[half].astype(o_ref.dtype)
        return carry
    lax.fori_loop(0, ATT_TILE // span, merge, 0)


def _attention(qkv_by_dilation, batch, seq):
    tiles = seq // ATT_TILE
    in_specs = []
    operands = []
    for dil, view in zip(DILATIONS, qkv_by_dilation):
        cur_rows = ATT_TILE // dil
        per_tile = cur_rows // ATT_BLOCK
        in_specs.append(pl.BlockSpec((None, dil, cur_rows, QKV_W), lambda b, t: (b, 0, t, 0)))
        in_specs.append(pl.BlockSpec(
            (None, dil, ATT_BLOCK, QKV_W),
            functools.partial(lambda b, t, per_tile: (b, 0, jnp.maximum(t * per_tile - 1, 0), 0), per_tile=per_tile)))
        operands += [view, view]
    stacked = (ATT_HEADS * ATT_BLOCK, 2 * ATT_BLOCK)
    return pl.pallas_call(
        _attn_kernel,
        out_shape=jax.ShapeDtypeStruct((batch, seq, ATT_WIDTH), BF16),
        grid=(batch, tiles),
        in_specs=in_specs,
        out_specs=pl.BlockSpec((None, ATT_TILE, ATT_WIDTH), lambda b, t: (b, t, 0)),
        scratch_shapes=[
            pltpu.VMEM((2,) + stacked, F32),
            pltpu.VMEM((len(DILATIONS), 3, 2, ATT_TILE, LANES), F32),
            pltpu.VMEM((len(DILATIONS), 2) + stacked, F32),
            pltpu.VMEM((len(DILATIONS), 2) + stacked, BF16),
            pltpu.VMEM((2, ATT_BLOCK * DILATIONS[1], LANES), F32),
        ],
        compiler_params=_params(("parallel", "parallel")),
        name="dilated_attention",
    )(*operands)


def _split3(x):
    hi = x.astype(BF16)
    r1 = x - hi.astype(F32)
    mid = r1.astype(BF16)
    lo = (r1 - mid.astype(F32)).astype(BF16)
    return hi, mid, lo


def _log_sigmoid(x):
    return -(jnp.maximum(-x, 0.0) + jnp.log1p(jnp.exp(-jnp.abs(x))))


def _mix_kernel(rest_ref, gate_ref, poolw_ref, pscale_ref, convw_ref, convb_ref, wqk_ref, wv_ref,
                gbias_ref, normw_ref, skip_ref, pool_ref, hm_ref,
                phist_ref, chist_ref, cstate_ref, mstate_ref, band_ref, shift_ref, upper_ref):
    s_idx = pl.program_id(1)
    n = SEQ_CHUNK
    dh = MLSTM_HEAD_DIM
    heads = MLSTM_HEADS
    group = range(SEQ_GROUP)

    @pl.when(s_idx == 0)
    def _():
        phist_ref[...] = jnp.zeros_like(phist_ref)
        chist_ref[...] = jnp.zeros_like(chist_ref)
        cstate_ref[...] = jnp.zeros_like(cstate_ref)
        mstate_ref[...] = jnp.zeros_like(mstate_ref)
        t_row = lax.broadcasted_iota(jnp.int32, (n, HIST + n), 0) + HIST
        col = lax.broadcasted_iota(jnp.int32, (n, HIST + n), 1)
        for g, wlen in enumerate(POOL_WINDOWS):
            band_ref[g * n:(g + 1) * n, :] = jnp.where((col <= t_row) & (col > t_row - wlen), 1.0, 0.0).astype(BF16)
        own_row = lax.broadcasted_iota(jnp.int32, (n, n), 0)
        own_col = lax.broadcasted_iota(jnp.int32, (n, n), 1)
        for back in range(1, MLSTM_CONV):
            shift_ref[(back - 1) * n:back * n, :] = jnp.where(own_col == own_row - back, 1.0, 0.0).astype(BF16)
        src = lax.broadcasted_iota(jnp.int32, (n, n), 0)
        dst = lax.broadcasted_iota(jnp.int32, (n, n), 1)
        upper_ref[...] = jnp.where(src <= dst, 1.0, 0.0).astype(BF16)

    lane = lax.broadcasted_iota(jnp.int32, (n, POOL_WIDTH), 1)
    pool_group = lane >> (POOL_GROUP_DIM.bit_length() - 1)
    pos = lax.broadcasted_iota(jnp.int32, (n, POOL_WIDTH), 0) + s_idx * n
    win_len = jnp.full((n, POOL_WIDTH), POOL_WINDOWS[0], jnp.int32)
    for g, wlen in enumerate(POOL_WINDOWS):
        win_len = jnp.where(pool_group == g, wlen, win_len)
    count = jnp.minimum(pos + 1, win_len).astype(F32)
    row = lax.broadcasted_iota(jnp.int32, (n, n), 0)
    col = lax.broadcasted_iota(jnp.int32, (n, n), 1)
    causal = col <= row
    lane8 = lax.broadcasted_iota(jnp.int32, (SUBLANES, n), 1)
    row8 = lax.broadcasted_iota(jnp.int32, (SUBLANES, n), 0)
    row_tile = lax.broadcasted_iota(jnp.int32, (SUBLANES, MLSTM_WIDTH), 0)
    ones = jnp.ones((n, dh), F32)

    def window_delta(b):
        p_b = rest_ref[b, :, 0:POOL_WIDTH]
        ext = jnp.concatenate([phist_ref[b].astype(BF16), p_b], axis=0)
        phist_ref[b] = p_b[n - HIST:, :].astype(F32)
        sums = jnp.dot(band_ref[...], ext, preferred_element_type=F32)
        win_sum = sums[0:n]
        for g in range(1, len(POOL_WINDOWS)):
            win_sum = jnp.where(pool_group == g, sums[g * n:(g + 1) * n], win_sum)
        return (win_sum / count - p_b.astype(F32)).astype(BF16)

    def front(b):
        xm_b = rest_ref[b, :, POOL_WIDTH:POOL_WIDTH + MLSTM_WIDTH]
        tail = chist_ref[b]
        chist_ref[b] = xm_b[n - SUBLANES:, :].astype(F32)
        shifted = jnp.dot(shift_ref[...], xm_b, preferred_element_type=F32)
        conv = xm_b.astype(F32) * convw_ref[MLSTM_CONV - 1:MLSTM_CONV, :] + convb_ref[...]
        head_rows = jnp.zeros((SUBLANES, MLSTM_WIDTH), F32)
        for back in range(1, MLSTM_CONV):
            tap = MLSTM_CONV - 1 - back
            conv = conv + shifted[(back - 1) * n:back * n] * convw_ref[tap:tap + 1, :]
            head_rows = head_rows + jnp.where(row_tile < back, pltpu.roll(tail, back, 0), 0.0) * convw_ref[tap:tap + 1, :]
        conv = conv + jnp.concatenate([head_rows, jnp.zeros((n - SUBLANES, MLSTM_WIDTH), F32)], axis=0)
        xc = conv * jax.nn.sigmoid(conv)
        return xc, xc.astype(BF16), xm_b

    def gate_rows(b):
        gates_t = (gate_ref[b] + gbias_ref[...]).T
        i_rows = gates_t[0:SUBLANES, :]
        return i_rows, _log_sigmoid(pltpu.roll(i_rows, SUBLANES - heads, 0))

    def gate_stats(b, i_rows, bcum):
        x_ib = i_rows - bcum
        b_last = bcum[:, n - 1:n]
        m_prev = mstate_ref[b, :, 0:1]
        a = b_last + x_ib
        m_new = jnp.maximum(b_last + m_prev, jnp.max(a, axis=1, keepdims=True))
        decay = jnp.exp(b_last + m_prev - m_new)
        w_state = jnp.exp(a - m_new)
        mstate_ref[b] = jnp.broadcast_to(m_new, mstate_ref.shape[1:])
        prefix_max = x_ib
        shift = 1
        while shift < n:
            prefix_max = jnp.maximum(prefix_max, jnp.where(lane8 >= shift, pltpu.roll(prefix_max, shift, 1), NEG))
            shift *= 2
        inter_log = bcum + m_prev
        m_t = jnp.maximum(inter_log, bcum + prefix_max)
        packed = jnp.concatenate(
            [jnp.where(row8 < heads, bcum - m_t, pltpu.roll(w_state, heads, 0)),
             jnp.where(row8 < heads, jnp.exp(inter_log - m_t), pltpu.roll(jnp.exp(-m_t), heads, 0)),
             jnp.zeros((LANES - 2 * SUBLANES, n), F32)], axis=0)
        return x_ib, packed.T, decay

    def head(b, h, qk, v, xc, x_ib, cols, decay):
        hs = slice(h * dh, (h + 1) * dh)
        q = (qk[:, :dh] * (dh ** -0.5)).astype(BF16)
        k = qk[:, dh:].astype(BF16)
        v_aug = jnp.concatenate([v, ones], axis=1)

        s = lax.dot_general(q, k, (((1,), (1,)), ((), ())), preferred_element_type=F32)
        log_d = jnp.where(causal, cols[:, h:h + 1] + x_ib[h:h + 1, :], NEG)
        sw = (s * jnp.exp(log_d)).astype(BF16)
        c_prev = cstate_ref[b, h]
        out = cols[:, 2 * heads + h:2 * heads + h + 1] * jnp.dot(q, c_prev.astype(BF16), preferred_element_type=F32) \
            + jnp.dot(sw, v_aug.astype(BF16), preferred_element_type=F32)
        hh = out[:, :dh] / jnp.maximum(jnp.abs(out[:, dh:]), cols[:, 3 * heads + h:3 * heads + h + 1])

        wv_state = (cols[:, heads + h:heads + h + 1] * v_aug).astype(BF16)
        d_c = lax.dot_general(k, wv_state, (((0,), (0,)), ((), ())), preferred_element_type=F32)
        cstate_ref[b, h] = decay[h:h + 1, :] * c_prev + d_c

        og = rest_ref[b, :, POOL_WIDTH + MLSTM_WIDTH + h * dh:POOL_WIDTH + MLSTM_WIDTH + (h + 1) * dh].astype(F32)
        hh = hh * jax.nn.sigmoid(og)
        hh = _rms(hh) * normw_ref[:, hs]
        hm_ref[b, :, hs] = (hh + skip_ref[:, hs] * xc[:, hs]).astype(hm_ref.dtype)

    def rows_of(stacked, b, rows):
        return stacked[b * rows:(b + 1) * rows]

    raw, deltas = [], []
    for b in group:
        deltas.append(window_delta(b))
        raw.append(gate_rows(b))
    pooled = jnp.dot(jnp.concatenate(deltas, axis=0), poolw_ref[...], preferred_element_type=F32) * pscale_ref[...]
    for b in group:
        pool_ref[b] = rows_of(pooled, b, n).astype(pool_ref.dtype)
    parts = [part for _, logf in raw for part in _split3(logf)]
    cums = jnp.dot(jnp.concatenate(parts, axis=0), upper_ref[...], preferred_element_type=F32)
    gated, fronts = [], []
    for b in group:
        fronts.append(front(b))
        piece = rows_of(cums, b, 3 * SUBLANES)
        bcum = piece[0:SUBLANES] + piece[SUBLANES:2 * SUBLANES] + piece[2 * SUBLANES:3 * SUBLANES]
        gated.append(gate_stats(b, raw[b][0], bcum))

    for h in range(heads):
        hs = slice(h * dh, (h + 1) * dh)
        qk_all = jnp.dot(jnp.concatenate([fronts[b][1][:, hs] for b in group], axis=0), wqk_ref[h],
                         preferred_element_type=F32)
        v_all = jnp.dot(jnp.concatenate([fronts[b][2][:, hs] for b in group], axis=0), wv_ref[h],
                        preferred_element_type=F32)
        for b in group:
            x_ib, cols, state_decay = gated[b]
            head(b, h, rows_of(qk_all, b, n), rows_of(v_all, b, n), fronts[b][0], x_ib, cols, state_decay)


def _mix(rest, gates, pool_bd, pool_scale, conv_w, conv_b, w_qk, w_v, gate_bias, norm_w, skip, batch, seq):
    n = SEQ_CHUNK
    sg = SEQ_GROUP
    n_rest = rest.shape[-1]
    rest = rest.reshape(batch, seq, n_rest)
    gates = gates.reshape(batch, seq, GATE_LANES)
    c2 = lambda b, s: (0, 0)
    c3 = lambda b, s: (0, 0, 0)
    chunk = lambda b, s: (b, s, 0)
    return pl.pallas_call(
        _mix_kernel,
        out_shape=(
            jax.ShapeDtypeStruct((batch, seq, POOL_WIDTH), BF16),
            jax.ShapeDtypeStruct((batch, seq, MLSTM_WIDTH), BF16),
        ),
        grid=(batch // sg, seq // n),
        in_specs=[
            pl.BlockSpec((sg, n, n_rest), chunk),
            pl.BlockSpec((sg, n, GATE_LANES), chunk),
            pl.BlockSpec(pool_bd.shape, c2),
            pl.BlockSpec(pool_scale.shape, c2),
            pl.BlockSpec(conv_w.shape, c2),
            pl.BlockSpec(conv_b.shape, c2),
            pl.BlockSpec(w_qk.shape, c3),
            pl.BlockSpec(w_v.shape, c3),
            pl.BlockSpec(gate_bias.shape, c2),
            pl.BlockSpec(norm_w.shape, c2),
            pl.BlockSpec(skip.shape, c2),
        ],
        out_specs=(
            pl.BlockSpec((sg, n, POOL_WIDTH), chunk),
            pl.BlockSpec((sg, n, MLSTM_WIDTH), chunk),
        ),
        scratch_shapes=[
            pltpu.VMEM((sg, HIST, POOL_WIDTH), F32),
            pltpu.VMEM((sg, SUBLANES, MLSTM_WIDTH), F32),
            pltpu.VMEM((sg, MLSTM_HEADS, MLSTM_HEAD_DIM, 2 * MLSTM_HEAD_DIM), F32),
            pltpu.VMEM((sg, SUBLANES, GATE_LANES), F32),
            pltpu.VMEM((len(POOL_WINDOWS) * n, HIST + n), BF16),
            pltpu.VMEM(((MLSTM_CONV - 1) * n, n), BF16),
            pltpu.VMEM((n, n), BF16),
        ],
        compiler_params=_params(("parallel", "arbitrary")),
        name="pool_mlstm",
    )(rest, gates, pool_bd, pool_scale, conv_w, conv_b, w_qk, w_v, gate_bias, norm_w, skip)


def kernel(x, c, ada_w, ada_b, pre_norm_w, post_norm_w, ffn_up, ffn_down, mix_in_w, mix_out_w, pool_w, pool_scale,
           mlstm_conv_w, mlstm_conv_b, mlstm_qkv_w, mlstm_gate_b, mlstm_norm_w, mlstm_skip):
    batch, seq, d = x.shape
    depth = ada_w.shape[0]
    assert d == ATT_WIDTH + POOL_WIDTH + MLSTM_WIDTH
    assert seq % ATT_TILE == 0 and seq % ROW_TILE == 0 and seq % SEQ_CHUNK == 0 and batch % SEQ_GROUP == 0

    up_b = ffn_up.astype(BF16)
    down_b = ffn_down.astype(BF16)
    out_b = mix_out_w.astype(BF16)
    n_main = QKV_W + POOL_WIDTH + 2 * MLSTM_WIDTH
    w_qkv = mix_in_w[:, :, :QKV_W].astype(BF16)
    w_rest = mix_in_w[:, :, QKV_W:n_main].astype(BF16)
    w_gate = jnp.pad(mix_in_w[:, :, n_main:], ((0, 0), (0, 0), (0, GATE_LANES - 2 * MLSTM_HEADS))).astype(BF16)
    gate_bias = jnp.pad(mlstm_gate_b.reshape(depth, 1, 2 * MLSTM_HEADS),
                        ((0, 0), (0, 0), (0, GATE_LANES - 2 * MLSTM_HEADS)))
    groups = len(POOL_WINDOWS)
    eye = jnp.eye(groups, dtype=pool_w.dtype)
    pool_bd = (pool_w[:, :, :, None, :] * eye[None, :, None, :, None]).reshape(depth, POOL_WIDTH, POOL_WIDTH).astype(BF16)
    w_qk = jnp.concatenate([mlstm_qkv_w[:, 0], mlstm_qkv_w[:, 1]], axis=-1).astype(BF16)
    w_v = mlstm_qkv_w[:, 2].astype(BF16)

    mod = _adaln(c, ada_w, ada_b).reshape(depth, batch, 9, d)

    h = x.reshape(batch * seq, d)
    for l in range(depth):
        h = _ffn(h, mod[l], pre_norm_w[l], post_norm_w[l], up_b, down_b, l, 0, seq)
        *qkv, rest, gates = _inproj(h, mod[l], pre_norm_w[l], w_qkv, w_rest, w_gate, l, batch, seq)
        att = _attention(qkv, batch, seq)
        pool, hm = _mix(rest, gates, pool_bd[l], pool_scale[l].reshape(1, -1), mlstm_conv_w[l],
                        mlstm_conv_b[l].reshape(1, -1), w_qk[l], w_v[l], gate_bias[l],
                        mlstm_norm_w[l].reshape(1, -1), mlstm_skip[l].reshape(1, -1), batch, seq)
        h = _outproj_ffn(h, att, pool, hm, mod[l], pre_norm_w[l], post_norm_w[l], out_b, up_b, down_b, l, seq)
    return h.reshape(batch, seq, d)
```

```python
import functools

import jax
import jax.numpy as jnp
from jax import lax
from jax.experimental import pallas as pl
from jax.experimental.pallas import tpu as pltpu

F32 = jnp.float32
BF16 = jnp.bfloat16

ATT_HEADS = 4
ATT_HEAD_DIM = 64
ATT_WIDTH = ATT_HEADS * ATT_HEAD_DIM
DILATIONS = (1, 4, 16)
ATT_SPAN = 128
POOL_WINDOWS = (2, 4, 8, 16)
POOL_GROUP_DIM = 64
POOL_WIDTH = len(POOL_WINDOWS) * POOL_GROUP_DIM
MLSTM_HEADS = 4
MLSTM_HEAD_DIM = 128
MLSTM_WIDTH = MLSTM_HEADS * MLSTM_HEAD_DIM
MLSTM_CONV = 4
MACARON_WEIGHT = 0.5
EPS = 1e-6
NEG = -1e30

LANES = 128
SUBLANES = 8
VMEM_LIMIT_BYTES = 56 * 1024 * 1024

ROW_TILE = 1024
ROW_PARTS = 2
FFN_PARTS = 2
UP_TILE = 256
ATT_TILE = 2048
ATT_BLOCK = 128
SEQ_CHUNK = 256
SEQ_GROUP = 4
HIST = 16
GATE_LANES = LANES


def _params(sem, **flags):
    return pltpu.CompilerParams(dimension_semantics=sem, vmem_limit_bytes=VMEM_LIMIT_BYTES, flags=flags or None)


def _rms(x):
    return x * lax.rsqrt(jnp.mean(x * x, axis=-1, keepdims=True) + EPS)


def _adaln_kernel(c_ref, w_ref, b_ref, o_ref):
    c = c_ref[...]
    c_act = (c * jax.nn.sigmoid(c)).astype(BF16)
    o_ref[...] = jnp.dot(c_act, w_ref[...].astype(BF16), preferred_element_type=F32) + b_ref[...]


def _adaln(c, ada_w, ada_b):
    depth, d, n = ada_w.shape
    b = c.shape[0]
    tn = n // 8
    return pl.pallas_call(
        _adaln_kernel,
        out_shape=jax.ShapeDtypeStruct((depth, b, n), F32),
        grid=(depth, n // tn),
        in_specs=[
            pl.BlockSpec((b, d), lambda l, j: (0, 0)),
            pl.BlockSpec((None, d, tn), lambda l, j: (l, 0, j)),
            pl.BlockSpec((None, 1, tn), lambda l, j: (l, 0, j)),
        ],
        out_specs=pl.BlockSpec((None, b, tn), lambda l, j: (l, 0, j)),
        compiler_params=_params(("parallel", "parallel")),
        name="adaln",
    )(c, ada_w, ada_b.reshape(depth, 1, n))


def _modulated(x, mod_ref, prew_ref, sub):
    scale = prew_ref[sub:sub + 1, :] * (1.0 + mod_ref[3 * sub + 1:3 * sub + 2, :])
    return (_rms(x) * scale + mod_ref[3 * sub:3 * sub + 1, :]).astype(BF16)


def _part(ref, part, parts=None):
    rows = ref.shape[0] // (parts or ROW_PARTS)
    return slice(part * rows, (part + 1) * rows)


def _ffn_kernel(x_ref, mod_ref, prew_ref, postw_ref, wup_ref, wdown_ref, o_ref, *, sub):
    gate = postw_ref[sub:sub + 1, :] * (MACARON_WEIGHT * mod_ref[3 * sub + 2:3 * sub + 3, :])
    u = _modulated(x_ref[_part(x_ref, 0, FFN_PARTS), :], mod_ref, prew_ref, sub)
    for part in range(FFN_PARTS):
        rows = _part(x_ref, part, FFN_PARTS)
        ff = wdown_ref.shape[0]
        acts = []
        for c in range(ff // UP_TILE):
            g = jnp.dot(u, wup_ref[:, c * UP_TILE:(c + 1) * UP_TILE], preferred_element_type=F32)
            v = jnp.dot(u, wup_ref[:, ff + c * UP_TILE:ff + (c + 1) * UP_TILE], preferred_element_type=F32)
            acts.append((g * jax.nn.sigmoid(g) * v).astype(BF16))
        if part + 1 < FFN_PARTS:
            u = _modulated(x_ref[_part(x_ref, part + 1, FFN_PARTS), :], mod_ref, prew_ref, sub)
        a = jnp.concatenate(acts, axis=1)
        y = jnp.dot(a, wdown_ref[...], preferred_element_type=F32)
        o_ref[rows, :] = x_ref[rows, :] + _rms(y) * gate


def _inproj_kernel(x_ref, mod_ref, prew_ref, wqkv_ref, wrest_ref, wgate_ref,
                   qkv1_ref, qkv4_ref, qkv16_ref, rest_ref, gate_ref, z_ref, y_ref):
    slabs = z_ref.shape[0]
    step = DILATIONS[1]
    assert DILATIONS[2] == step * step

    def by_residue(part, r):
        rows = _part(x_ref, part)
        n = rows.stop - rows.start
        per, per2 = n // step, n // (step * step)
        base = rows.start + r * per
        for c in range(slabs):
            lanes = slice(c * LANES, (c + 1) * LANES)
            picked = z_ref[c, pl.ds(rows.start + r, per, stride=step), :]
            y_ref[c, base:base + per, :] = picked
            qkv4_ref[r, part * per:(part + 1) * per, lanes] = picked.astype(BF16)
        for q in range(step):
            for c in range(slabs):
                lanes = slice(c * LANES, (c + 1) * LANES)
                qkv16_ref[r + step * q, part * per2:(part + 1) * per2, lanes] = (
                    y_ref[c, pl.ds(base + q, per2, stride=step), :].astype(BF16))

    n_rest = rest_ref.shape[1]
    groups = n_rest // UP_TILE
    residues = list(range(step))
    share = -(-len(residues) // groups)
    u = _modulated(x_ref[_part(x_ref, 0), :], mod_ref, prew_ref, 1)
    for part in range(ROW_PARTS):
        rows = _part(x_ref, part)
        z = jnp.dot(u, wqkv_ref[...], preferred_element_type=F32)
        qkv1_ref[0, rows, :] = z.astype(BF16)
        for c in range(slabs):
            z_ref[c, rows, :] = z[:, c * LANES:(c + 1) * LANES]
        gate_ref[rows, :] = jnp.dot(u, wgate_ref[...], preferred_element_type=F32)
        for k in range(groups):
            cols = slice(k * UP_TILE, (k + 1) * UP_TILE)
            rest_ref[rows, cols] = jnp.dot(u, wrest_ref[:, cols], preferred_element_type=F32).astype(BF16)
            for r in residues[k * share:(k + 1) * share]:
                by_residue(part, r)
        if part + 1 < ROW_PARTS:
            u = _modulated(x_ref[_part(x_ref, part + 1), :], mod_ref, prew_ref, 1)


def _outproj_ffn_kernel(x_ref, att_ref, pool_ref, hm_ref, mod_ref, prew_ref, postw_ref,
                        wa_ref, wp_ref, wh_ref, wup_ref, wdown_ref, o_ref):
    sub = 2
    mix_gate = postw_ref[1:2, :] * mod_ref[5:6, :]
    ffn_gate = postw_ref[sub:sub + 1, :] * (MACARON_WEIGHT * mod_ref[3 * sub + 2:3 * sub + 3, :])
    ff = wdown_ref.shape[0]

    def mixed(part):
        rows = _part(x_ref, part, FFN_PARTS)
        y = jnp.dot(att_ref[rows, :], wa_ref[...], preferred_element_type=F32)
        y = y + jnp.dot(pool_ref[rows, :], wp_ref[...], preferred_element_type=F32)
        y = y + jnp.dot(hm_ref[rows, :], wh_ref[...], preferred_element_type=F32)
        return x_ref[rows, :] + _rms(y) * mix_gate

    h = mixed(0)
    for part in range(FFN_PARTS):
        rows = _part(x_ref, part, FFN_PARTS)
        u = _modulated(h, mod_ref, prew_ref, sub)
        acts = []
        for c in range(ff // UP_TILE):
            g = jnp.dot(u, wup_ref[:, c * UP_TILE:(c + 1) * UP_TILE], preferred_element_type=F32)
            v = jnp.dot(u, wup_ref[:, ff + c * UP_TILE:ff + (c + 1) * UP_TILE], preferred_element_type=F32)
            acts.append((g * jax.nn.sigmoid(g) * v).astype(BF16))
        h_next = mixed(part + 1) if part + 1 < FFN_PARTS else None
        y = jnp.dot(jnp.concatenate(acts, axis=1), wdown_ref[...], preferred_element_type=F32)
        o_ref[rows, :] = h + _rms(y) * ffn_gate
        h = h_next


def _resident(shape, index):
    return pl.BlockSpec(shape, index, pipeline_mode=pl.Buffered(1))


def _row_specs(d, seq):
    tm = ROW_TILE
    per_batch = seq // tm
    return (pl.BlockSpec((tm, d), lambda i: (i, 0)),
            pl.BlockSpec((None, 9, d), lambda i: (i // per_batch, 0, 0)),
            pl.BlockSpec((3, d), lambda i: (0, 0)))


def _ffn(h, mod, pre_w, post_w, w_up, w_down, layer, slot, seq):
    t, d = h.shape
    ff = w_down.shape[2]
    rows, mods, gains = _row_specs(d, seq)
    return pl.pallas_call(
        functools.partial(_ffn_kernel, sub=2 * slot),
        out_shape=jax.ShapeDtypeStruct((t, d), F32),
        grid=(t // ROW_TILE,),
        in_specs=[
            rows, mods, gains, gains,
            _resident((None, None, d, 2 * ff), lambda i: (layer, slot, 0, 0)),
            _resident((None, None, ff, d), lambda i: (layer, slot, 0, 0)),
        ],
        out_specs=rows,
        compiler_params=_params(("parallel",)),
        name="ffn",
    )(h, mod, pre_w, post_w, w_up, w_down)


def _inproj(h, mod, pre_w, w_qkv, w_rest, w_gate, layer, batch, seq):
    t, d = h.shape
    tm = ROW_TILE
    per_batch = seq // tm
    n_qkv, n_rest, n_gate = w_qkv.shape[2], w_rest.shape[2], w_gate.shape[2]
    rows, mods, gains = _row_specs(d, seq)
    by_residue = lambda i: (i // per_batch, 0, i % per_batch, 0)
    qkv_shapes = [jax.ShapeDtypeStruct((batch, dil, seq // dil, n_qkv), BF16) for dil in DILATIONS]
    qkv_specs = [pl.BlockSpec((None, dil, tm // dil, n_qkv), by_residue) for dil in DILATIONS]
    return pl.pallas_call(
        _inproj_kernel,
        out_shape=(
            *qkv_shapes,
            jax.ShapeDtypeStruct((t, n_rest), BF16),
            jax.ShapeDtypeStruct((t, n_gate), F32),
        ),
        grid=(t // tm,),
        in_specs=[
            rows, mods, gains,
            _resident((None, d, n_qkv), lambda i: (layer, 0, 0)),
            _resident((None, d, n_rest), lambda i: (layer, 0, 0)),
            _resident((None, d, n_gate), lambda i: (layer, 0, 0)),
        ],
        out_specs=(
            *qkv_specs,
            pl.BlockSpec((tm, n_rest), lambda i: (i, 0)),
            pl.BlockSpec((tm, n_gate), lambda i: (i, 0)),
        ),
        scratch_shapes=[pltpu.VMEM((n_qkv // LANES, tm, LANES), F32)] * 2,
        compiler_params=_params(("parallel",)),
        name="inproj",
    )(h, mod, pre_w, w_qkv, w_rest, w_gate)


def _outproj_ffn(h, att, pool, hm, mod, pre_w, post_w, w_out, w_up, w_down, layer, seq):
    t, d = h.shape
    ff = w_down.shape[2]
    tm = ROW_TILE
    rows, mods, gains = _row_specs(d, seq)
    widths = (att.shape[-1], pool.shape[-1], hm.shape[-1])
    w_specs = []
    off = 0
    for wdt in widths:
        w_specs.append(_resident((None, wdt, d), functools.partial(lambda i, blk: (layer, blk, 0), blk=off // wdt)))
        off += wdt
    return pl.pallas_call(
        _outproj_ffn_kernel,
        out_shape=jax.ShapeDtypeStruct((t, d), F32),
        grid=(t // tm,),
        in_specs=[
            rows,
            pl.BlockSpec((tm, widths[0]), lambda i: (i, 0)),
            pl.BlockSpec((tm, widths[1]), lambda i: (i, 0)),
            pl.BlockSpec((tm, widths[2]), lambda i: (i, 0)),
            mods, gains, gains,
            *w_specs,
            _resident((None, None, d, 2 * ff), lambda i: (layer, 1, 0, 0)),
            _resident((None, None, ff, d), lambda i: (layer, 1, 0, 0)),
        ],
        out_specs=rows,
        compiler_params=_params(("parallel",)),
        name="outproj_ffn",
    )(h, att.reshape(t, widths[0]), pool.reshape(t, widths[1]), hm.reshape(t, widths[2]), mod, pre_w, post_w,
      w_out, w_out, w_out, w_up, w_down)


QKV_W = 3 * ATT_WIDTH


def _attn_scores(q, kwin, bias, head_of_lane):
    q32 = q.astype(F32) * (ATT_HEAD_DIM ** -0.5)
    lhs = jnp.concatenate(
        [jnp.where(head_of_lane == h, q32, 0.0) for h in range(ATT_HEADS)], axis=0).astype(BF16)
    return lax.dot_general(lhs, kwin, (((1,), (1,)), ((), ())), preferred_element_type=F32) + bias


def _by_lane_half(x, low):
    nq = x.shape[0] // ATT_HEADS
    out = []
    for half in range(2):
        first = x[(2 * half) * nq:(2 * half + 1) * nq]
        second = x[(2 * half + 1) * nq:(2 * half + 2) * nq]
        if x.shape[1] != 1:
            first = first[:, half * LANES:(half + 1) * LANES]
            second = second[:, half * LANES:(half + 1) * LANES]
        out.append(jnp.where(low, first, second))
    return out


def _attn_kernel(c1_ref, h1_ref, c4_ref, h4_ref, c16_ref, h16_ref, o_ref, bias_ref, dst_ref, s_ref, p_ref, nat_ref):
    t = pl.program_id(1)
    nq = ATT_BLOCK
    items = ATT_TILE // nq
    head_of_lane = lax.broadcasted_iota(jnp.int32, (nq, ATT_WIDTH), 1) >> (ATT_HEAD_DIM.bit_length() - 1)
    low = lax.broadcasted_iota(jnp.int32, (nq, LANES), 1) < ATT_HEAD_DIM

    qi = lax.broadcasted_iota(jnp.int32, (ATT_HEADS * nq, 2 * nq), 0) & (nq - 1)
    ki = lax.broadcasted_iota(jnp.int32, (ATT_HEADS * nq, 2 * nq), 1)
    ok = (ki >= qi) & (ki <= qi + ATT_SPAN)
    bias_ref[0] = jnp.where(ok, 0.0, NEG)
    bias_ref[1] = jnp.where(ok & (ki >= nq), 0.0, NEG)
    qs = slice(0, ATT_WIDTH)
    ks = slice(ATT_WIDTH, 2 * ATT_WIDTH)
    vs = slice(2 * ATT_WIDTH, 3 * ATT_WIDTH)

    branches = tuple(zip((c1_ref, c4_ref, c16_ref), (h1_ref, h4_ref, h16_ref), DILATIONS))

    def split(dil, idx):
        blocks = items // dil
        return idx >> (blocks.bit_length() - 1), idx & (blocks - 1)

    def window(cur_ref, halo_ref, dil, idx, cols):
        r, i = split(dil, idx)
        own = cur_ref[r, pl.ds(pl.multiple_of(i * nq, nq), nq), cols]
        if items == dil:
            prev = halo_ref[r, :, cols]
        else:
            prev0 = pl.multiple_of(jnp.maximum(i - 1, 0) * nq, nq)
            prev = jnp.where(i == 0, halo_ref[r, :, cols], cur_ref[r, pl.ds(prev0, nq), cols])
        return jnp.concatenate([prev, own], axis=0)

    step = DILATIONS[1]
    assert DILATIONS == (1, step, step * step)

    def put(branch, quantity, halves, dil, idx):
        r, i = split(dil, idx)
        if dil == 1:
            rows = pl.ds(pl.multiple_of(i * nq, nq), nq)
        elif dil == step:
            rows = pl.ds(pl.multiple_of(r * (ATT_TILE // step) + i * nq, nq), nq)
        else:
            rows = pl.ds((r & (step - 1)) * (ATT_TILE // step) + (r >> (step.bit_length() - 1)), nq, stride=step)
        for half, piece in enumerate(halves):
            dst_ref[branch, quantity, half, rows, :] = piece

    def trip(j, slot, scores=True, softmax=True, values=True):
        other = 1 - slot
        j = jnp.asarray(j, jnp.int32)
        item_a, item_b, item_c = j, j - 1, j - 2

        if values:
            for branch, (cur_ref, halo_ref, dil) in enumerate(branches):
                pv = jnp.dot(p_ref[branch, other], window(cur_ref, halo_ref, dil, item_c, vs),
                             preferred_element_type=F32)
                put(branch, 0, _by_lane_half(pv, low), dil, item_c)

        if softmax:
            for branch, (cur_ref, halo_ref, dil) in enumerate(branches):
                s = s_ref[branch, other]
                m = jnp.max(s, axis=-1, keepdims=True)
                p = jnp.exp(s - m)
                l = jnp.sum(p, axis=-1, keepdims=True)
                p_ref[branch, slot] = p.astype(BF16)
                put(branch, 1, _by_lane_half(m, low), dil, item_b)
                put(branch, 2, _by_lane_half(l, low), dil, item_b)

        if scores:
            for branch, (cur_ref, halo_ref, dil) in enumerate(branches):
                r, i = split(dil, item_a)
                no_past = jnp.where((i == 0) & (t == 0), 1, 0)
                q = cur_ref[r, pl.ds(pl.multiple_of(i * nq, nq), nq), qs]
                s_ref[branch, slot] = _attn_scores(q, window(cur_ref, halo_ref, dil, item_a, ks),
                                                   bias_ref[no_past], head_of_lane)

    assert items % 2 == 0 and items >= 4
    trip(0, 0, softmax=False, values=False)
    trip(1, 1, values=False)

    def body(pair, carry):
        trip(2 * pair, 0)
        trip(2 * pair + 1, 1)
        return carry
    lax.fori_loop(1, items // 2, body, 0)
    trip(items, 0, scores=False)
    trip(items + 1, 1, scores=False, softmax=False)

    rows = ATT_BLOCK
    span = rows * step

    def merge(c, carry):
        base = pl.multiple_of(c * span, span)
        for cls in range(step):
            grouped = pl.ds(pl.multiple_of(cls * (ATT_TILE // step) + c * rows, rows), rows)
            natural = pl.ds(base + cls, rows, stride=step)

            def get(branch, quantity, half):
                return dst_ref[branch, quantity, half, natural if branch == 0 else grouped, :]

            for half in range(2):
                maxes = [get(branch, 1, half) for branch in range(len(branches))]
                top = functools.reduce(jnp.maximum, maxes)
                num = 0.0
                den = 0.0
                for branch, m_b in enumerate(maxes):
                    e = jnp.exp(m_b - top)
                    num = num + e * get(branch, 0, half)
                    den = den + e * get(branch, 2, half)
                nat_ref[half, pl.ds(cls, rows, stride=step), :] = num / den
        for half in range(2):
            o_ref[pl.ds(base, span), half * LANES:(half + 1) * LANES] = nat_ref[half].astype(o_ref.dtype)
        return carry
    lax.fori_loop(0, ATT_TILE // span, merge, 0)


def _attention(qkv_by_dilation, batch, seq):
    tiles = seq // ATT_TILE
    in_specs = []
    operands = []
    for dil, view in zip(DILATIONS, qkv_by_dilation):
        cur_rows = ATT_TILE // dil
        per_tile = cur_rows // ATT_BLOCK
        in_specs.append(pl.BlockSpec((None, dil, cur_rows, QKV_W), lambda b, t: (b, 0, t, 0)))
        in_specs.append(pl.BlockSpec(
            (None, dil, ATT_BLOCK, QKV_W),
            functools.partial(lambda b, t, per_tile: (b, 0, jnp.maximum(t * per_tile - 1, 0), 0), per_tile=per_tile)))
        operands += [view, view]
    stacked = (ATT_HEADS * ATT_BLOCK, 2 * ATT_BLOCK)
    return pl.pallas_call(
        _attn_kernel,
        out_shape=jax.ShapeDtypeStruct((batch, seq, ATT_WIDTH), BF16),
        grid=(batch, tiles),
        in_specs=in_specs,
        out_specs=pl.BlockSpec((None, ATT_TILE, ATT_WIDTH), lambda b, t: (b, t, 0)),
        scratch_shapes=[
            pltpu.VMEM((2,) + stacked, F32),
            pltpu.VMEM((len(DILATIONS), 3, 2, ATT_TILE, LANES), F32),
            pltpu.VMEM((len(DILATIONS), 2) + stacked, F32),
            pltpu.VMEM((len(DILATIONS), 2) + stacked, BF16),
            pltpu.VMEM((2, ATT_BLOCK * DILATIONS[1], LANES), F32),
        ],
        compiler_params=_params(("parallel", "parallel")),
        name="dilated_attention",
    )(*operands)


def _split3(x):
    hi = x.astype(BF16)
    r1 = x - hi.astype(F32)
    mid = r1.astype(BF16)
    lo = (r1 - mid.astype(F32)).astype(BF16)
    return hi, mid, lo


def _log_sigmoid(x):
    return -(jnp.maximum(-x, 0.0) + jnp.log1p(jnp.exp(-jnp.abs(x))))


def _mix_kernel(rest_ref, gate_ref, poolw_ref, pscale_ref, convw_ref, convb_ref, wqk_ref, wv_ref,
                gbias_ref, normw_ref, skip_ref, pool_ref, hm_ref,
                phist_ref, chist_ref, cstate_ref, mstate_ref, band_ref, shift_ref, upper_ref):
    s_idx = pl.program_id(1)
    n = SEQ_CHUNK
    dh = MLSTM_HEAD_DIM
    heads = MLSTM_HEADS
    group = range(SEQ_GROUP)

    @pl.when(s_idx == 0)
    def _():
        phist_ref[...] = jnp.zeros_like(phist_ref)
        chist_ref[...] = jnp.zeros_like(chist_ref)
        cstate_ref[...] = jnp.zeros_like(cstate_ref)
        mstate_ref[...] = jnp.zeros_like(mstate_ref)
        t_row = lax.broadcasted_iota(jnp.int32, (n, HIST + n), 0) + HIST
        col = lax.broadcasted_iota(jnp.int32, (n, HIST + n), 1)
        for g, wlen in enumerate(POOL_WINDOWS):
            band_ref[g * n:(g + 1) * n, :] = jnp.where((col <= t_row) & (col > t_row - wlen), 1.0, 0.0).astype(BF16)
        own_row = lax.broadcasted_iota(jnp.int32, (n, n), 0)
        own_col = lax.broadcasted_iota(jnp.int32, (n, n), 1)
        for back in range(1, MLSTM_CONV):
            shift_ref[(back - 1) * n:back * n, :] = jnp.where(own_col == own_row - back, 1.0, 0.0).astype(BF16)
        src = lax.broadcasted_iota(jnp.int32, (n, n), 0)
        dst = lax.broadcasted_iota(jnp.int32, (n, n), 1)
        upper_ref[...] = jnp.where(src <= dst, 1.0, 0.0).astype(BF16)

    lane = lax.broadcasted_iota(jnp.int32, (n, POOL_WIDTH), 1)
    pool_group = lane >> (POOL_GROUP_DIM.bit_length() - 1)
    pos = lax.broadcasted_iota(jnp.int32, (n, POOL_WIDTH), 0) + s_idx * n
    win_len = jnp.full((n, POOL_WIDTH), POOL_WINDOWS[0], jnp.int32)
    for g, wlen in enumerate(POOL_WINDOWS):
        win_len = jnp.where(pool_group == g, wlen, win_len)
    count = jnp.minimum(pos + 1, win_len).astype(F32)
    row = lax.broadcasted_iota(jnp.int32, (n, n), 0)
    col = lax.broadcasted_iota(jnp.int32, (n, n), 1)
    causal = col <= row
    lane8 = lax.broadcasted_iota(jnp.int32, (SUBLANES, n), 1)
    row8 = lax.broadcasted_iota(jnp.int32, (SUBLANES, n), 0)
    row_tile = lax.broadcasted_iota(jnp.int32, (SUBLANES, MLSTM_WIDTH), 0)
    ones = jnp.ones((n, dh), F32)

    def window_delta(b):
        p_b = rest_ref[b, :, 0:POOL_WIDTH]
        ext = jnp.concatenate([phist_ref[b].astype(BF16), p_b], axis=0)
        phist_ref[b] = p_b[n - HIST:, :].astype(F32)
        sums = jnp.dot(band_ref[...], ext, preferred_element_type=F32)
        win_sum = sums[0:n]
        for g in range(1, len(POOL_WINDOWS)):
            win_sum = jnp.where(pool_group == g, sums[g * n:(g + 1) * n], win_sum)
        return (win_sum / count - p_b.astype(F32)).astype(BF16)

    def front(b):
        xm_b = rest_ref[b, :, POOL_WIDTH:POOL_WIDTH + MLSTM_WIDTH]
        tail = chist_ref[b]
        chist_ref[b] = xm_b[n - SUBLANES:, :].astype(F32)
        shifted = jnp.dot(shift_ref[...], xm_b, preferred_element_type=F32)
        conv = xm_b.astype(F32) * convw_ref[MLSTM_CONV - 1:MLSTM_CONV, :] + convb_ref[...]
        head_rows = jnp.zeros((SUBLANES, MLSTM_WIDTH), F32)
        for back in range(1, MLSTM_CONV):
            tap = MLSTM_CONV - 1 - back
            conv = conv + shifted[(back - 1) * n:back * n] * convw_ref[tap:tap + 1, :]
            head_rows = head_rows + jnp.where(row_tile < back, pltpu.roll(tail, back, 0), 0.0) * convw_ref[tap:tap + 1, :]
        conv = conv + jnp.concatenate([head_rows, jnp.zeros((n - SUBLANES, MLSTM_WIDTH), F32)], axis=0)
        xc = conv * jax.nn.sigmoid(conv)
        return xc, xc.astype(BF16), xm_b

    def gate_rows(b):
        gates_t = (gate_ref[b] + gbias_ref[...]).T
        i_rows = gates_t[0:SUBLANES, :]
        return i_rows, _log_sigmoid(pltpu.roll(i_rows, SUBLANES - heads, 0))

    def gate_stats(b, i_rows, bcum):
        x_ib = i_rows - bcum
        b_last = bcum[:, n - 1:n]
        m_prev = mstate_ref[b, :, 0:1]
        a = b_last + x_ib
        m_new = jnp.maximum(b_last + m_prev, jnp.max(a, axis=1, keepdims=True))
        decay = jnp.exp(b_last + m_prev - m_new)
        w_state = jnp.exp(a - m_new)
        mstate_ref[b] = jnp.broadcast_to(m_new, mstate_ref.shape[1:])
        prefix_max = x_ib
        shift = 1
        while shift < n:
            prefix_max = jnp.maximum(prefix_max, jnp.where(lane8 >= shift, pltpu.roll(prefix_max, shift, 1), NEG))
            shift *= 2
        inter_log = bcum + m_prev
        m_t = jnp.maximum(inter_log, bcum + prefix_max)
        packed = jnp.concatenate(
            [jnp.where(row8 < heads, bcum - m_t, pltpu.roll(w_state, heads, 0)),
             jnp.where(row8 < heads, jnp.exp(inter_log - m_t), pltpu.roll(jnp.exp(-m_t), heads, 0)),
             jnp.zeros((LANES - 2 * SUBLANES, n), F32)], axis=0)
        return x_ib, packed.T, decay

    def head(b, h, qk, v, xc, x_ib, cols, decay):
        hs = slice(h * dh, (h + 1) * dh)
        q = (qk[:, :dh] * (dh ** -0.5)).astype(BF16)
        k = qk[:, dh:].astype(BF16)
        v_aug = jnp.concatenate([v, ones], axis=1)

        s = lax.dot_general(q, k, (((1,), (1,)), ((), ())), preferred_element_type=F32)
        log_d = jnp.where(causal, cols[:, h:h + 1] + x_ib[h:h + 1, :], NEG)
        sw = (s * jnp.exp(log_d)).astype(BF16)
        c_prev = cstate_ref[b, h]
        out = cols[:, 2 * heads + h:2 * heads + h + 1] * jnp.dot(q, c_prev.astype(BF16), preferred_element_type=F32) \
            + jnp.dot(sw, v_aug.astype(BF16), preferred_element_type=F32)
        hh = out[:, :dh] / jnp.maximum(jnp.abs(out[:, dh:]), cols[:, 3 * heads + h:3 * heads + h + 1])

        wv_state = (cols[:, heads + h:heads + h + 1] * v_aug).astype(BF16)
        d_c = lax.dot_general(k, wv_state, (((0,), (0,)), ((), ())), preferred_element_type=F32)
        cstate_ref[b, h] = decay[h:h + 1, :] * c_prev + d_c

        og = rest_ref[b, :, POOL_WIDTH + MLSTM_WIDTH + h * dh:POOL_WIDTH + MLSTM_WIDTH + (h + 1) * dh].astype(F32)
        hh = hh * jax.nn.sigmoid(og)
        hh = _rms(hh) * normw_ref[:, hs]
        hm_ref[b, :, hs] = (hh + skip_ref[:, hs] * xc[:, hs]).astype(hm_ref.dtype)

    def rows_of(stacked, b, rows):
        return stacked[b * rows:(b + 1) * rows]

    raw, deltas = [], []
    for b in group:
        deltas.append(window_delta(b))
        raw.append(gate_rows(b))
    pooled = jnp.dot(jnp.concatenate(deltas, axis=0), poolw_ref[...], preferred_element_type=F32) * pscale_ref[...]
    for b in group:
        pool_ref[b] = rows_of(pooled, b, n).astype(pool_ref.dtype)
    parts = [part for _, logf in raw for part in _split3(logf)]
    cums = jnp.dot(jnp.concatenate(parts, axis=0), upper_ref[...], preferred_element_type=F32)
    gated, fronts = [], []
    for b in group:
        fronts.append(front(b))
        piece = rows_of(cums, b, 3 * SUBLANES)
        bcum = piece[0:SUBLANES] + piece[SUBLANES:2 * SUBLANES] + piece[2 * SUBLANES:3 * SUBLANES]
        gated.append(gate_stats(b, raw[b][0], bcum))

    for h in range(heads):
        hs = slice(h * dh, (h + 1) * dh)
        qk_all = jnp.dot(jnp.concatenate([fronts[b][1][:, hs] for b in group], axis=0), wqk_ref[h],
                         preferred_element_type=F32)
        v_all = jnp.dot(jnp.concatenate([fronts[b][2][:, hs] for b in group], axis=0), wv_ref[h],
                        preferred_element_type=F32)
        for b in group:
            x_ib, cols, state_decay = gated[b]
            head(b, h, rows_of(qk_all, b, n), rows_of(v_all, b, n), fronts[b][0], x_ib, cols, state_decay)


def _mix(rest, gates, pool_bd, pool_scale, conv_w, conv_b, w_qk, w_v, gate_bias, norm_w, skip, batch, seq):
    n = SEQ_CHUNK
    sg = SEQ_GROUP
    n_rest = rest.shape[-1]
    rest = rest.reshape(batch, seq, n_rest)
    gates = gates.reshape(batch, seq, GATE_LANES)
    c2 = lambda b, s: (0, 0)
    c3 = lambda b, s: (0, 0, 0)
    chunk = lambda b, s: (b, s, 0)
    return pl.pallas_call(
        _mix_kernel,
        out_shape=(
            jax.ShapeDtypeStruct((batch, seq, POOL_WIDTH), BF16),
            jax.ShapeDtypeStruct((batch, seq, MLSTM_WIDTH), BF16),
        ),
        grid=(batch // sg, seq // n),
        in_specs=[
            pl.BlockSpec((sg, n, n_rest), chunk),
            pl.BlockSpec((sg, n, GATE_LANES), chunk),
            pl.BlockSpec(pool_bd.shape, c2),
            pl.BlockSpec(pool_scale.shape, c2),
            pl.BlockSpec(conv_w.shape, c2),
            pl.BlockSpec(conv_b.shape, c2),
            pl.BlockSpec(w_qk.shape, c3),
            pl.BlockSpec(w_v.shape, c3),
            pl.BlockSpec(gate_bias.shape, c2),
            pl.BlockSpec(norm_w.shape, c2),
            pl.BlockSpec(skip.shape, c2),
        ],
        out_specs=(
            pl.BlockSpec((sg, n, POOL_WIDTH), chunk),
            pl.BlockSpec((sg, n, MLSTM_WIDTH), chunk),
        ),
        scratch_shapes=[
            pltpu.VMEM((sg, HIST, POOL_WIDTH), F32),
            pltpu.VMEM((sg, SUBLANES, MLSTM_WIDTH), F32),
            pltpu.VMEM((sg, MLSTM_HEADS, MLSTM_HEAD_DIM, 2 * MLSTM_HEAD_DIM), F32),
            pltpu.VMEM((sg, SUBLANES, GATE_LANES), F32),
            pltpu.VMEM((len(POOL_WINDOWS) * n, HIST + n), BF16),
            pltpu.VMEM(((MLSTM_CONV - 1) * n, n), BF16),
            pltpu.VMEM((n, n), BF16),
        ],
        compiler_params=_params(("parallel", "arbitrary")),
        name="pool_mlstm",
    )(rest, gates, pool_bd, pool_scale, conv_w, conv_b, w_qk, w_v, gate_bias, norm_w, skip)


def kernel(x, c, ada_w, ada_b, pre_norm_w, post_norm_w, ffn_up, ffn_down, mix_in_w, mix_out_w, pool_w, pool_scale,
           mlstm_conv_w, mlstm_conv_b, mlstm_qkv_w, mlstm_gate_b, mlstm_norm_w, mlstm_skip):
    batch, seq, d = x.shape
    depth = ada_w.shape[0]
    assert d == ATT_WIDTH + POOL_WIDTH + MLSTM_WIDTH
    assert seq % ATT_TILE == 0 and seq % ROW_TILE == 0 and seq % SEQ_CHUNK == 0 and batch % SEQ_GROUP == 0

    up_b = ffn_up.astype(BF16)
    down_b = ffn_down.astype(BF16)
    out_b = mix_out_w.astype(BF16)
    n_main = QKV_W + POOL_WIDTH + 2 * MLSTM_WIDTH
    w_qkv = mix_in_w[:, :, :QKV_W].astype(BF16)
    w_rest = mix_in_w[:, :, QKV_W:n_main].astype(BF16)
    w_gate = jnp.pad(mix_in_w[:, :, n_main:], ((0, 0), (0, 0), (0, GATE_LANES - 2 * MLSTM_HEADS))).astype(BF16)
    gate_bias = jnp.pad(mlstm_gate_b.reshape(depth, 1, 2 * MLSTM_HEADS),
                        ((0, 0), (0, 0), (0, GATE_LANES - 2 * MLSTM_HEADS)))
    groups = len(POOL_WINDOWS)
    eye = jnp.eye(groups, dtype=pool_w.dtype)
    pool_bd = (pool_w[:, :, :, None, :] * eye[None, :, None, :, None]).reshape(depth, POOL_WIDTH, POOL_WIDTH).astype(BF16)
    w_qk = jnp.concatenate([mlstm_qkv_w[:, 0], mlstm_qkv_w[:, 1]], axis=-1).astype(BF16)
    w_v = mlstm_qkv_w[:, 2].astype(BF16)

    mod = _adaln(c, ada_w, ada_b).reshape(depth, batch, 9, d)

    h = x.reshape(batch * seq, d)
    for l in range(depth):
        h = _ffn(h, mod[l], pre_norm_w[l], post_norm_w[l], up_b, down_b, l, 0, seq)
        *qkv, rest, gates = _inproj(h, mod[l], pre_norm_w[l], w_qkv, w_rest, w_gate, l, batch, seq)
        att = _attention(qkv, batch, seq)
        pool, hm = _mix(rest, gates, pool_bd[l], pool_scale[l].reshape(1, -1), mlstm_conv_w[l],
                        mlstm_conv_b[l].reshape(1, -1), w_qk[l], w_v[l], gate_bias[l],
                        mlstm_norm_w[l].reshape(1, -1), mlstm_skip[l].reshape(1, -1), batch, seq)
        h = _outproj_ffn(h, att, pool, hm, mod[l], pre_norm_w[l], post_norm_w[l], out_b, up_b, down_b, l, seq)
    return h.reshape(batch, seq, d)
```
